```python
import jax, jax.numpy as jnp
from jax import lax
import numpy as np

D_MODEL = 2048
BATCH = 8
SEQ = 4096
DEPTH = 2
DEC_BATCH = 4
DEC_SEQ = 8192
PAST_LEN = 128

GRID_W = 64
BRANCH_W = D_MODEL // 4
N_BRANCH = 3
NA_HEAD_DIM = 64
NA_HEADS = BRANCH_W // NA_HEAD_DIM
NA_ROWS = 8
NA_COLS = 16
HGRN_DK = 128
HGRN_DV = 128
HGRN_HEADS = BRANCH_W // HGRN_DK
HGRN_CHUNK = 64
CONV_CH = BRANCH_W
CONV_WIDTH = 31
D_FF = 5632
N_EXPERTS = 8
TOP_K = 2
D_EXPERT = 7168
MOE_BLOCK = 128
N_DENSE = (DEPTH + 1) // 2
N_MOE = DEPTH // 2
EPS = 1e-6
F_MIN = 1e-30
IN_COLS = 3 * BRANCH_W + 5 * BRANCH_W + 2 * CONV_CH + N_BRANCH * D_MODEL

kernel_name = 'hybrid_na_hgrn2_conformer_encoder'


def _rms_norm(x, g):
    xf = x.astype(jnp.float32)
    y = xf * lax.rsqrt(jnp.mean(xf * xf, axis=-1, keepdims=True) + EPS)
    return (y * g.astype(jnp.float32)).astype(x.dtype)


def _layer_norm(x, g, b):
    xf = x.astype(jnp.float32)
    mu = jnp.mean(xf, axis=-1, keepdims=True)
    var = jnp.mean(jnp.square(xf - mu), axis=-1, keepdims=True)
    y = (xf - mu) * lax.rsqrt(var + EPS) * g.astype(jnp.float32) + b.astype(jnp.float32)
    return y.astype(x.dtype)


def _neighbourhood_attention(q, k, v, rpb):
    B, L, H, dh = q.shape
    rows = L // GRID_W
    kr = min(NA_ROWS, rows)
    grid = (B, rows, GRID_W, H, dh)
    qg = (q * dh ** -0.5).reshape(grid)
    kg = k.reshape(grid)
    vg = v.reshape(grid)
    cols = jnp.arange(GRID_W)
    col_start = jnp.clip(cols - NA_COLS // 2, 0, GRID_W - NA_COLS)
    col_idx = col_start[:, None] + jnp.arange(NA_COLS)[None, :]
    dc = col_idx - cols[:, None] + (NA_COLS - 1)
    rpb32 = rpb.astype(jnp.float32)

    def row_fn(r):
        rs = jnp.clip(r - kr // 2, 0, rows - kr)
        q_r = lax.dynamic_index_in_dim(qg, r, axis=1, keepdims=False)
        k_win = lax.dynamic_slice_in_dim(kg, rs, kr, axis=1)[:, :, col_idx]
        v_win = lax.dynamic_slice_in_dim(vg, rs, kr, axis=1)[:, :, col_idx]
        dr = rs + jnp.arange(kr) - r + (NA_ROWS - 1)
        bias = rpb32[:, dr[None, :, None], dc[:, None, :]]
        s = jnp.einsum('bqhd,brqchd->bhqrc', q_r, k_win).astype(jnp.float32) + bias
        p = jax.nn.softmax(s.reshape(B, H, GRID_W, kr * NA_COLS), axis=-1)
        p = p.reshape(B, H, GRID_W, kr, NA_COLS).astype(v.dtype)
        return jnp.einsum('bhqrc,brqchd->bqhd', p, v_win)

    out = lax.map(row_fn, jnp.arange(rows))
    return out.transpose(1, 0, 2, 3, 4).reshape(B, L, H * dh)


def _hgrn2_bidir(q, i, f_fwd, f_bwd, og, lb, norm_g):
    B, L, _ = q.shape
    H = HGRN_HEADS
    C = HGRN_CHUNK
    f32 = jnp.float32

    def heads(t, d):
        return t.astype(f32).reshape(B, L, H, d).transpose(0, 2, 1, 3)

    def log_forget(f, lb_dir):
        fg = lb_dir + (1.0 - lb_dir) * jax.nn.sigmoid(f.astype(f32))
        return jnp.log(jnp.maximum(fg, F_MIN))

    rev = lambda t: jnp.flip(t, axis=2)
    qh = heads(q, HGRN_DK) * HGRN_DK ** -0.5
    vh = heads(i, HGRN_DV)
    lg_f = heads(log_forget(f_fwd, lb[0]), HGRN_DK)
    lg_b = heads(log_forget(f_bwd, lb[1]), HGRN_DK)
    q2 = jnp.concatenate([qh, rev(qh)], axis=1)
    v2 = jnp.concatenate([vh, rev(vh)], axis=1)
    lg2 = jnp.concatenate([lg_f, rev(lg_b)], axis=1)
    k2 = -jnp.expm1(lg2)
    nc = L // C

    def chunks(t):
        return t.reshape(B, 2 * H, nc, C, t.shape[-1]).transpose(2, 0, 1, 3, 4)

    mask = jnp.tril(jnp.ones((C, C), dtype=bool))

    def step(S, inp):
        qc, kc, vc, lgc = inp
        b = jnp.cumsum(lgc, axis=2)
        o_inter = jnp.einsum('bhtk,bhkv->bhtv', qc * jnp.exp(b), S)
        diff = b[:, :, :, None, :] - b[:, :, None, :, :]
        decay = jnp.where(mask[:, :, None], jnp.exp(jnp.minimum(diff, 0.0)), 0.0)
        A = jnp.einsum('bhtsk,bhsk->bhts', qc[:, :, :, None, :] * decay, kc)
        o = o_inter + jnp.einsum('bhts,bhsv->bhtv', A, vc)
        b_end = b[:, :, -1:, :]
        S = jnp.exp(b_end[:, :, 0, :])[..., None] * S + jnp.einsum('bhsk,bhsv->bhkv', kc * jnp.exp(b_end - b), vc)
        return S, o

    S0 = jnp.zeros((B, 2 * H, HGRN_DK, HGRN_DV), f32)
    _, o = lax.scan(step, S0, (chunks(q2), chunks(k2), chunks(v2), chunks(lg2)))
    o = o.transpose(1, 2, 0, 3, 4).reshape(B, 2 * H, L, HGRN_DV)
    o = o[:, :H] + rev(o[:, H:])
    o = o.transpose(0, 2, 1, 3)
    o = o * lax.rsqrt(jnp.mean(o * o, axis=-1, keepdims=True) + EPS)
    o = o.reshape(B, L, H * HGRN_DV) * norm_g.astype(f32) * jax.nn.silu(og.astype(f32))
    return o.astype(q.dtype)


def _conv_module(u, w, b, ln_g, ln_b):
    a, gate = jnp.split(u, 2, axis=-1)
    h = a * jax.nn.sigmoid(gate)
    h = lax.conv_general_dilated(h, w[:, None, :], window_strides=(1,),
                                 padding=[(CONV_WIDTH // 2, CONV_WIDTH // 2)],
                                 dimension_numbers=('NWC', 'WIO', 'NWC'),
                                 feature_group_count=CONV_CH) + b
    h = _layer_norm(h, ln_g, ln_b)
    return jax.nn.silu(h)


def _mixer(h, w_in, na_rpb, lb, hgrn_norm_g, conv_w, conv_b, conv_ln_g, conv_ln_b, w_branch, w_out):
    B, L, _ = h.shape
    u = h @ w_in
    sizes = [BRANCH_W] * 8 + [2 * CONV_CH, N_BRANCH * D_MODEL]
    idx = [int(s) for s in np.cumsum(sizes)[:-1]]
    na_q, na_k, na_v, hg_q, hg_i, hg_ff, hg_fb, hg_og, conv_u, gate_logits = jnp.split(u, idx, axis=-1)
    hd = lambda t: t.reshape(B, L, NA_HEADS, NA_HEAD_DIM)
    y_a = _neighbourhood_attention(hd(na_q), hd(na_k), hd(na_v), na_rpb)
    y_b = _hgrn2_bidir(hg_q, hg_i, hg_ff, hg_fb, hg_og, lb, hgrn_norm_g)
    y_c = _conv_module(conv_u, conv_w, conv_b, conv_ln_g, conv_ln_b)
    gates = jax.nn.sigmoid(gate_logits.reshape(B, L, N_BRANCH, D_MODEL))
    merged = gates[:, :, 0] * (y_a @ w_branch[0])
    merged = merged + gates[:, :, 1] * (y_b @ w_branch[1])
    merged = merged + gates[:, :, 2] * (y_c @ w_branch[2])
    return merged @ w_out


def _swiglu(h, w13, w2):
    a, g = jnp.split(h @ w13, 2, axis=-1)
    return (jax.nn.silu(a) * g) @ w2


def _moe(h, router_w, w13, w2):
    B, L, D = h.shape
    xf = h.reshape(-1, D)
    N = xf.shape[0]
    logits = (xf @ router_w).astype(jnp.float32)
    top_v, top_e = lax.top_k(logits, TOP_K)
    gate = jax.nn.softmax(top_v, axis=-1)
    e_flat = top_e.reshape(-1).astype(jnp.int32)
    w_flat = gate.reshape(-1)
    tok_flat = jnp.arange(N * TOP_K, dtype=jnp.int32) // TOP_K
    order = jnp.argsort(e_flat)
    e_sorted = e_flat[order]
    counts = jnp.zeros((N_EXPERTS,), jnp.int32).at[e_flat].add(1)
    padded = (counts + MOE_BLOCK - 1) // MOE_BLOCK * MOE_BLOCK
    pad_end = jnp.cumsum(padded)
    pad_start = pad_end - padded
    start = jnp.cumsum(counts) - counts
    rank = jnp.arange(N * TOP_K, dtype=jnp.int32) - start[e_sorted]
    dest = pad_start[e_sorted] + rank
    R = N * TOP_K + N_EXPERTS * MOE_BLOCK
    n_blk = R // MOE_BLOCK
    row_tok = jnp.zeros((R,), jnp.int32).at[dest].set(tok_flat[order])
    row_w = jnp.zeros((R,), jnp.float32).at[dest].set(w_flat[order])
    blk_e = jnp.minimum(jnp.searchsorted(pad_end, jnp.arange(n_blk) * MOE_BLOCK, side='right'), N_EXPERTS - 1)

    def block_fn(args):
        tok, e = args
        xb = xf[tok]
        a, g = jnp.split(xb @ w13[e], 2, axis=-1)
        return (jax.nn.silu(a) * g) @ w2[e]

    y = lax.map(block_fn, (row_tok.reshape(n_blk, MOE_BLOCK), blk_e))
    out = jnp.zeros((N, D), jnp.float32).at[row_tok].add(row_w[:, None] * y.reshape(R, D).astype(jnp.float32))
    return out.astype(h.dtype).reshape(B, L, D)


def _trunk(x, c, ada_w, ada_b, norm_g, w_in, na_rpb, hgrn_lb, hgrn_norm_g, conv_w, conv_b,
           conv_ln_g, conv_ln_b, w_branch, w_out, ffn_w13, ffn_w2, router_w, moe_w13, moe_w2):
    sm = jax.nn.softmax(hgrn_lb.astype(jnp.float32), axis=0)
    lb_all = jnp.clip(jnp.cumsum(sm, axis=0) - sm[0], 0.0, 1.0)
    cs = jax.nn.silu(c)
    for l in range(DEPTH):
        mod = (cs @ ada_w[l] + ada_b[l])[:, None, :]
        sh1, sc1, gt1, sh2, sc2, gt2 = jnp.split(mod, 6, axis=-1)
        h = (_rms_norm(x, norm_g[l, 0]) * (1 + sc1) + sh1).astype(x.dtype)
        y = _mixer(h, w_in[l], na_rpb[l], lb_all[l], hgrn_norm_g[l], conv_w[l], conv_b[l],
                   conv_ln_g[l], conv_ln_b[l], w_branch[l], w_out[l])
        x = x + (gt1 * _rms_norm(y, norm_g[l, 1])).astype(x.dtype)
        h = (_rms_norm(x, norm_g[l, 2]) * (1 + sc2) + sh2).astype(x.dtype)
        if l % 2 == 0:
            y = _swiglu(h, ffn_w13[l // 2], ffn_w2[l // 2])
        else:
            y = _moe(h, router_w[l // 2], moe_w13[l // 2], moe_w2[l // 2])
        x = x + (gt2 * _rms_norm(y, norm_g[l, 3])).astype(x.dtype)
    return x


def setup_inputs(seed: int = 0) -> dict:
    key = jax.random.key(seed)
    ks = jax.random.split(key, 24)
    f32 = jnp.float32
    nrm = lambda k, shape, s: jax.random.normal(k, shape, f32) * s
    D = D_MODEL
    return {
        'x_prompt': nrm(ks[0], (BATCH, SEQ, D), 1.0),
        'x_sample': nrm(ks[1], (DEC_BATCH, DEC_SEQ, D), 1.0),
        'c_prompt': nrm(ks[2], (BATCH, D), 1.0),
        'c_sample': nrm(ks[3], (DEC_BATCH, D), 1.0),
        'ada_w': nrm(ks[4], (DEPTH, D, 6 * D), 0.5 * D ** -0.5),
        'ada_b': nrm(ks[5], (DEPTH, 6 * D), 0.01),
        'norm_g': 1.0 + nrm(ks[6], (DEPTH, 4, D), 0.05),
        'w_in': nrm(ks[7], (DEPTH, D, IN_COLS), D ** -0.5),
        'na_rpb': nrm(ks[8], (DEPTH, NA_HEADS, 2 * NA_ROWS - 1, 2 * NA_COLS - 1), 0.02),
        'hgrn_lb': nrm(ks[9], (DEPTH, 2, BRANCH_W), 0.5),
        'hgrn_norm_g': 1.0 + nrm(ks[10], (DEPTH, BRANCH_W), 0.05),
        'conv_w': nrm(ks[11], (DEPTH, CONV_WIDTH, CONV_CH), CONV_WIDTH ** -0.5),
        'conv_b': nrm(ks[12], (DEPTH, CONV_CH), 0.01),
        'conv_ln_g': 1.0 + nrm(ks[13], (DEPTH, CONV_CH), 0.05),
        'conv_ln_b': nrm(ks[14], (DEPTH, CONV_CH), 0.01),
        'w_branch': nrm(ks[15], (DEPTH, N_BRANCH, BRANCH_W, D), BRANCH_W ** -0.5),
        'w_out': nrm(ks[16], (DEPTH, D, D), D ** -0.5),
        'ffn_w13': nrm(ks[17], (N_DENSE, D, 2 * D_FF), D ** -0.5),
        'ffn_w2': nrm(ks[18], (N_DENSE, D_FF, D), D_FF ** -0.5),
        'router_w': nrm(ks[19], (N_MOE, D, N_EXPERTS), D ** -0.5),
        'moe_w13': nrm(ks[20], (N_MOE, N_EXPERTS, D, 2 * D_EXPERT), D ** -0.5),
        'moe_w2': nrm(ks[21], (N_MOE, N_EXPERTS, D_EXPERT, D), D_EXPERT ** -0.5),
    }


def reference(x_prompt, x_sample, c_prompt, c_sample, ada_w, ada_b, norm_g, w_in, na_rpb, hgrn_lb,
              hgrn_norm_g, conv_w, conv_b, conv_ln_g, conv_ln_b, w_branch, w_out, ffn_w13, ffn_w2,
              router_w, moe_w13, moe_w2):
    y_prompt = _trunk(x_prompt, c_prompt, ada_w, ada_b, norm_g, w_in, na_rpb, hgrn_lb, hgrn_norm_g,
                      conv_w, conv_b, conv_ln_g, conv_ln_b, w_branch, w_out, ffn_w13, ffn_w2,
                      router_w, moe_w13, moe_w2)
    y_sample = _trunk(x_sample, c_sample, ada_w, ada_b, norm_g, w_in, na_rpb, hgrn_lb, hgrn_norm_g,
                      conv_w, conv_b, conv_ln_g, conv_ln_b, w_branch, w_out, ffn_w13, ffn_w2,
                      router_w, moe_w13, moe_w2)
    return (y_prompt, y_sample)
```

```python
import functools
import math

import numpy as np
import jax
import jax.numpy as jnp
from jax import lax
from jax.experimental import pallas as pl
from jax.experimental.pallas import tpu as pltpu

GRID_W = 64
NA_HEAD_DIM = 64
NA_ROWS = 8
NA_COLS = 16
HGRN_DK = 128
TOP_K = 2
EPS = 1e-6
F_MIN = 1e-30
NEG_BIG = -1e30

LANE = 128
SUBLANE = 8
VMEM_LIMIT_BYTES = 56 * 1024 * 1024

MXU_DTYPE = jnp.bfloat16
ACT_DTYPE = jnp.bfloat16

HGRN_CHUNK = 128
F32 = jnp.float32


def _cparams(sem, vmem=VMEM_LIMIT_BYTES):
    return pltpu.CompilerParams(dimension_semantics=sem, vmem_limit_bytes=vmem)


def _sigmoid(x):
    return 1.0 / (1.0 + jnp.exp(-x))


def _dot(a, b):
    return jnp.dot(a, b, preferred_element_type=F32)


def _dot_nt(a, b):
    return lax.dot_general(a, b, (((1,), (1,)), ((), ())), preferred_element_type=F32)


def _dot_tn(a, b):
    return lax.dot_general(a, b, (((0,), (0,)), ((), ())), preferred_element_type=F32)


def _rms(x, g):
    return x * lax.rsqrt(jnp.mean(x * x, axis=-1, keepdims=True) + EPS) * g


def _norm_mod(x, g, sh, sc):
    return _rms(x, g) * (1.0 + sc) + sh


def _tile(n, pref, mult=SUBLANE):
    if n <= pref:
        return n
    t = (pref // mult) * mult
    while t >= mult:
        if n % t == 0:
            return t
        t -= mult
    return n


def _ada_kernel(c_ref, w_ref, b_ref, o_ref):
    c = c_ref[...]
    cs = (c * _sigmoid(c)).astype(MXU_DTYPE)
    o_ref[0] = _dot(cs, w_ref[0].astype(MXU_DTYPE)) + b_ref[0]


def _ada(c_all, ada_w, ada_b):
    depth, d, n = ada_w.shape
    rows = c_all.shape[0]
    tn = _tile(n, 1024, LANE)
    return pl.pallas_call(
        _ada_kernel,
        grid=(depth, n // tn),
        in_specs=[
            pl.BlockSpec((rows, d), lambda l, j: (0, 0)),
            pl.BlockSpec((1, d, tn), lambda l, j: (l, 0, j)),
            pl.BlockSpec((1, 1, tn), lambda l, j: (l, 0, j)),
        ],
        out_specs=pl.BlockSpec((1, rows, tn), lambda l, j: (l, 0, j)),
        out_shape=jax.ShapeDtypeStruct((depth, rows, n), F32),
        compiler_params=_cparams(("parallel", "parallel")),
        name="ada",
    )(c_all, ada_w, ada_b.reshape(depth, 1, n))


def _nm_matmul_kernel(x_ref, g_ref, sh_ref, sc_ref, w_ref, o_ref, h_ref):
    @pl.when(pl.program_id(1) == 0)
    def _():
        h_ref[...] = _norm_mod(x_ref[...], g_ref[...], sh_ref[0], sc_ref[0]).astype(h_ref.dtype)

    o_ref[...] = _dot(h_ref[...], w_ref[...]).astype(o_ref.dtype)


def _nm_matmul(x, g, sh, sc, w, out_dtype, lv, tm_pref=1024, tn_pref=1024):
    t, d = x.shape
    n = w.shape[1]
    tm = _tile(lv, tm_pref)
    tn = _tile(n, tn_pref, LANE)
    return pl.pallas_call(
        _nm_matmul_kernel,
        grid=(t // tm, n // tn),
        in_specs=[
            pl.BlockSpec((tm, d), lambda i, j: (i, 0)),
            pl.BlockSpec((1, d), lambda i, j: (0, 0)),
            pl.BlockSpec((1, 1, d), lambda i, j: (i * tm // lv, 0, 0)),
            pl.BlockSpec((1, 1, d), lambda i, j: (i * tm // lv, 0, 0)),
            pl.BlockSpec((d, tn), lambda i, j: (0, j)),
        ],
        out_specs=pl.BlockSpec((tm, tn), lambda i, j: (i, j)),
        out_shape=jax.ShapeDtypeStruct((t, n), out_dtype),
        scratch_shapes=[pltpu.VMEM((tm, d), MXU_DTYPE)],
        compiler_params=_cparams(("parallel", "arbitrary")),
        name="nm_matmul",
    )(x, g, sh, sc, w)


def _na_bias_table(rpb):
    cols = np.arange(GRID_W)
    col_start = np.clip(cols - NA_COLS // 2, 0, GRID_W - NA_COLS)
    kc = np.arange(GRID_W)
    in_win = (kc[None, :] >= col_start[:, None]) & (kc[None, :] < col_start[:, None] + NA_COLS)
    dc = np.clip(kc[None, :] - cols[:, None] + (NA_COLS - 1), 0, 2 * NA_COLS - 2)
    delta = np.arange(NA_ROWS)
    j = np.arange(NA_ROWS)
    dr = j[None, :] - delta[:, None] + (NA_ROWS - 1)
    b = rpb.astype(F32)[:, dr[:, :, None, None], dc[None, None, :, :]]
    b = jnp.where(jnp.asarray(in_win)[None, None, None], b, NEG_BIG)
    b = b.transpose(1, 0, 3, 2, 4)
    return b.reshape(NA_ROWS, rpb.shape[0], GRID_W, NA_ROWS * GRID_W)


def _na_kernel(q_ref, k_ref, v_ref, bias_ref, o_ref, *, rows):
    win = NA_ROWS * GRID_W
    lane = lax.broadcasted_iota(jnp.int32, (1, LANE), 1)
    head_masks = [lane < NA_HEAD_DIM, lane >= NA_HEAD_DIM]
    scale = NA_HEAD_DIM ** -0.5

    def row_fn(r, carry):
        rs = jnp.clip(r - NA_ROWS // 2, 0, rows - NA_ROWS)
        delta = r - rs
        q2 = q_ref[pl.ds(pl.multiple_of(r * GRID_W, GRID_W), GRID_W), :]
        kw = k_ref[pl.ds(pl.multiple_of(rs * GRID_W, GRID_W), win), :]
        vw = v_ref[pl.ds(pl.multiple_of(rs * GRID_W, GRID_W), win), :]
        out = jnp.zeros((GRID_W, LANE), F32)
        for h in range(2):
            qm = jnp.where(head_masks[h], q2, jnp.zeros_like(q2))
            s = _dot_nt(qm, kw) * scale + bias_ref[delta, h]
            m = jnp.max(s, axis=-1, keepdims=True)
            e = jnp.exp(s - m)
            den = jnp.sum(e, axis=-1, keepdims=True)
            vm = jnp.where(head_masks[h], vw, jnp.zeros_like(vw))
            out = out + _dot(e.astype(MXU_DTYPE), vm) * (1.0 / den)
        o_ref[pl.ds(pl.multiple_of(r * GRID_W, GRID_W), GRID_W), :] = out.astype(o_ref.dtype)
        return carry

    lax.fori_loop(0, rows, row_fn, 0)


def _na_call(u, bias_tab, y_prev, tok_off, nseq, seq_len):
    t = u.shape[0]
    bw = bias_tab.shape[1] * NA_HEAD_DIM
    npair = bw // LANE
    rows = seq_len // GRID_W
    assert rows >= NA_ROWS and seq_len % GRID_W == 0 and tok_off % seq_len == 0
    s0 = tok_off // seq_len
    kern = functools.partial(_na_kernel, rows=rows)
    in_specs = [
        pl.BlockSpec((seq_len, LANE), lambda b, p: (s0 + b, p)),
        pl.BlockSpec((seq_len, LANE), lambda b, p: (s0 + b, npair + p)),
        pl.BlockSpec((seq_len, LANE), lambda b, p: (s0 + b, 2 * npair + p)),
        pl.BlockSpec((NA_ROWS, 2, GRID_W, NA_ROWS * GRID_W), lambda b, p: (0, p, 0, 0)),
    ]
    args = [u, u, u, bias_tab]
    aliases = {}
    if y_prev is not None:
        in_specs.append(pl.BlockSpec(memory_space=pl.ANY))
        args.append(y_prev)
        aliases = {4: 0}
        kern_fn = lambda q, k, v, b, _prev, o: kern(q, k, v, b, o)
    else:
        kern_fn = kern
    return pl.pallas_call(
        kern_fn,
        grid=(nseq, npair),
        in_specs=in_specs,
        out_specs=pl.BlockSpec((seq_len, LANE), lambda b, p: (s0 + b, p)),
        out_shape=jax.ShapeDtypeStruct((t, bw), ACT_DTYPE),
        input_output_aliases=aliases,
        compiler_params=_cparams(("parallel", "parallel")),
        name="na",
    )(*args)


def _hgrn_consts(c, reverse):
    nlev = int(math.log2(c))
    idx = np.arange(c)
    tri = (idx[None, :] <= idx[:, None]).astype(np.float32)
    mats = [tri]
    for l in range(nlev):
        m = 1 << l
        ref = (idx // (2 * m)) * (2 * m) + m - 1
        mats.append(tri[ref])
    mall = np.concatenate(mats, axis=0)
    x = idx[:, None] ^ idx[None, :]
    lvl = np.where(x > 0, np.floor(np.log2(np.maximum(x, 1))), -1).astype(np.int32)
    lvl = np.where(idx[None, :] > idx[:, None], -2, lvl)
    if reverse:
        mall = mall.reshape(nlev + 1, c, c)[:, ::-1, ::-1].reshape((nlev + 1) * c, c)
        lvl = lvl[::-1, ::-1]
    return np.ascontiguousarray(mall), np.ascontiguousarray(lvl), nlev


def _hgrn_kernel(*refs, c, nlev, nheads, reverse, final):
    if final:
        (q_ref, i_ref, f_ref, lb_ref, mall_ref, lvl_ref, og_ref, ob_ref, ng_ref, o_ref, st_ref) = refs
    else:
        (q_ref, i_ref, f_ref, lb_ref, mall_ref, lvl_ref, o_ref, st_ref) = refs

    @pl.when(pl.program_id(1) == 0)
    def _():
        st_ref[...] = jnp.zeros_like(st_ref)

    lb = lb_ref[...]
    fg = jnp.maximum(lb + (1.0 - lb) * _sigmoid(f_ref[...]), F_MIN)
    lg = jnp.log(fg)
    kk = 1.0 - fg
    mall = mall_ref[...]
    hi = lg.astype(MXU_DTYPE)
    r1 = lg - hi.astype(F32)
    mid = r1.astype(MXU_DTYPE)
    lo = (r1 - mid.astype(F32)).astype(MXU_DTYPE)
    ball = _dot(mall, hi) + _dot(mall, mid) + _dot(mall, lo)
    q = q_ref[...].astype(F32) * (HGRN_DK ** -0.5)
    v = i_ref[...].astype(MXU_DTYPE)
    lvl = lvl_ref[...]
    tot_row = 0 if reverse else c - 1
    outs = []
    for h in range(nheads):
        sl = slice(h * HGRN_DK, (h + 1) * HGRN_DK)
        bh = ball[0:c, sl]
        qh = q[:, sl]
        kh = kk[:, sl]
        vh = v[:, sl]
        a = jnp.where(lvl == -1, _dot_nt(qh.astype(MXU_DTYPE), kh.astype(MXU_DTYPE)), 0.0)
        for l in range(nlev):
            br = ball[(l + 1) * c:(l + 2) * c, sl]
            ql = (qh * jnp.exp(jnp.minimum(bh - br, 0.0))).astype(MXU_DTYPE)
            kl = (kh * jnp.exp(jnp.minimum(br - bh, 0.0))).astype(MXU_DTYPE)
            a = jnp.where(lvl == l, _dot_nt(ql, kl), a)
        o = _dot(a.astype(MXU_DTYPE), vh)
        st = st_ref[h]
        o = o + _dot_nt((qh * jnp.exp(bh)).astype(MXU_DTYPE), st.astype(MXU_DTYPE))
        btot = bh[tot_row:tot_row + 1, :]
        kd = (kh * jnp.exp(btot - bh)).astype(MXU_DTYPE)
        st_ref[h] = st * jnp.exp(btot) + _dot_tn(vh, kd)
        outs.append(o)
    if final:
        ob = ob_ref[...]
        og = og_ref[...].astype(F32)
        ng = ng_ref[...]
        for h in range(nheads):
            sl = slice(h * HGRN_DK, (h + 1) * HGRN_DK)
            o = outs[h] + ob[:, sl]
            o = o * lax.rsqrt(jnp.mean(o * o, axis=-1, keepdims=True) + EPS)
            g = og[:, sl]
            o_ref[:, sl] = (o * ng[:, sl] * (g * _sigmoid(g))).astype(o_ref.dtype)
    else:
        for h in range(nheads):
            sl = slice(h * HGRN_DK, (h + 1) * HGRN_DK)
            o_ref[:, sl] = outs[h]


def _hgrn_call(u, uf, lb_dir, norm_g, ob, out_prev, tok_off, nseq, seq_len, reverse, final, cols):
    t = u.shape[0]
    bw = lb_dir.shape[-1]
    nheads = bw // HGRN_DK
    c = min(HGRN_CHUNK, seq_len)
    nck = seq_len // c
    mall, lvl, nlev = _hgrn_consts(c, reverse)
    blk0 = tok_off // c
    cq, ci, cog, cf = cols

    def tmap(col):
        if reverse:
            return lambda b, k: (blk0 + b * nck + (nck - 1 - k), col)
        return lambda b, k: (blk0 + b * nck + k, col)

    const = lambda b, k: (0, 0)
    in_specs = [
        pl.BlockSpec((c, bw), tmap(cq)),
        pl.BlockSpec((c, bw), tmap(ci)),
        pl.BlockSpec((c, bw), tmap(cf)),
        pl.BlockSpec((1, bw), const),
        pl.BlockSpec(mall.shape, const),
        pl.BlockSpec(lvl.shape, const),
    ]
    args = [u, u, uf, lb_dir.reshape(1, bw), jnp.asarray(mall, MXU_DTYPE), jnp.asarray(lvl)]
    if final:
        in_specs += [pl.BlockSpec((c, bw), tmap(cog)), pl.BlockSpec((c, bw), tmap(0)), pl.BlockSpec((1, bw), const)]
        args += [u, ob, norm_g.reshape(1, bw)]
    out_dtype = ACT_DTYPE if final else F32
    kern = functools.partial(_hgrn_kernel, c=c, nlev=nlev, nheads=nheads, reverse=reverse, final=final)
    aliases = {}
    if out_prev is not None:
        n_in = len(args)
        in_specs.append(pl.BlockSpec(memory_space=pl.ANY))
        args.append(out_prev)
        aliases = {n_in: 0}
        inner = kern
        kern = lambda *r: inner(*r[:n_in], *r[n_in + 1:])
    return pl.pallas_call(
        kern,
        grid=(nseq, nck),
        in_specs=in_specs,
        out_specs=pl.BlockSpec((c, bw), tmap(0)),
        out_shape=jax.ShapeDtypeStruct((t, bw), out_dtype),
        scratch_shapes=[pltpu.VMEM((nheads, HGRN_DK, HGRN_DK), F32)],
        input_output_aliases=aliases,
        compiler_params=_cparams(("parallel", "arbitrary")),
        name="hgrn_final" if final else "hgrn_rev",
    )(*args)


CONV_HALO = 16
CONV_ROWS = 32


def _conv_kernel(first_ref, last_ref, a_ref, g_ref, ap_ref, gp_ref, an_ref, gn_ref,
                 w_ref, b_ref, lg_ref, lb_ref, o_ref, hbuf, *, tb, width):
    i = pl.program_id(0)
    pad = width // 2

    def glu(a, g):
        return a.astype(F32) * _sigmoid(g.astype(F32))

    hbuf[CONV_HALO:CONV_HALO + tb, :] = glu(a_ref[...], g_ref[...])
    hp = glu(ap_ref[...], gp_ref[...])
    hbuf[0:CONV_HALO, :] = jnp.where(first_ref[i] == 1, 0.0, hp)
    hn = glu(an_ref[...], gn_ref[...])
    hbuf[CONV_HALO + tb:2 * CONV_HALO + tb, :] = jnp.where(last_ref[i] == 1, 0.0, hn)
    w = w_ref[...]
    for r0 in range(0, tb, CONV_ROWS):
        acc = jnp.zeros((CONV_ROWS, w.shape[1]), F32)
        for j in range(width):
            s = CONV_HALO + r0 + j - pad
            acc = acc + w[j:j + 1, :] * hbuf[s:s + CONV_ROWS, :]
        h = acc + b_ref[...]
        mu = jnp.mean(h, axis=-1, keepdims=True)
        hc = h - mu
        var = jnp.mean(hc * hc, axis=-1, keepdims=True)
        y = hc * lax.rsqrt(var + EPS) * lg_ref[...] + lb_ref[...]
        o_ref[r0:r0 + CONV_ROWS, :] = (y * _sigmoid(y)).astype(o_ref.dtype)


def _conv_call(u, conv_w, conv_b, ln_g, ln_b, seq_starts, seq_ends, col_a, col_g, tb):
    t = u.shape[0]
    width, ch = conv_w.shape
    assert width // 2 < CONV_HALO and tb % CONV_ROWS == 0
    nblk = t // tb
    hb = tb // CONV_HALO
    nh = t // CONV_HALO
    kern = functools.partial(_conv_kernel, tb=tb, width=width)
    cur = lambda col: pl.BlockSpec((tb, ch), lambda i, f, l: (i, col))
    prev = lambda col: pl.BlockSpec((CONV_HALO, ch), lambda i, f, l: (jnp.maximum(i * hb - 1, 0), col))
    nxt = lambda col: pl.BlockSpec((CONV_HALO, ch), lambda i, f, l: (jnp.minimum((i + 1) * hb, nh - 1), col))
    vec = lambda: pl.BlockSpec((1, ch), lambda i, f, l: (0, 0))
    grid_spec = pltpu.PrefetchScalarGridSpec(
        num_scalar_prefetch=2,
        grid=(nblk,),
        in_specs=[cur(col_a), cur(col_g), prev(col_a), prev(col_g), nxt(col_a), nxt(col_g),
                  pl.BlockSpec((width, ch), lambda i, f, l: (0, 0)), vec(), vec(), vec()],
        out_specs=pl.BlockSpec((tb, ch), lambda i, f, l: (i, 0)),
        scratch_shapes=[pltpu.VMEM((tb + 2 * CONV_HALO, ch), F32)],
    )
    return pl.pallas_call(
        kern,
        grid_spec=grid_spec,
        out_shape=jax.ShapeDtypeStruct((t, ch), ACT_DTYPE),
        compiler_params=_cparams(("parallel",)),
        name="conv",
    )(seq_starts, seq_ends, u, u, u, u, u, u, conv_w, conv_b.reshape(1, ch), ln_g.reshape(1, ch), ln_b.reshape(1, ch))


def _merge_kernel(ya_ref, yb_ref, yc_ref, g0_ref, g1_ref, g2_ref, wb_ref, wo_ref, x_ref, gn_ref, gt_ref,
                  o_ref, m_ref, *, cw):
    d = o_ref.shape[1]
    ys = (ya_ref, yb_ref, yc_ref)
    gs = (g0_ref, g1_ref, g2_ref)
    for cb in range(d // cw):
        cs = slice(cb * cw, (cb + 1) * cw)
        acc = None
        for i in range(3):
            term = _sigmoid(gs[i][:, cs].astype(F32)) * _dot(ys[i][...], wb_ref[i, :, cs])
            acc = term if acc is None else acc + term
        m_ref[:, cs] = acc.astype(m_ref.dtype)
    y = _dot(m_ref[...], wo_ref[...])
    o_ref[...] = x_ref[...] + gt_ref[0] * _rms(y, gn_ref[...])


def _merge_call(ya, yb, yc, u, gate_col0, wb, wo, x, gn, gt, lv, tm_pref=256):
    t, d = x.shape
    bw = ya.shape[1]
    tm = _tile(lv, tm_pref)
    cw = _tile(d, 512, LANE)
    row = lambda i: (i, 0)
    gate = lambda k: pl.BlockSpec((tm, d), lambda i: (i, gate_col0 + k))
    const2 = lambda i: (0, 0)
    return pl.pallas_call(
        functools.partial(_merge_kernel, cw=cw),
        grid=(t // tm,),
        in_specs=[
            pl.BlockSpec((tm, bw), row), pl.BlockSpec((tm, bw), row), pl.BlockSpec((tm, bw), row),
            gate(0), gate(1), gate(2),
            pl.BlockSpec(wb.shape, lambda i: (0, 0, 0)),
            pl.BlockSpec(wo.shape, const2),
            pl.BlockSpec((tm, d), row),
            pl.BlockSpec((1, d), const2),
            pl.BlockSpec((1, 1, d), lambda i: (i * tm // lv, 0, 0)),
        ],
        out_specs=pl.BlockSpec((tm, d), row),
        out_shape=jax.ShapeDtypeStruct((t, d), F32),
        scratch_shapes=[pltpu.VMEM((tm, d), MXU_DTYPE)],
        compiler_params=_cparams(("parallel",)),
        name="merge",
    )(ya, yb, yc, u, u, u, wb, wo, x, gn, gt)


def _ffn_kernel(x_ref, g_ref, sh_ref, sc_ref, w1_ref, w3_ref, w2_ref, gn_ref, gt_ref, o_ref, h_ref, acc_ref):
    j = pl.program_id(1)

    @pl.when(j == 0)
    def _():
        h_ref[...] = _norm_mod(x_ref[...], g_ref[...], sh_ref[0], sc_ref[0]).astype(h_ref.dtype)
        acc_ref[...] = jnp.zeros_like(acc_ref)

    h = h_ref[...]
    a = _dot(h, w1_ref[...])
    g = _dot(h, w3_ref[...])
    m = (a * _sigmoid(a) * g).astype(MXU_DTYPE)
    acc_ref[...] += _dot(m, w2_ref[...])

    @pl.when(j == pl.num_programs(1) - 1)
    def _():
        o_ref[...] = x_ref[...] + gt_ref[0] * _rms(acc_ref[...], gn_ref[...])


def _ffn_call(x, g, sh, sc, w13, w2, gn, gt, lv, tm_pref=512, tf_pref=512):
    t, d = x.shape
    f = w2.shape[0]
    tm = _tile(lv, tm_pref)
    tf = _tile(f, tf_pref, LANE)
    nf = f // tf
    row = lambda i, j: (i, 0)
    const2 = lambda i, j: (0, 0)
    mod = lambda i, j: (i * tm // lv, 0, 0)
    return pl.pallas_call(
        _ffn_kernel,
        grid=(t // tm, nf),
        in_specs=[
            pl.BlockSpec((tm, d), row),
            pl.BlockSpec((1, d), const2),
            pl.BlockSpec((1, 1, d), mod),
            pl.BlockSpec((1, 1, d), mod),
            pl.BlockSpec((d, tf), lambda i, j: (0, j)),
            pl.BlockSpec((d, tf), lambda i, j: (0, nf + j)),
            pl.BlockSpec((tf, d), lambda i, j: (j, 0)),
            pl.BlockSpec((1, d), const2),
            pl.BlockSpec((1, 1, d), mod),
        ],
        out_specs=pl.BlockSpec((tm, d), row),
        out_shape=jax.ShapeDtypeStruct((t, d), F32),
        scratch_shapes=[pltpu.VMEM((tm, d), MXU_DTYPE), pltpu.VMEM((tm, d), F32)],
        compiler_params=_cparams(("parallel", "arbitrary")),
        name="ffn",
    )(x, g, sh, sc, w13, w13, w2, gn, gt)


ROUTE_ROWS = 8


def _route_kernel(x_ref, g_ref, sh_ref, sc_ref, rw_ref, tri_ref, h_ref, route_ref, rt_ref, cnt_ref, run_ref,
                  *, n_experts, cap):
    @pl.when(pl.program_id(0) == 0)
    def _():
        run_ref[...] = jnp.zeros_like(run_ref)

    h = _norm_mod(x_ref[...], g_ref[...], sh_ref[0], sc_ref[0])
    h_ref[...] = h
    logits = jnp.dot(h, rw_ref[...], preferred_element_type=F32, precision=lax.Precision.HIGHEST)
    lane = lax.broadcasted_iota(jnp.int32, logits.shape, 1)
    neg_inf = -jnp.inf
    l1 = jnp.where(lane < n_experts, logits, neg_inf)
    m1 = jnp.max(l1, axis=-1, keepdims=True)
    i1 = jnp.min(jnp.where(l1 == m1, lane, LANE), axis=-1, keepdims=True)
    l2 = jnp.where(lane == i1, neg_inf, l1)
    m2 = jnp.max(l2, axis=-1, keepdims=True)
    i2 = jnp.min(jnp.where(l2 == m2, lane, LANE), axis=-1, keepdims=True)
    e = jnp.exp(m2 - m1)
    w0 = 1.0 / (1.0 + e)
    w1 = e * w0
    sel1 = lane == i1
    sel2 = lane == i2
    member = jnp.where(sel1, 1.0, jnp.where(sel2, 1.0, 0.0))
    rank = _dot(tri_ref[...], member.astype(MXU_DTYPE))
    base = run_ref[...] + rank + lane.astype(F32) * float(cap)
    pos0 = jnp.sum(jnp.where(sel1, base, 0.0), axis=-1, keepdims=True)
    pos1 = jnp.sum(jnp.where(sel2, base, 0.0), axis=-1, keepdims=True)
    route = jnp.where(lane == 0, pos0, jnp.where(lane == 1, pos1, jnp.where(lane == 2, w0, jnp.where(lane == 3, w1, 0.0))))
    route_ref[...] = route
    rt_ref[...] = route.T[0:ROUTE_ROWS, :].astype(jnp.int32)
    run_ref[...] += jnp.sum(member, axis=0, keepdims=True)
    cnt_ref[...] = run_ref[...]


def _route_call(x, g, sh, sc, router_w, lv, cap, tb_pref=256):
    t, d = x.shape
    n_experts = router_w.shape[1]
    tb = _tile(lv, tb_pref, LANE)
    rw = jnp.zeros((d, LANE), F32).at[:, :n_experts].set(router_w)
    idx = np.arange(tb)
    tri = jnp.asarray((idx[None, :] < idx[:, None]).astype(np.float32), MXU_DTYPE)
    row = lambda i: (i, 0)
    const2 = lambda i: (0, 0)
    mod = lambda i: (i * tb // lv, 0, 0)
    return pl.pallas_call(
        functools.partial(_route_kernel, n_experts=n_experts, cap=cap),
        grid=(t // tb,),
        in_specs=[
            pl.BlockSpec((tb, d), row),
            pl.BlockSpec((1, d), const2),
            pl.BlockSpec((1, 1, d), mod),
            pl.BlockSpec((1, 1, d), mod),
            pl.BlockSpec((d, LANE), const2),
            pl.BlockSpec((tb, tb), const2),
        ],
        out_specs=[
            pl.BlockSpec((tb, d), row),
            pl.BlockSpec((tb, LANE), row),
            pl.BlockSpec((ROUTE_ROWS, tb), lambda i: (0, i)),
            pl.BlockSpec((1, LANE), const2),
        ],
        out_shape=[
            jax.ShapeDtypeStruct((t, d), F32),
            jax.ShapeDtypeStruct((t, LANE), F32),
            jax.ShapeDtypeStruct((ROUTE_ROWS, t), jnp.int32),
            jax.ShapeDtypeStruct((1, LANE), F32),
        ],
        scratch_shapes=[pltpu.VMEM((1, LANE), F32)],
        compiler_params=_cparams(("arbitrary",)),
        name="route",
    )(x, g, sh, sc, rw, tri)


def _scatter_kernel(cnt_ref, rt_ref, h_hbm, xs_hbm, idx_ref, zrow_ref, sem_idx, sem, *, tb, tm, cap, n_experts):
    i = pl.program_id(0)
    cp = pltpu.make_async_copy(rt_ref, idx_ref, sem_idx)
    cp.start()
    cp.wait()

    def row_copy(src_row, dst_row):
        return pltpu.make_async_copy(h_hbm.at[pl.ds(src_row, 1)], xs_hbm.at[pl.ds(dst_row, 1)], sem)

    def issue(k, carry):
        row_copy(i * tb + k, idx_ref[0, k]).start()
        row_copy(i * tb + k, idx_ref[1, k]).start()
        return carry

    lax.fori_loop(0, tb, issue, 0)

    def drain(k, carry):
        row_copy(0, 0).wait()
        row_copy(0, 0).wait()
        return carry

    lax.fori_loop(0, tb, drain, 0)

    @pl.when(i == pl.num_programs(0) - 1)
    def _():
        zrow_ref[...] = jnp.zeros_like(zrow_ref)

        def zero_copy(dst_row):
            return pltpu.make_async_copy(zrow_ref.at[pl.ds(0, 1)], xs_hbm.at[pl.ds(dst_row, 1)], sem)

        for e in range(n_experts):
            cnt = cnt_ref[e]
            end = ((cnt + tm - 1) // tm) * tm

            def zissue(r, carry, e=e):
                zero_copy(e * cap + r).start()
                return carry

            def zdrain(r, carry):
                zero_copy(0).wait()
                return carry

            lax.fori_loop(cnt, end, zissue, 0)
            lax.fori_loop(cnt, end, zdrain, 0)


def _scatter_call(counts, rt, h, cap, tm, n_experts, tb_pref=512):
    t, d = h.shape
    tb = _tile(t, tb_pref, LANE)
    grid_spec = pltpu.PrefetchScalarGridSpec(
        num_scalar_prefetch=1,
        grid=(t // tb,),
        in_specs=[
            pl.BlockSpec((ROUTE_ROWS, tb), lambda i, c: (0, i)),
            pl.BlockSpec(memory_space=pl.ANY),
        ],
        out_specs=pl.BlockSpec(memory_space=pl.ANY),
        scratch_shapes=[
            pltpu.SMEM((ROUTE_ROWS, tb), jnp.int32),
            pltpu.VMEM((SUBLANE, d), F32),
            pltpu.SemaphoreType.DMA,
            pltpu.SemaphoreType.DMA,
        ],
    )
    return pl.pallas_call(
        functools.partial(_scatter_kernel, tb=tb, tm=tm, cap=cap, n_experts=n_experts),
        grid_spec=grid_spec,
        out_shape=jax.ShapeDtypeStruct((n_experts * cap, d), F32),
        compiler_params=_cparams(("arbitrary",)),
        name="moe_scatter",
    )(counts, rt, h)


def _gmm_kernel(be_ref, br_ref, bv_ref, xs_ref, w1_ref, w3_ref, w2_ref, y_ref, xb_ref, acc_ref):
    i = pl.program_id(0)
    j = pl.program_id(1)

    @pl.when(bv_ref[i] == 1)
    def _():
        @pl.when(j == 0)
        def _():
            xb_ref[...] = xs_ref[...].astype(xb_ref.dtype)
            acc_ref[...] = jnp.zeros_like(acc_ref)

        xb = xb_ref[...]
        a = _dot(xb, w1_ref[0])
        g = _dot(xb, w3_ref[0])
        m = (a * _sigmoid(a) * g).astype(MXU_DTYPE)
        acc_ref[...] += _dot(m, w2_ref[0])

        @pl.when(j == pl.num_programs(1) - 1)
        def _():
            y_ref[...] = acc_ref[...]


def _gmm_call(blk_e, blk_row, blk_valid, xs, w13, w2, tm, tf_pref=512):
    n_rows, d = xs.shape
    f = w2.shape[1]
    tf = _tile(f, tf_pref, LANE)
    nf = f // tf
    n_blk = blk_e.shape[0]

    def jsel(i, j, bv):
        return jnp.where(bv[i] == 1, j, nf - 1)

    grid_spec = pltpu.PrefetchScalarGridSpec(
        num_scalar_prefetch=3,
        grid=(n_blk, nf),
        in_specs=[
            pl.BlockSpec((tm, d), lambda i, j, be, br, bv: (br[i], 0)),
            pl.BlockSpec((1, d, tf), lambda i, j, be, br, bv: (be[i], 0, jsel(i, j, bv))),
            pl.BlockSpec((1, d, tf), lambda i, j, be, br, bv: (be[i], 0, nf + jsel(i, j, bv))),
            pl.BlockSpec((1, tf, d), lambda i, j, be, br, bv: (be[i], jsel(i, j, bv), 0)),
        ],
        out_specs=pl.BlockSpec((tm, d), lambda i, j, be, br, bv: (br[i], 0)),
        scratch_shapes=[pltpu.VMEM((tm, d), MXU_DTYPE), pltpu.VMEM((tm, d), F32)],
    )
    return pl.pallas_call(
        _gmm_kernel,
        grid_spec=grid_spec,
        out_shape=jax.ShapeDtypeStruct((n_rows, d), F32),
        compiler_params=_cparams(("arbitrary", "arbitrary")),
        name="moe_gmm",
    )(blk_e, blk_row, blk_valid, xs, w13, w13, w2)


def _combine_kernel(rt_ref, route_ref, x_ref, gn_ref, gt_ref, y_hbm, o_ref, idx_ref, y0_ref, y1_ref, sem_idx, sem,
                    *, tb):
    cp = pltpu.make_async_copy(rt_ref, idx_ref, sem_idx)
    cp.start()
    cp.wait()

    def row_copy(src_row, dst_ref, k, s):
        return pltpu.make_async_copy(y_hbm.at[pl.ds(src_row, 1)], dst_ref.at[pl.ds(k, 1)], sem.at[s])

    def issue(k, carry):
        row_copy(idx_ref[0, k], y0_ref, k, 0).start()
        row_copy(idx_ref[1, k], y1_ref, k, 1).start()
        return carry

    lax.fori_loop(0, tb, issue, 0)

    def drain(k, carry):
        row_copy(0, y0_ref, k, 0).wait()
        row_copy(0, y1_ref, k, 1).wait()
        return carry

    lax.fori_loop(0, tb, drain, 0)
    route = route_ref[...]
    w0 = route[:, 2:3]
    w1 = route[:, 3:4]
    y = w0 * y0_ref[...] + w1 * y1_ref[...]
    o_ref[...] = x_ref[...] + gt_ref[0] * _rms(y, gn_ref[...])


def _combine_call(rt, route, x, gn, gt, y, lv, tb_pref=256):
    t, d = x.shape
    tb = _tile(lv, tb_pref, LANE)
    row = lambda i: (i, 0)
    return pl.pallas_call(
        functools.partial(_combine_kernel, tb=tb),
        grid=(t // tb,),
        in_specs=[
            pl.BlockSpec((ROUTE_ROWS, tb), lambda i: (0, i)),
            pl.BlockSpec((tb, LANE), row),
            pl.BlockSpec((tb, d), row),
            pl.BlockSpec((1, d), lambda i: (0, 0)),
            pl.BlockSpec((1, 1, d), lambda i: (i * tb // lv, 0, 0)),
            pl.BlockSpec(memory_space=pl.ANY),
        ],
        out_specs=pl.BlockSpec((tb, d), row),
        out_shape=jax.ShapeDtypeStruct((t, d), F32),
        scratch_shapes=[
            pltpu.SMEM((ROUTE_ROWS, tb), jnp.int32),
            pltpu.VMEM((tb, d), F32),
            pltpu.VMEM((tb, d), F32),
            pltpu.SemaphoreType.DMA,
            pltpu.SemaphoreType.DMA((2,)),
        ],
        compiler_params=_cparams(("arbitrary",)),
        name="moe_combine",
    )(rt, route, x, gn, gt, y)


def _moe(x, g, sh, sc, router_w, w13, w2, gn, gt, lv, tm_pref=512):
    t, d = x.shape
    n_experts = router_w.shape[1]
    tm = _tile(t, tm_pref, LANE)
    cap = -(-t // tm) * tm
    h, route, rt, cnt = _route_call(x, g, sh, sc, router_w, lv, cap)
    counts = cnt[0, :n_experts].astype(jnp.int32)
    xs = _scatter_call(counts, rt, h, cap, tm, n_experts)
    nb = (counts + tm - 1) // tm
    ends = jnp.cumsum(nb)
    n_blk = TOP_K * t // tm + n_experts
    bi = jnp.arange(n_blk, dtype=jnp.int32)
    valid = bi < ends[-1]
    bi_c = jnp.minimum(bi, ends[-1] - 1)
    be = jnp.minimum(jnp.searchsorted(ends, bi_c, side="right"), n_experts - 1).astype(jnp.int32)
    br = be * (cap // tm) + (bi_c - (ends - nb)[be])
    y = _gmm_call(be, br.astype(jnp.int32), valid.astype(jnp.int32), xs, w13, w2, tm)
    return _combine_call(rt, route, x, gn, gt, y, lv)


def kernel(x_prompt, x_sample, c_prompt, c_sample, ada_w, ada_b, norm_g, w_in, na_rpb, hgrn_lb, hgrn_norm_g,
           conv_w, conv_b, conv_ln_g, conv_ln_b, w_branch, w_out, ffn_w13, ffn_w2, router_w, moe_w13, moe_w2):
    bp, lp, d = x_prompt.shape
    bs, ls, _ = x_sample.shape
    depth = ada_w.shape[0]
    bw = w_branch.shape[2]
    tp = bp * lp
    t = tp + bs * ls
    lv = math.gcd(lp, ls)
    n_vseq = t // lv
    assert 8 * bw % d == 0 and d % LANE == 0 and bw % LANE == 0

    x = jnp.concatenate([x_prompt.reshape(tp, d), x_sample.reshape(bs * ls, d)], axis=0)

    nb = bp + bs
    rows = -(-nb // SUBLANE) * SUBLANE
    c_all = jnp.zeros((rows, d), F32).at[:nb].set(jnp.concatenate([c_prompt, c_sample], axis=0))
    mod = _ada(c_all, ada_w, ada_b)
    starts = np.arange(n_vseq) * lv
    vb = np.where(starts < tp, starts // lp, bp + (starts - tp) // ls)
    mod_v = mod[:, vb, :].reshape(depth, n_vseq, 1, 6, d)
    part = lambda l, k: mod_v[l, :, :, k, :]

    sm = jax.nn.softmax(hgrn_lb.astype(F32), axis=0)
    lb_all = jnp.clip(jnp.cumsum(sm, axis=0) - sm[0], 0.0, 1.0)

    cb = lambda k: slice(k * bw, (k + 1) * bw)
    order = [0, 1, 2, 3, 4, 7, 8, 9]
    conv_tb = _tile(lv, 256)
    blk_start = np.arange(t // conv_tb) * conv_tb
    seq_pos = np.where(blk_start < tp, blk_start % lp, (blk_start - tp) % ls)
    seq_len_of = np.where(blk_start < tp, lp, ls)
    seq_starts = jnp.asarray((seq_pos == 0).astype(np.int32))
    seq_ends = jnp.asarray((seq_pos + conv_tb == seq_len_of).astype(np.int32))

    for l in range(depth):
        wl = w_in[l]
        w_a = jnp.concatenate([wl[:, cb(k)] for k in order] + [wl[:, 10 * bw:]], axis=1).astype(MXU_DTYPE)
        w_f = wl[:, 5 * bw:7 * bw].astype(MXU_DTYPE)
        g0 = norm_g[l, 0].reshape(1, d)
        u = _nm_matmul(x, g0, part(l, 0), part(l, 1), w_a, ACT_DTYPE, lv)
        uf = _nm_matmul(x, g0, part(l, 0), part(l, 1), w_f, F32, lv)

        bias_tab = _na_bias_table(na_rpb[l])
        ya = _na_call(u, bias_tab, None, 0, bp, lp)
        ya = _na_call(u, bias_tab, ya, tp, bs, ls)

        cols_b = (3, 4, 5, 1)
        cols_f = (3, 4, 5, 0)
        ob = _hgrn_call(u, uf, lb_all[l, 1], None, None, None, 0, bp, lp, True, False, cols_b)
        ob = _hgrn_call(u, uf, lb_all[l, 1], None, None, ob, tp, bs, ls, True, False, cols_b)
        yb = _hgrn_call(u, uf, lb_all[l, 0], hgrn_norm_g[l], ob, None, 0, bp, lp, False, True, cols_f)
        yb = _hgrn_call(u, uf, lb_all[l, 0], hgrn_norm_g[l], ob, yb, tp, bs, ls, False, True, cols_f)

        yc = _conv_call(u, conv_w[l], conv_b[l], conv_ln_g[l], conv_ln_b[l], seq_starts, seq_ends, 6, 7, conv_tb)

        x = _merge_call(ya, yb, yc, u, 8 * bw // d, w_branch[l].astype(MXU_DTYPE), w_out[l].astype(MXU_DTYPE),
                        x, norm_g[l, 1].reshape(1, d), part(l, 2), lv)

        g2 = norm_g[l, 2].reshape(1, d)
        g3 = norm_g[l, 3].reshape(1, d)
        if l % 2 == 0:
            x = _ffn_call(x, g2, part(l, 3), part(l, 4), ffn_w13[l // 2].astype(MXU_DTYPE),
                          ffn_w2[l // 2].astype(MXU_DTYPE), g3, part(l, 5), lv)
        else:
            x = _moe(x, g2, part(l, 3), part(l, 4), router_w[l // 2], moe_w13[l // 2].astype(MXU_DTYPE),
                     moe_w2[l // 2].astype(MXU_DTYPE), g3, part(l, 5), lv)

    return (x[:tp].reshape(bp, lp, d), x[tp:].reshape(bs, ls, d))
```

```python
import functools
import math

import numpy as np
import jax
import jax.numpy as jnp
from jax import lax
from jax.experimental import pallas as pl
from jax.experimental.pallas import tpu as pltpu

GRID_W = 64
NA_HEAD_DIM = 64
NA_ROWS = 8
NA_COLS = 16
HGRN_DK = 128
TOP_K = 2
EPS = 1e-6
F_MIN = 1e-30
NEG_BIG = -1e30

LANE = 128
SUBLANE = 8
VMEM_LIMIT_BYTES = 56 * 1024 * 1024

MXU_DTYPE = jnp.bfloat16
ACT_DTYPE = jnp.bfloat16

HGRN_CHUNK = 128
HGRN_MATMUL_LEVELS = 3
NA_ROW_UNROLL = 4
NA_QGROUP = 16
F32 = jnp.float32


def _cparams(sem, vmem=VMEM_LIMIT_BYTES):
    return pltpu.CompilerParams(dimension_semantics=sem, vmem_limit_bytes=vmem)


def _sigmoid(x):
    return 1.0 / (1.0 + jnp.exp(-x))


def _dot(a, b):
    return jnp.dot(a, b, preferred_element_type=F32)


def _dot_nt(a, b):
    return lax.dot_general(a, b, (((1,), (1,)), ((), ())), preferred_element_type=F32)


def _dot_tn(a, b):
    return lax.dot_general(a, b, (((0,), (0,)), ((), ())), preferred_element_type=F32)


def _rms(x, g):
    return x * lax.rsqrt(jnp.mean(x * x, axis=-1, keepdims=True) + EPS) * g


def _norm_mod(x, g, sh, sc):
    return _rms(x, g) * (1.0 + sc) + sh


def _tile(n, pref, mult=SUBLANE):
    if n <= pref:
        return n
    t = (pref // mult) * mult
    while t >= mult:
        if n % t == 0:
            return t
        t -= mult
    return n


def _ada_kernel(c_ref, w_ref, b_ref, o_ref):
    c = c_ref[...]
    cs = (c * _sigmoid(c)).astype(MXU_DTYPE)
    o_ref[0] = _dot(cs, w_ref[0].astype(MXU_DTYPE)) + b_ref[0]


def _ada(c_all, ada_w, ada_b):
    depth, d, n = ada_w.shape
    rows = c_all.shape[0]
    tn = _tile(n, 1024, LANE)
    return pl.pallas_call(
        _ada_kernel,
        grid=(depth, n // tn),
        in_specs=[
            pl.BlockSpec((rows, d), lambda l, j: (0, 0)),
            pl.BlockSpec((1, d, tn), lambda l, j: (l, 0, j)),
            pl.BlockSpec((1, 1, tn), lambda l, j: (l, 0, j)),
        ],
        out_specs=pl.BlockSpec((1, rows, tn), lambda l, j: (l, 0, j)),
        out_shape=jax.ShapeDtypeStruct((depth, rows, n), F32),
        compiler_params=_cparams(("parallel", "parallel")),
        name="ada",
    )(c_all, ada_w, ada_b.reshape(depth, 1, n))


def _nm_matmul_kernel(x_ref, g_ref, sh_ref, sc_ref, w_ref, o_ref, h_ref):
    @pl.when(pl.program_id(1) == 0)
    def _():
        h_ref[...] = _norm_mod(x_ref[...], g_ref[...], sh_ref[0], sc_ref[0]).astype(h_ref.dtype)

    o_ref[...] = _dot(h_ref[...], w_ref[...]).astype(o_ref.dtype)


def _nm_matmul(x, g, sh, sc, w, out_dtype, lv, tm_pref=1024, tn_pref=1024):
    t, d = x.shape
    n = w.shape[1]
    tm = _tile(lv, tm_pref)
    tn = _tile(n, tn_pref, LANE)
    return pl.pallas_call(
        _nm_matmul_kernel,
        grid=(t // tm, n // tn),
        in_specs=[
            pl.BlockSpec((tm, d), lambda i, j: (i, 0)),
            pl.BlockSpec((1, d), lambda i, j: (0, 0)),
            pl.BlockSpec((1, 1, d), lambda i, j: (i * tm // lv, 0, 0)),
            pl.BlockSpec((1, 1, d), lambda i, j: (i * tm // lv, 0, 0)),
            pl.BlockSpec((d, tn), lambda i, j: (0, j)),
        ],
        out_specs=pl.BlockSpec((tm, tn), lambda i, j: (i, j)),
        out_shape=jax.ShapeDtypeStruct((t, n), out_dtype),
        scratch_shapes=[pltpu.VMEM((tm, d), MXU_DTYPE)],
        compiler_params=_cparams(("parallel", "arbitrary")),
        name="nm_matmul",
    )(x, g, sh, sc, w)


def _na_bias_table(rpb):
    cols = np.arange(GRID_W)
    col_start = np.clip(cols - NA_COLS // 2, 0, GRID_W - NA_COLS)
    in_win = (cols[None, :] >= col_start[:, None]) & (cols[None, :] < col_start[:, None] + NA_COLS)
    dc = cols[None, :] - cols[:, None] + (NA_COLS - 1)
    onehot = ((dc[None] == np.arange(2 * NA_COLS - 1)[:, None, None]) & in_win[None]).astype(np.float32)
    full = jnp.einsum("hrd,dqk->hrqk", rpb.astype(F32), jnp.asarray(onehot), precision=lax.Precision.HIGHEST)
    tabs = []
    for delta in range(NA_ROWS):
        rows = full[:, NA_ROWS - 1 - delta:2 * NA_ROWS - 1 - delta]
        tabs.append(rows.transpose(0, 2, 1, 3).reshape(rpb.shape[0], GRID_W, NA_ROWS * GRID_W))
    mask = jnp.asarray(np.tile(in_win, (1, NA_ROWS)))
    return jnp.where(mask[None, None], jnp.stack(tabs, axis=0), NEG_BIG)


def _na_kernel(q_ref, k_ref, v_ref, bias_ref, o_ref, s_ref, p_ref, *, rows):
    win = NA_ROWS * GRID_W
    lane = lax.broadcasted_iota(jnp.int32, (1, LANE), 1)
    head_masks = [lane < NA_HEAD_DIM, lane >= NA_HEAD_DIM]
    scale = NA_HEAD_DIM ** -0.5

    def body(i, carry):
        r0 = i * NA_ROW_UNROLL
        starts = [jnp.clip(r0 + u - NA_ROWS // 2, 0, rows - NA_ROWS) for u in range(NA_ROW_UNROLL)]
        for u in range(NA_ROW_UNROLL):
            q2 = q_ref[pl.ds(pl.multiple_of((r0 + u) * GRID_W, GRID_W), GRID_W), :]
            q2 = q2 * jnp.asarray(scale, q2.dtype)
            kw = k_ref[pl.ds(pl.multiple_of(starts[u] * GRID_W, GRID_W), win), :]
            qq = jnp.concatenate([jnp.where(head_masks[h], q2, jnp.zeros_like(q2)) for h in range(2)], axis=0)
            s_ref[u] = _dot_nt(qq, kw)
        rden = []
        for u in range(NA_ROW_UNROLL):
            delta = r0 + u - starts[u]
            parts = []
            for g in range(2 * GRID_W // NA_QGROUP):
                h, gq = divmod(g * NA_QGROUP, GRID_W)
                s = s_ref[u, pl.ds(g * NA_QGROUP, NA_QGROUP), :] + bias_ref[delta, h, pl.ds(gq, NA_QGROUP), :]
                m = jnp.max(s, axis=-1, keepdims=True)
                e = jnp.exp(s - m)
                parts.append(1.0 / jnp.sum(e, axis=-1, keepdims=True))
                p_ref[u, pl.ds(g * NA_QGROUP, NA_QGROUP), :] = e.astype(p_ref.dtype)
            rden.append(jnp.concatenate(parts, axis=0))
        for u in range(NA_ROW_UNROLL):
            vw = v_ref[pl.ds(pl.multiple_of(starts[u] * GRID_W, GRID_W), win), :]
            o2 = _dot(p_ref[u], vw) * rden[u]
            out = jnp.where(head_masks[0], o2[:GRID_W], o2[GRID_W:])
            o_ref[pl.ds(pl.multiple_of((r0 + u) * GRID_W, GRID_W), GRID_W), :] = out.astype(o_ref.dtype)
        return carry

    lax.fori_loop(0, rows // NA_ROW_UNROLL, body, 0)


def _na_call(u, bias_tab, y_prev, tok_off, nseq, seq_len):
    t = u.shape[0]
    bw = bias_tab.shape[1] * NA_HEAD_DIM
    npair = bw // LANE
    rows = seq_len // GRID_W
    assert rows >= NA_ROWS and rows % NA_ROW_UNROLL == 0 and seq_len % GRID_W == 0 and tok_off % seq_len == 0
    s0 = tok_off // seq_len
    kern = functools.partial(_na_kernel, rows=rows)
    in_specs = [
        pl.BlockSpec((seq_len, LANE), lambda b, p: (s0 + b, p)),
        pl.BlockSpec((seq_len, LANE), lambda b, p: (s0 + b, npair + p)),
        pl.BlockSpec((seq_len, LANE), lambda b, p: (s0 + b, 2 * npair + p)),
        pl.BlockSpec((NA_ROWS, 2, GRID_W, NA_ROWS * GRID_W), lambda b, p: (0, p, 0, 0)),
    ]
    args = [u, u, u, bias_tab]
    aliases = {}
    if y_prev is not None:
        in_specs.append(pl.BlockSpec(memory_space=pl.ANY))
        args.append(y_prev)
        aliases = {4: 0}
        kern_fn = lambda q, k, v, b, _prev, o, s, p: kern(q, k, v, b, o, s, p)
    else:
        kern_fn = kern
    return pl.pallas_call(
        kern_fn,
        grid=(nseq, npair),
        in_specs=in_specs,
        out_specs=pl.BlockSpec((seq_len, LANE), lambda b, p: (s0 + b, p)),
        out_shape=jax.ShapeDtypeStruct((t, bw), ACT_DTYPE),
        scratch_shapes=[
            pltpu.VMEM((NA_ROW_UNROLL, 2 * GRID_W, NA_ROWS * GRID_W), F32),
            pltpu.VMEM((NA_ROW_UNROLL, 2 * GRID_W, NA_ROWS * GRID_W), MXU_DTYPE),
        ],
        input_output_aliases=aliases,
        compiler_params=_cparams(("parallel", "parallel")),
        name="na",
    )(*args)


def _hgrn_consts(c, reverse):
    nlev = int(math.log2(c))
    nmat = min(HGRN_MATMUL_LEVELS, nlev)
    idx = np.arange(c)
    tri = (idx[None, :] <= idx[:, None]).astype(np.float32)
    mats = [tri]
    for l in range(nmat):
        m = 1 << l
        ref = (idx // (2 * m)) * (2 * m) + m - 1
        mats.append(tri[ref])
    mall = np.concatenate(mats, axis=0)
    x = idx[:, None] ^ idx[None, :]
    lvl = np.where(x > 0, np.floor(np.log2(np.maximum(x, 1))), -1).astype(np.int32)
    lvl = np.where(idx[None, :] > idx[:, None], -2, lvl)
    if reverse:
        mall = mall.reshape(nmat + 1, c, c)[:, ::-1, ::-1].reshape((nmat + 1) * c, c)
        lvl = lvl[::-1, ::-1]
    return np.ascontiguousarray(np.tile(mall, (1, 3))), np.ascontiguousarray(lvl), nlev


def _hgrn_kernel(*refs, c, nlev, nheads, reverse, final):
    if final:
        (q_ref, i_ref, f_ref, lb_ref, mall_ref, lvl_ref, og_ref, ob_ref, ng_ref, o_ref, st_ref) = refs
    else:
        (q_ref, i_ref, f_ref, lb_ref, mall_ref, lvl_ref, o_ref, st_ref) = refs

    @pl.when(pl.program_id(1) == 0)
    def _():
        st_ref[...] = jnp.zeros_like(st_ref)

    lb = lb_ref[...]
    fg = jnp.maximum(lb + (1.0 - lb) * _sigmoid(f_ref[...]), F_MIN)
    lg = jnp.log2(fg)
    kk = 1.0 - fg
    hi = lg.astype(MXU_DTYPE)
    r1 = lg - hi.astype(F32)
    mid = r1.astype(MXU_DTYPE)
    lo = (r1 - mid.astype(F32)).astype(MXU_DTYPE)
    ball = _dot(mall_ref[...], jnp.concatenate([hi, mid, lo], axis=0))
    nmat = min(HGRN_MATMUL_LEVELS, nlev)
    b_all = ball[0:c]
    bref = [ball[(l + 1) * c:(l + 2) * c] for l in range(nmat)]
    for l in range(nmat, nlev):
        m = 1 << l
        blocks = []
        for g0 in range(0, c, 2 * m):
            r = g0 + m if reverse else g0 + m - 1
            blocks.append(jnp.broadcast_to(b_all[r:r + 1, :], (2 * m, b_all.shape[1])))
        bref.append(blocks[0] if len(blocks) == 1 else jnp.concatenate(blocks, axis=0))
    q = q_ref[...].astype(F32) * (HGRN_DK ** -0.5)
    v = i_ref[...].astype(MXU_DTYPE)
    lvl = lvl_ref[...]
    tot_row = 0 if reverse else c - 1
    outs = []
    for h in range(nheads):
        sl = slice(h * HGRN_DK, (h + 1) * HGRN_DK)
        bh = b_all[:, sl]
        qh = q[:, sl]
        kh = kk[:, sl]
        vh = v[:, sl]
        a = jnp.where(lvl == -1, _dot_nt(qh.astype(MXU_DTYPE), kh.astype(MXU_DTYPE)), 0.0)
        for l in range(nlev):
            e = jnp.exp2(-jnp.abs(bh - bref[l][:, sl]))
            a = jnp.where(lvl == l, _dot_nt((qh * e).astype(MXU_DTYPE), (kh * e).astype(MXU_DTYPE)), a)
        o = _dot(a.astype(MXU_DTYPE), vh)
        st = st_ref[h]
        o = o + _dot_nt((qh * jnp.exp2(bh)).astype(MXU_DTYPE), st.astype(MXU_DTYPE))
        btot = bh[tot_row:tot_row + 1, :]
        kd = (kh * jnp.exp2(btot - bh)).astype(MXU_DTYPE)
        st_ref[h] = st * jnp.exp2(btot) + _dot_tn(vh, kd)
        outs.append(o)
    if final:
        ob = ob_ref[...]
        og = og_ref[...].astype(F32)
        ng = ng_ref[...]
        for h in range(nheads):
            sl = slice(h * HGRN_DK, (h + 1) * HGRN_DK)
            o = outs[h] + ob[:, sl]
            o = o * lax.rsqrt(jnp.mean(o * o, axis=-1, keepdims=True) + EPS)
            g = og[:, sl]
            o_ref[:, sl] = (o * ng[:, sl] * (g * _sigmoid(g))).astype(o_ref.dtype)
    else:
        for h in range(nheads):
            sl = slice(h * HGRN_DK, (h + 1) * HGRN_DK)
            o_ref[:, sl] = outs[h]


def _hgrn_call(u, uf, lb_dir, norm_g, ob, out_prev, tok_off, nseq, seq_len, reverse, final, cols):
    t = u.shape[0]
    bw = lb_dir.shape[-1]
    nheads = bw // HGRN_DK
    c = min(HGRN_CHUNK, seq_len)
    nck = seq_len // c
    mall, lvl, nlev = _hgrn_consts(c, reverse)
    blk0 = tok_off // c
    cq, ci, cog, cf = cols

    def tmap(col):
        if reverse:
            return lambda b, k: (blk0 + b * nck + (nck - 1 - k), col)
        return lambda b, k: (blk0 + b * nck + k, col)

    const = lambda b, k: (0, 0)
    in_specs = [
        pl.BlockSpec((c, bw), tmap(cq)),
        pl.BlockSpec((c, bw), tmap(ci)),
        pl.BlockSpec((c, bw), tmap(cf)),
        pl.BlockSpec((1, bw), const),
        pl.BlockSpec(mall.shape, const),
        pl.BlockSpec(lvl.shape, const),
    ]
    args = [u, u, uf, lb_dir.reshape(1, bw), jnp.asarray(mall, MXU_DTYPE), jnp.asarray(lvl)]
    if final:
        in_specs += [pl.BlockSpec((c, bw), tmap(cog)), pl.BlockSpec((c, bw), tmap(0)), pl.BlockSpec((1, bw), const)]
        args += [u, ob, norm_g.reshape(1, bw)]
    out_dtype = ACT_DTYPE if final else F32
    kern = functools.partial(_hgrn_kernel, c=c, nlev=nlev, nheads=nheads, reverse=reverse, final=final)
    aliases = {}
    if out_prev is not None:
        n_in = len(args)
        in_specs.append(pl.BlockSpec(memory_space=pl.ANY))
        args.append(out_prev)
        aliases = {n_in: 0}
        inner = kern
        kern = lambda *r: inner(*r[:n_in], *r[n_in + 1:])
    return pl.pallas_call(
        kern,
        grid=(nseq, nck),
        in_specs=in_specs,
        out_specs=pl.BlockSpec((c, bw), tmap(0)),
        out_shape=jax.ShapeDtypeStruct((t, bw), out_dtype),
        scratch_shapes=[pltpu.VMEM((nheads, HGRN_DK, HGRN_DK), F32)],
        input_output_aliases=aliases,
        compiler_params=_cparams(("parallel", "arbitrary")),
        name="hgrn_final" if final else "hgrn_rev",
    )(*args)


CONV_HALO = 16
CONV_ROWS = 32


def _conv_kernel(first_ref, last_ref, a_ref, g_ref, ap_ref, gp_ref, an_ref, gn_ref,
                 w_ref, b_ref, lg_ref, lb_ref, o_ref, hbuf, *, tb, width):
    i = pl.program_id(0)
    pad = width // 2

    def glu(a, g):
        return a.astype(F32) * _sigmoid(g.astype(F32))

    hbuf[CONV_HALO:CONV_HALO + tb, :] = glu(a_ref[...], g_ref[...])
    hp = glu(ap_ref[...], gp_ref[...])
    hbuf[0:CONV_HALO, :] = jnp.where(first_ref[i] == 1, 0.0, hp)
    hn = glu(an_ref[...], gn_ref[...])
    hbuf[CONV_HALO + tb:2 * CONV_HALO + tb, :] = jnp.where(last_ref[i] == 1, 0.0, hn)
    w = w_ref[...]
    for r0 in range(0, tb, CONV_ROWS):
        acc = jnp.zeros((CONV_ROWS, w.shape[1]), F32)
        for j in range(width):
            s = CONV_HALO + r0 + j - pad
            acc = acc + w[j:j + 1, :] * hbuf[s:s + CONV_ROWS, :]
        h = acc + b_ref[...]
        mu = jnp.mean(h, axis=-1, keepdims=True)
        hc = h - mu
        var = jnp.mean(hc * hc, axis=-1, keepdims=True)
        y = hc * lax.rsqrt(var + EPS) * lg_ref[...] + lb_ref[...]
        o_ref[r0:r0 + CONV_ROWS, :] = (y * _sigmoid(y)).astype(o_ref.dtype)


def _conv_call(u, conv_w, conv_b, ln_g, ln_b, seq_starts, seq_ends, col_a, col_g, tb):
    t = u.shape[0]
    width, ch = conv_w.shape
    assert width // 2 < CONV_HALO and tb % CONV_ROWS == 0
    nblk = t // tb
    hb = tb // CONV_HALO
    nh = t // CONV_HALO
    kern = functools.partial(_conv_kernel, tb=tb, width=width)
    cur = lambda col: pl.BlockSpec((tb, ch), lambda i, f, l: (i, col))
    prev = lambda col: pl.BlockSpec((CONV_HALO, ch), lambda i, f, l: (jnp.maximum(i * hb - 1, 0), col))
    nxt = lambda col: pl.BlockSpec((CONV_HALO, ch), lambda i, f, l: (jnp.minimum((i + 1) * hb, nh - 1), col))
    vec = lambda: pl.BlockSpec((1, ch), lambda i, f, l: (0, 0))
    grid_spec = pltpu.PrefetchScalarGridSpec(
        num_scalar_prefetch=2,
        grid=(nblk,),
        in_specs=[cur(col_a), cur(col_g), prev(col_a), prev(col_g), nxt(col_a), nxt(col_g),
                  pl.BlockSpec((width, ch), lambda i, f, l: (0, 0)), vec(), vec(), vec()],
        out_specs=pl.BlockSpec((tb, ch), lambda i, f, l: (i, 0)),
        scratch_shapes=[pltpu.VMEM((tb + 2 * CONV_HALO, ch), F32)],
    )
    return pl.pallas_call(
        kern,
        grid_spec=grid_spec,
        out_shape=jax.ShapeDtypeStruct((t, ch), ACT_DTYPE),
        compiler_params=_cparams(("parallel",)),
        name="conv",
    )(seq_starts, seq_ends, u, u, u, u, u, u, conv_w, conv_b.reshape(1, ch), ln_g.reshape(1, ch), ln_b.reshape(1, ch))


def _merge_kernel(ya_ref, yb_ref, yc_ref, g0_ref, g1_ref, g2_ref, wb_ref, wo_ref, x_ref, gn_ref, gt_ref,
                  o_ref, m_ref, *, cw):
    d = o_ref.shape[1]
    ys = (ya_ref, yb_ref, yc_ref)
    gs = (g0_ref, g1_ref, g2_ref)
    for cb in range(d // cw):
        cs = slice(cb * cw, (cb + 1) * cw)
        acc = None
        for i in range(3):
            term = _sigmoid(gs[i][:, cs].astype(F32)) * _dot(ys[i][...], wb_ref[i, :, cs])
            acc = term if acc is None else acc + term
        m_ref[:, cs] = acc.astype(m_ref.dtype)
    y = _dot(m_ref[...], wo_ref[...])
    o_ref[...] = x_ref[...] + gt_ref[0] * _rms(y, gn_ref[...])


def _merge_call(ya, yb, yc, u, gate_col0, wb, wo, x, gn, gt, lv, tm_pref=256):
    t, d = x.shape
    bw = ya.shape[1]
    tm = _tile(lv, tm_pref)
    cw = _tile(d, 512, LANE)
    row = lambda i: (i, 0)
    gate = lambda k: pl.BlockSpec((tm, d), lambda i: (i, gate_col0 + k))
    const2 = lambda i: (0, 0)
    return pl.pallas_call(
        functools.partial(_merge_kernel, cw=cw),
        grid=(t // tm,),
        in_specs=[
            pl.BlockSpec((tm, bw), row), pl.BlockSpec((tm, bw), row), pl.BlockSpec((tm, bw), row),
            gate(0), gate(1), gate(2),
            pl.BlockSpec(wb.shape, lambda i: (0, 0, 0)),
            pl.BlockSpec(wo.shape, const2),
            pl.BlockSpec((tm, d), row),
            pl.BlockSpec((1, d), const2),
            pl.BlockSpec((1, 1, d), lambda i: (i * tm // lv, 0, 0)),
        ],
        out_specs=pl.BlockSpec((tm, d), row),
        out_shape=jax.ShapeDtypeStruct((t, d), F32),
        scratch_shapes=[pltpu.VMEM((tm, d), MXU_DTYPE)],
        compiler_params=_cparams(("parallel",)),
        name="merge",
    )(ya, yb, yc, u, u, u, wb, wo, x, gn, gt)


def _ffn_kernel(x_ref, g_ref, sh_ref, sc_ref, w1_ref, w3_ref, w2_ref, gn_ref, gt_ref, o_ref, h_ref, acc_ref):
    j = pl.program_id(1)

    @pl.when(j == 0)
    def _():
        h_ref[...] = _norm_mod(x_ref[...], g_ref[...], sh_ref[0], sc_ref[0]).astype(h_ref.dtype)
        acc_ref[...] = jnp.zeros_like(acc_ref)

    h = h_ref[...]
    a = _dot(h, w1_ref[...])
    g = _dot(h, w3_ref[...])
    m = (a * _sigmoid(a) * g).astype(MXU_DTYPE)
    acc_ref[...] += _dot(m, w2_ref[...])

    @pl.when(j == pl.num_programs(1) - 1)
    def _():
        o_ref[...] = x_ref[...] + gt_ref[0] * _rms(acc_ref[...], gn_ref[...])


def _ffn_call(x, g, sh, sc, w13, w2, gn, gt, lv, tm_pref=512, tf_pref=512):
    t, d = x.shape
    f = w2.shape[0]
    tm = _tile(lv, tm_pref)
    tf = _tile(f, tf_pref, LANE)
    nf = f // tf
    row = lambda i, j: (i, 0)
    const2 = lambda i, j: (0, 0)
    mod = lambda i, j: (i * tm // lv, 0, 0)
    return pl.pallas_call(
        _ffn_kernel,
        grid=(t // tm, nf),
        in_specs=[
            pl.BlockSpec((tm, d), row),
            pl.BlockSpec((1, d), const2),
            pl.BlockSpec((1, 1, d), mod),
            pl.BlockSpec((1, 1, d), mod),
            pl.BlockSpec((d, tf), lambda i, j: (0, j)),
            pl.BlockSpec((d, tf), lambda i, j: (0, nf + j)),
            pl.BlockSpec((tf, d), lambda i, j: (j, 0)),
            pl.BlockSpec((1, d), const2),
            pl.BlockSpec((1, 1, d), mod),
        ],
        out_specs=pl.BlockSpec((tm, d), row),
        out_shape=jax.ShapeDtypeStruct((t, d), F32),
        scratch_shapes=[pltpu.VMEM((tm, d), MXU_DTYPE), pltpu.VMEM((tm, d), F32)],
        compiler_params=_cparams(("parallel", "arbitrary")),
        name="ffn",
    )(x, g, sh, sc, w13, w13, w2, gn, gt)


ROUTE_ROWS = 8


def _pack_rows(x):
    half = x.shape[1] // 2
    bits = pltpu.bitcast(x.astype(jnp.bfloat16).astype(F32), jnp.uint32)
    return (bits[:, :half] >> 16) | bits[:, half:]


def _unpack_rows(p):
    lo = pltpu.bitcast(p << 16, F32)
    hi = pltpu.bitcast(p & jnp.uint32(0xFFFF0000), F32)
    return lo, hi


def _route_kernel(x_ref, g_ref, sh_ref, sc_ref, rw_ref, tri_ref, route_ref, rt_ref, cnt_ref, xs_hbm,
                  h_ref, run_ref, idx_ref, cntv_ref, cnts_ref, zrow_ref, sem_idx, sem, *, n_experts, cap, tb, tm):
    @pl.when(pl.program_id(0) == 0)
    def _():
        run_ref[...] = jnp.zeros_like(run_ref)

    h = _norm_mod(x_ref[...], g_ref[...], sh_ref[0], sc_ref[0])
    h_ref[...] = _pack_rows(h)
    logits = jnp.dot(h, rw_ref[...], preferred_element_type=F32, precision=lax.Precision.HIGHEST)
    lane = lax.broadcasted_iota(jnp.int32, logits.shape, 1)
    neg_inf = -jnp.inf
    l1 = jnp.where(lane < n_experts, logits, neg_inf)
    m1 = jnp.max(l1, axis=-1, keepdims=True)
    i1 = jnp.min(jnp.where(l1 == m1, lane, LANE), axis=-1, keepdims=True)
    l2 = jnp.where(lane == i1, neg_inf, l1)
    m2 = jnp.max(l2, axis=-1, keepdims=True)
    i2 = jnp.min(jnp.where(l2 == m2, lane, LANE), axis=-1, keepdims=True)
    e = jnp.exp(m2 - m1)
    w0 = 1.0 / (1.0 + e)
    w1 = e * w0
    sel1 = lane == i1
    sel2 = lane == i2
    member = jnp.where(sel1, 1.0, jnp.where(sel2, 1.0, 0.0))
    rank = _dot(tri_ref[...], member.astype(MXU_DTYPE))
    base = run_ref[...] + rank + lane.astype(F32) * float(cap)
    pos0 = jnp.sum(jnp.where(sel1, base, 0.0), axis=-1, keepdims=True)
    pos1 = jnp.sum(jnp.where(sel2, base, 0.0), axis=-1, keepdims=True)
    route = jnp.where(lane == 0, pos0, jnp.where(lane == 1, pos1, jnp.where(lane == 2, w0, jnp.where(lane == 3, w1, 0.0))))
    route_ref[...] = route
    rt_ref[...] = route.T[0:ROUTE_ROWS, :].astype(jnp.int32)
    run_ref[...] += jnp.sum(member, axis=0, keepdims=True)
    cnt_ref[...] = run_ref[...]

    cp = pltpu.make_async_copy(rt_ref, idx_ref, sem_idx)
    cp.start()
    cp.wait()

    def row_copy(k, dst_row):
        return pltpu.make_async_copy(h_ref.at[pl.ds(k, 1)], xs_hbm.at[pl.ds(dst_row, 1)], sem)

    def issue(k, carry):
        row_copy(k, idx_ref[0, k]).start()
        row_copy(k, idx_ref[1, k]).start()
        return carry

    lax.fori_loop(0, tb, issue, 0)

    def drain(k, carry):
        row_copy(0, 0).wait()
        row_copy(0, 0).wait()
        return carry

    lax.fori_loop(0, tb, drain, 0)

    @pl.when(pl.program_id(0) == pl.num_programs(0) - 1)
    def _():
        zrow_ref[...] = jnp.zeros_like(zrow_ref)
        cntv_ref[...] = jnp.broadcast_to(run_ref[...], cntv_ref.shape).astype(jnp.int32)
        cpc = pltpu.make_async_copy(cntv_ref, cnts_ref, sem_idx)
        cpc.start()
        cpc.wait()

        def zero_copy(dst_row):
            return pltpu.make_async_copy(zrow_ref.at[pl.ds(0, 1)], xs_hbm.at[pl.ds(dst_row, 1)], sem)

        for e in range(n_experts):
            cnt = cnts_ref[0, e]
            end = ((cnt + tm - 1) // tm) * tm

            def zissue(r, carry, e=e):
                zero_copy(e * cap + r).start()
                return carry

            def zdrain(r, carry):
                zero_copy(0).wait()
                return carry

            lax.fori_loop(cnt, end, zissue, 0)
            lax.fori_loop(cnt, end, zdrain, 0)


def _route_call(x, g, sh, sc, router_w, lv, cap, tm, tb_pref=256):
    t, d = x.shape
    n_experts = router_w.shape[1]
    tb = _tile(lv, tb_pref, LANE)
    rw = jnp.zeros((d, LANE), F32).at[:, :n_experts].set(router_w)
    idx = np.arange(tb)
    tri = jnp.asarray((idx[None, :] < idx[:, None]).astype(np.float32), MXU_DTYPE)
    row = lambda i: (i, 0)
    const2 = lambda i: (0, 0)
    mod = lambda i: (i * tb // lv, 0, 0)
    return pl.pallas_call(
        functools.partial(_route_kernel, n_experts=n_experts, cap=cap, tb=tb, tm=tm),
        grid=(t // tb,),
        in_specs=[
            pl.BlockSpec((tb, d), row),
            pl.BlockSpec((1, d), const2),
            pl.BlockSpec((1, 1, d), mod),
            pl.BlockSpec((1, 1, d), mod),
            pl.BlockSpec((d, LANE), const2),
            pl.BlockSpec((tb, tb), const2),
        ],
        out_specs=[
            pl.BlockSpec((tb, LANE), row),
            pl.BlockSpec((ROUTE_ROWS, tb), lambda i: (0, i)),
            pl.BlockSpec((1, LANE), const2),
            pl.BlockSpec(memory_space=pl.ANY),
        ],
        out_shape=[
            jax.ShapeDtypeStruct((t, LANE), F32),
            jax.ShapeDtypeStruct((ROUTE_ROWS, t), jnp.int32),
            jax.ShapeDtypeStruct((1, LANE), F32),
            jax.ShapeDtypeStruct((n_experts * cap, d // 2), jnp.uint32),
        ],
        scratch_shapes=[
            pltpu.VMEM((tb, d // 2), jnp.uint32),
            pltpu.VMEM((1, LANE), F32),
            pltpu.SMEM((ROUTE_ROWS, tb), jnp.int32),
            pltpu.VMEM((SUBLANE, LANE), jnp.int32),
            pltpu.SMEM((SUBLANE, LANE), jnp.int32),
            pltpu.VMEM((SUBLANE, d // 2), jnp.uint32),
            pltpu.SemaphoreType.DMA,
            pltpu.SemaphoreType.DMA,
        ],
        compiler_params=_cparams(("arbitrary",)),
        name="route",
    )(x, g, sh, sc, rw, tri)


def _gmm_kernel(be_ref, br_ref, bv_ref, xs_ref, w1_ref, w3_ref, w2_ref, y_ref, xb_ref, acc_ref):
    i = pl.program_id(0)
    j = pl.program_id(1)

    @pl.when(bv_ref[i] == 1)
    def _():
        @pl.when(j == 0)
        def _():
            half = xs_ref.shape[1]
            lo, hi = _unpack_rows(xs_ref[...])
            xb_ref[:, :half] = lo.astype(xb_ref.dtype)
            xb_ref[:, half:] = hi.astype(xb_ref.dtype)
            acc_ref[...] = jnp.zeros_like(acc_ref)

        xb = xb_ref[...]
        a = _dot(xb, w1_ref[0])
        g = _dot(xb, w3_ref[0])
        m = (a * _sigmoid(a) * g).astype(MXU_DTYPE)
        acc_ref[...] += _dot(m, w2_ref[0])

        @pl.when(j == pl.num_programs(1) - 1)
        def _():
            y_ref[...] = _pack_rows(acc_ref[...])


def _gmm_call(blk_e, blk_row, blk_valid, xs, w13, w2, tm, tf_pref=1024):
    n_rows, half = xs.shape
    d = 2 * half
    f = w2.shape[1]
    tf = _tile(f, tf_pref, LANE)
    nf = f // tf
    n_blk = blk_e.shape[0]

    def jsel(i, j, bv):
        return jnp.where(bv[i] == 1, j, nf - 1)

    grid_spec = pltpu.PrefetchScalarGridSpec(
        num_scalar_prefetch=3,
        grid=(n_blk, nf),
        in_specs=[
            pl.BlockSpec((tm, half), lambda i, j, be, br, bv: (br[i], 0)),
            pl.BlockSpec((1, d, tf), lambda i, j, be, br, bv: (be[i], 0, jsel(i, j, bv))),
            pl.BlockSpec((1, d, tf), lambda i, j, be, br, bv: (be[i], 0, nf + jsel(i, j, bv))),
            pl.BlockSpec((1, tf, d), lambda i, j, be, br, bv: (be[i], jsel(i, j, bv), 0)),
        ],
        out_specs=pl.BlockSpec((tm, half), lambda i, j, be, br, bv: (br[i], 0)),
        scratch_shapes=[pltpu.VMEM((tm, d), MXU_DTYPE), pltpu.VMEM((tm, d), F32)],
    )
    return pl.pallas_call(
        _gmm_kernel,
        grid_spec=grid_spec,
        out_shape=jax.ShapeDtypeStruct((n_rows, half), jnp.uint32),
        compiler_params=_cparams(("arbitrary", "arbitrary")),
        name="moe_gmm",
    )(blk_e, blk_row, blk_valid, xs, w13, w13, w2)


def _combine_kernel(rt_ref, route_ref, x_ref, gn_ref, gt_ref, y_hbm, o_ref, idx_ref, y0_ref, y1_ref, sem_idx, sem,
                    *, tb):
    cp = pltpu.make_async_copy(rt_ref, idx_ref, sem_idx)
    cp.start()
    cp.wait()

    def row_copy(src_row, dst_ref, k, s):
        return pltpu.make_async_copy(y_hbm.at[pl.ds(src_row, 1)], dst_ref.at[pl.ds(k, 1)], sem.at[s])

    def issue(k, carry):
        row_copy(idx_ref[0, k], y0_ref, k, 0).start()
        row_copy(idx_ref[1, k], y1_ref, k, 1).start()
        return carry

    lax.fori_loop(0, tb, issue, 0)

    def drain(k, carry):
        row_copy(0, y0_ref, k, 0).wait()
        row_copy(0, y1_ref, k, 1).wait()
        return carry

    lax.fori_loop(0, tb, drain, 0)
    route = route_ref[...]
    w0 = route[:, 2:3]
    w1 = route[:, 3:4]
    lo0, hi0 = _unpack_rows(y0_ref[...])
    lo1, hi1 = _unpack_rows(y1_ref[...])
    y = jnp.concatenate([w0 * lo0 + w1 * lo1, w0 * hi0 + w1 * hi1], axis=1)
    o_ref[...] = x_ref[...] + gt_ref[0] * _rms(y, gn_ref[...])


def _combine_call(rt, route, x, gn, gt, y, lv, tb_pref=256):
    t, d = x.shape
    tb = _tile(lv, tb_pref, LANE)
    row = lambda i: (i, 0)
    return pl.pallas_call(
        functools.partial(_combine_kernel, tb=tb),
        grid=(t // tb,),
        in_specs=[
            pl.BlockSpec((ROUTE_ROWS, tb), lambda i: (0, i)),
            pl.BlockSpec((tb, LANE), row),
            pl.BlockSpec((tb, d), row),
            pl.BlockSpec((1, d), lambda i: (0, 0)),
            pl.BlockSpec((1, 1, d), lambda i: (i * tb // lv, 0, 0)),
            pl.BlockSpec(memory_space=pl.ANY),
        ],
        out_specs=pl.BlockSpec((tb, d), row),
        out_shape=jax.ShapeDtypeStruct((t, d), F32),
        scratch_shapes=[
            pltpu.SMEM((ROUTE_ROWS, tb), jnp.int32),
            pltpu.VMEM((tb, d // 2), jnp.uint32),
            pltpu.VMEM((tb, d // 2), jnp.uint32),
            pltpu.SemaphoreType.DMA,
            pltpu.SemaphoreType.DMA((2,)),
        ],
        compiler_params=_cparams(("arbitrary",)),
        name="moe_combine",
    )(rt, route, x, gn, gt, y)


def _moe(x, g, sh, sc, router_w, w13, w2, gn, gt, lv, tm_pref=512):
    t, d = x.shape
    n_experts = router_w.shape[1]
    tm = _tile(t, tm_pref, LANE)
    cap = -(-t // tm) * tm
    route, rt, cnt, xs = _route_call(x, g, sh, sc, router_w, lv, cap, tm)
    counts = cnt[0, :n_experts].astype(jnp.int32)
    nb = (counts + tm - 1) // tm
    ends = jnp.cumsum(nb)
    n_blk = TOP_K * t // tm + n_experts
    bi = jnp.arange(n_blk, dtype=jnp.int32)
    valid = bi < ends[-1]
    bi_c = jnp.minimum(bi, ends[-1] - 1)
    be = jnp.minimum(jnp.searchsorted(ends, bi_c, side="right"), n_experts - 1).astype(jnp.int32)
    br = be * (cap // tm) + (bi_c - (ends - nb)[be])
    y = _gmm_call(be, br.astype(jnp.int32), valid.astype(jnp.int32), xs, w13, w2, tm)
    return _combine_call(rt, route, x, gn, gt, y, lv)


def kernel(x_prompt, x_sample, c_prompt, c_sample, ada_w, ada_b, norm_g, w_in, na_rpb, hgrn_lb, hgrn_norm_g,
           conv_w, conv_b, conv_ln_g, conv_ln_b, w_branch, w_out, ffn_w13, ffn_w2, router_w, moe_w13, moe_w2):
    bp, lp, d = x_prompt.shape
    bs, ls, _ = x_sample.shape
    depth = ada_w.shape[0]
    bw = w_branch.shape[2]
    tp = bp * lp
    t = tp + bs * ls
    lv = math.gcd(lp, ls)
    n_vseq = t // lv
    assert 8 * bw % d == 0 and d % LANE == 0 and bw % LANE == 0

    x = jnp.concatenate([x_prompt.reshape(tp, d), x_sample.reshape(bs * ls, d)], axis=0)

    nb = bp + bs
    rows = -(-nb // SUBLANE) * SUBLANE
    c_all = jnp.zeros((rows, d), F32).at[:nb].set(jnp.concatenate([c_prompt, c_sample], axis=0))
    mod = _ada(c_all, ada_w, ada_b)
    starts = np.arange(n_vseq) * lv
    vb = np.where(starts < tp, starts // lp, bp + (starts - tp) // ls)
    mod_v = mod[:, vb, :].reshape(depth, n_vseq, 1, 6, d)
    part = lambda l, k: mod_v[l, :, :, k, :]

    sm = jax.nn.softmax(hgrn_lb.astype(F32), axis=0)
    lb_all = jnp.clip(jnp.cumsum(sm, axis=0) - sm[0], 0.0, 1.0)

    cb = lambda k: slice(k * bw, (k + 1) * bw)
    order = [0, 1, 2, 3, 4, 7, 8, 9]
    conv_tb = _tile(lv, 256)
    blk_start = np.arange(t // conv_tb) * conv_tb
    seq_pos = np.where(blk_start < tp, blk_start % lp, (blk_start - tp) % ls)
    seq_len_of = np.where(blk_start < tp, lp, ls)
    seq_starts = jnp.asarray((seq_pos == 0).astype(np.int32))
    seq_ends = jnp.asarray((seq_pos + conv_tb == seq_len_of).astype(np.int32))

    for l in range(depth):
        wl = w_in[l]
        w_a = jnp.concatenate([wl[:, cb(k)] for k in order] + [wl[:, 10 * bw:]], axis=1).astype(MXU_DTYPE)
        w_f = wl[:, 5 * bw:7 * bw].astype(MXU_DTYPE)
        g0 = norm_g[l, 0].reshape(1, d)
        u = _nm_matmul(x, g0, part(l, 0), part(l, 1), w_a, ACT_DTYPE, lv)
        uf = _nm_matmul(x, g0, part(l, 0), part(l, 1), w_f, F32, lv)

        bias_tab = _na_bias_table(na_rpb[l])
        ya = _na_call(u, bias_tab, None, 0, bp, lp)
        ya = _na_call(u, bias_tab, ya, tp, bs, ls)

        cols_b = (3, 4, 5, 1)
        cols_f = (3, 4, 5, 0)
        ob = _hgrn_call(u, uf, lb_all[l, 1], None, None, None, 0, bp, lp, True, False, cols_b)
        ob = _hgrn_call(u, uf, lb_all[l, 1], None, None, ob, tp, bs, ls, True, False, cols_b)
        yb = _hgrn_call(u, uf, lb_all[l, 0], hgrn_norm_g[l], ob, None, 0, bp, lp, False, True, cols_f)
        yb = _hgrn_call(u, uf, lb_all[l, 0], hgrn_norm_g[l], ob, yb, tp, bs, ls, False, True, cols_f)

        yc = _conv_call(u, conv_w[l], conv_b[l], conv_ln_g[l], conv_ln_b[l], seq_starts, seq_ends, 6, 7, conv_tb)

        x = _merge_call(ya, yb, yc, u, 8 * bw // d, w_branch[l].astype(MXU_DTYPE), w_out[l].astype(MXU_DTYPE),
                        x, norm_g[l, 1].reshape(1, d), part(l, 2), lv)

        g2 = norm_g[l, 2].reshape(1, d)
        g3 = norm_g[l, 3].reshape(1, d)
        if l % 2 == 0:
            x = _ffn_call(x, g2, part(l, 3), part(l, 4), ffn_w13[l // 2].astype(MXU_DTYPE),
                          ffn_w2[l // 2].astype(MXU_DTYPE), g3, part(l, 5), lv)
        else:
            x = _moe(x, g2, part(l, 3), part(l, 4), router_w[l // 2], moe_w13[l // 2].astype(MXU_DTYPE),
                     moe_w2[l // 2].astype(MXU_DTYPE), g3, part(l, 5), lv)

    return (x[:tp].reshape(bp, lp, d), x[tp:].reshape(bs, ls, d))
```

```python
import functools
import math

import numpy as np
import jax
import jax.numpy as jnp
from jax import lax
from jax.experimental import pallas as pl
from jax.experimental.pallas import tpu as pltpu

GRID_W = 64
NA_HEAD_DIM = 64
NA_ROWS = 8
NA_COLS = 16
HGRN_DK = 128
TOP_K = 2
EPS = 1e-6
F_MIN = 1e-30
NEG_BIG = -1e30

LANE = 128
SUBLANE = 8
VMEM_LIMIT_BYTES = 56 * 1024 * 1024

MXU_DTYPE = jnp.bfloat16
ACT_DTYPE = jnp.bfloat16

HGRN_CHUNK = 128
HGRN_MATMUL_LEVELS = 3
NA_ROW_UNROLL = 4
NA_QGROUP = 16
F32 = jnp.float32


def _cparams(sem, vmem=VMEM_LIMIT_BYTES):
    return pltpu.CompilerParams(dimension_semantics=sem, vmem_limit_bytes=vmem)


def _sigmoid(x):
    return 1.0 / (1.0 + jnp.exp(-x))


def _dot(a, b):
    return jnp.dot(a, b, preferred_element_type=F32)


def _dot_nt(a, b):
    return lax.dot_general(a, b, (((1,), (1,)), ((), ())), preferred_element_type=F32)


def _dot_tn(a, b):
    return lax.dot_general(a, b, (((0,), (0,)), ((), ())), preferred_element_type=F32)


def _rms(x, g):
    return x * lax.rsqrt(jnp.mean(x * x, axis=-1, keepdims=True) + EPS) * g


def _norm_mod(x, g, sh, sc):
    return _rms(x, g) * (1.0 + sc) + sh


def _tile(n, pref, mult=SUBLANE):
    if n <= pref:
        return n
    t = (pref // mult) * mult
    while t >= mult:
        if n % t == 0:
            return t
        t -= mult
    return n


def _ada_kernel(c_ref, w_ref, b_ref, o_ref):
    c = c_ref[...]
    cs = (c * _sigmoid(c)).astype(MXU_DTYPE)
    o_ref[0] = _dot(cs, w_ref[0].astype(MXU_DTYPE)) + b_ref[0]


def _ada(c_all, ada_w, ada_b):
    depth, d, n = ada_w.shape
    rows = c_all.shape[0]
    tn = _tile(n, 1024, LANE)
    return pl.pallas_call(
        _ada_kernel,
        grid=(depth, n // tn),
        in_specs=[
            pl.BlockSpec((rows, d), lambda l, j: (0, 0)),
            pl.BlockSpec((1, d, tn), lambda l, j: (l, 0, j)),
            pl.BlockSpec((1, 1, tn), lambda l, j: (l, 0, j)),
        ],
        out_specs=pl.BlockSpec((1, rows, tn), lambda l, j: (l, 0, j)),
        out_shape=jax.ShapeDtypeStruct((depth, rows, n), F32),
        compiler_params=_cparams(("parallel", "parallel")),
        name="ada",
    )(c_all, ada_w, ada_b.reshape(depth, 1, n))


def _nm_matmul_kernel(x_ref, g_ref, sh_ref, sc_ref, w_ref, oa_ref, of_ref, h_ref, *, nja):
    j = pl.program_id(1)

    @pl.when(j == 0)
    def _():
        h_ref[...] = _norm_mod(x_ref[...], g_ref[...], sh_ref[0], sc_ref[0]).astype(h_ref.dtype)

    r = _dot(h_ref[...], w_ref[...])

    @pl.when(j < nja)
    def _():
        oa_ref[...] = r.astype(oa_ref.dtype)

    @pl.when(j >= nja)
    def _():
        of_ref[...] = r


def _nm_matmul(x, g, sh, sc, w, n_a, lv, tm_pref=1024, tn_pref=1024):
    t, d = x.shape
    n = w.shape[1]
    n_f = n - n_a
    tm = _tile(lv, tm_pref)
    tn = _tile(math.gcd(n_a, n_f), tn_pref, LANE)
    nja = n_a // tn
    return pl.pallas_call(
        functools.partial(_nm_matmul_kernel, nja=nja),
        grid=(t // tm, n // tn),
        in_specs=[
            pl.BlockSpec((tm, d), lambda i, j: (i, 0)),
            pl.BlockSpec((1, d), lambda i, j: (0, 0)),
            pl.BlockSpec((1, 1, d), lambda i, j: (i * tm // lv, 0, 0)),
            pl.BlockSpec((1, 1, d), lambda i, j: (i * tm // lv, 0, 0)),
            pl.BlockSpec((d, tn), lambda i, j: (0, j)),
        ],
        out_specs=[
            pl.BlockSpec((tm, tn), lambda i, j: (i, jnp.minimum(j, nja - 1))),
            pl.BlockSpec((tm, tn), lambda i, j: (i, jnp.maximum(j - nja, 0))),
        ],
        out_shape=[jax.ShapeDtypeStruct((t, n_a), ACT_DTYPE), jax.ShapeDtypeStruct((t, n_f), F32)],
        scratch_shapes=[pltpu.VMEM((tm, d), MXU_DTYPE)],
        compiler_params=_cparams(("parallel", "arbitrary")),
        name="nm_matmul",
    )(x, g, sh, sc, w)


def _na_bias_table(rpb):
    cols = np.arange(GRID_W)
    col_start = np.clip(cols - NA_COLS // 2, 0, GRID_W - NA_COLS)
    in_win = (cols[None, :] >= col_start[:, None]) & (cols[None, :] < col_start[:, None] + NA_COLS)
    dc = cols[None, :] - cols[:, None] + (NA_COLS - 1)
    onehot = ((dc[None] == np.arange(2 * NA_COLS - 1)[:, None, None]) & in_win[None]).astype(np.float32)
    full = jnp.einsum("hrd,dqk->hrqk", rpb.astype(F32), jnp.asarray(onehot), precision=lax.Precision.HIGHEST)
    tabs = []
    for delta in range(NA_ROWS):
        rows = full[:, NA_ROWS - 1 - delta:2 * NA_ROWS - 1 - delta]
        tabs.append(rows.transpose(0, 2, 1, 3).reshape(rpb.shape[0], GRID_W, NA_ROWS * GRID_W))
    mask = jnp.asarray(np.tile(in_win, (1, NA_ROWS)))
    return jnp.where(mask[None, None], jnp.stack(tabs, axis=0), NEG_BIG)


def _na_kernel(q_ref, k_ref, v_ref, bias_ref, o_ref, s_ref, p_ref, *, rows):
    win = NA_ROWS * GRID_W
    lane = lax.broadcasted_iota(jnp.int32, (1, LANE), 1)
    head_masks = [lane < NA_HEAD_DIM, lane >= NA_HEAD_DIM]
    scale = NA_HEAD_DIM ** -0.5

    def body(i, carry):
        r0 = i * NA_ROW_UNROLL
        starts = [jnp.clip(r0 + u - NA_ROWS // 2, 0, rows - NA_ROWS) for u in range(NA_ROW_UNROLL)]
        for u in range(NA_ROW_UNROLL):
            q2 = q_ref[pl.ds(pl.multiple_of((r0 + u) * GRID_W, GRID_W), GRID_W), :]
            q2 = q2 * jnp.asarray(scale, q2.dtype)
            kw = k_ref[pl.ds(pl.multiple_of(starts[u] * GRID_W, GRID_W), win), :]
            qq = jnp.concatenate([jnp.where(head_masks[h], q2, jnp.zeros_like(q2)) for h in range(2)], axis=0)
            s_ref[u] = _dot_nt(qq, kw)
        rden = []
        for u in range(NA_ROW_UNROLL):
            delta = r0 + u - starts[u]
            parts = []
            for g in range(2 * GRID_W // NA_QGROUP):
                h, gq = divmod(g * NA_QGROUP, GRID_W)
                s = s_ref[u, pl.ds(g * NA_QGROUP, NA_QGROUP), :] + bias_ref[delta, h, pl.ds(gq, NA_QGROUP), :]
                m = jnp.max(s, axis=-1, keepdims=True)
                e = jnp.exp(s - m)
                parts.append(1.0 / jnp.sum(e, axis=-1, keepdims=True))
                p_ref[u, pl.ds(g * NA_QGROUP, NA_QGROUP), :] = e.astype(p_ref.dtype)
            rden.append(jnp.concatenate(parts, axis=0))
        for u in range(NA_ROW_UNROLL):
            vw = v_ref[pl.ds(pl.multiple_of(starts[u] * GRID_W, GRID_W), win), :]
            o2 = _dot(p_ref[u], vw) * rden[u]
            out = jnp.where(head_masks[0], o2[:GRID_W], o2[GRID_W:])
            o_ref[pl.ds(pl.multiple_of((r0 + u) * GRID_W, GRID_W), GRID_W), :] = out.astype(o_ref.dtype)
        return carry

    lax.fori_loop(0, rows // NA_ROW_UNROLL, body, 0)


def _na_call(u, bias_tab, y_prev, tok_off, nseq, seq_len):
    t = u.shape[0]
    bw = bias_tab.shape[1] * NA_HEAD_DIM
    npair = bw // LANE
    rows = seq_len // GRID_W
    assert rows >= NA_ROWS and rows % NA_ROW_UNROLL == 0 and seq_len % GRID_W == 0 and tok_off % seq_len == 0
    s0 = tok_off // seq_len
    kern = functools.partial(_na_kernel, rows=rows)
    in_specs = [
        pl.BlockSpec((seq_len, LANE), lambda b, p: (s0 + b, p)),
        pl.BlockSpec((seq_len, LANE), lambda b, p: (s0 + b, npair + p)),
        pl.BlockSpec((seq_len, LANE), lambda b, p: (s0 + b, 2 * npair + p)),
        pl.BlockSpec((NA_ROWS, 2, GRID_W, NA_ROWS * GRID_W), lambda b, p: (0, p, 0, 0)),
    ]
    args = [u, u, u, bias_tab]
    aliases = {}
    if y_prev is not None:
        in_specs.append(pl.BlockSpec(memory_space=pl.ANY))
        args.append(y_prev)
        aliases = {4: 0}
        kern_fn = lambda q, k, v, b, _prev, o, s, p: kern(q, k, v, b, o, s, p)
    else:
        kern_fn = kern
    return pl.pallas_call(
        kern_fn,
        grid=(nseq, npair),
        in_specs=in_specs,
        out_specs=pl.BlockSpec((seq_len, LANE), lambda b, p: (s0 + b, p)),
        out_shape=jax.ShapeDtypeStruct((t, bw), ACT_DTYPE),
        scratch_shapes=[
            pltpu.VMEM((NA_ROW_UNROLL, 2 * GRID_W, NA_ROWS * GRID_W), F32),
            pltpu.VMEM((NA_ROW_UNROLL, 2 * GRID_W, NA_ROWS * GRID_W), MXU_DTYPE),
        ],
        input_output_aliases=aliases,
        compiler_params=_cparams(("parallel", "parallel")),
        name="na",
    )(*args)


def _hgrn_consts(c, reverse):
    nlev = int(math.log2(c))
    nmat = min(HGRN_MATMUL_LEVELS, nlev)
    idx = np.arange(c)
    tri = (idx[None, :] <= idx[:, None]).astype(np.float32)
    mats = [tri]
    for l in range(nmat):
        m = 1 << l
        ref = (idx // (2 * m)) * (2 * m) + m - 1
        mats.append(tri[ref])
    mall = np.concatenate(mats, axis=0)
    x = idx[:, None] ^ idx[None, :]
    lvl = np.where(x > 0, np.floor(np.log2(np.maximum(x, 1))), -1).astype(np.int32)
    lvl = np.where(idx[None, :] > idx[:, None], -2, lvl)
    if reverse:
        mall = mall.reshape(nmat + 1, c, c)[:, ::-1, ::-1].reshape((nmat + 1) * c, c)
        lvl = lvl[::-1, ::-1]
    return np.ascontiguousarray(np.tile(mall, (1, 3))), np.ascontiguousarray(lvl), nlev


def _hgrn_direction(q_ref, i_ref, f_ref, lb, mall, lvl, st_ref, d, *, c, nlev, nheads, reverse):
    fg = jnp.maximum(lb + (1.0 - lb) * _sigmoid(f_ref[...]), F_MIN)
    lg = jnp.log2(fg)
    kk = 1.0 - fg
    hi = lg.astype(MXU_DTYPE)
    r1 = lg - hi.astype(F32)
    mid = r1.astype(MXU_DTYPE)
    lo = (r1 - mid.astype(F32)).astype(MXU_DTYPE)
    ball = _dot(mall, jnp.concatenate([hi, mid, lo], axis=0))
    nmat = min(HGRN_MATMUL_LEVELS, nlev)
    b_all = ball[0:c]
    bref = [ball[(l + 1) * c:(l + 2) * c] for l in range(nmat)]
    for l in range(nmat, nlev):
        m = 1 << l
        blocks = []
        for g0 in range(0, c, 2 * m):
            r = g0 + m if reverse else g0 + m - 1
            blocks.append(jnp.broadcast_to(b_all[r:r + 1, :], (2 * m, b_all.shape[1])))
        bref.append(blocks[0] if len(blocks) == 1 else jnp.concatenate(blocks, axis=0))
    q = q_ref[...].astype(F32) * (HGRN_DK ** -0.5)
    v = i_ref[...].astype(MXU_DTYPE)
    tot_row = 0 if reverse else c - 1
    outs = []
    for h in range(nheads):
        sl = slice(h * HGRN_DK, (h + 1) * HGRN_DK)
        bh = b_all[:, sl]
        qh = q[:, sl]
        kh = kk[:, sl]
        vh = v[:, sl]
        a = jnp.where(lvl == -1, _dot_nt(qh.astype(MXU_DTYPE), kh.astype(MXU_DTYPE)), 0.0)
        for l in range(nlev):
            e = jnp.exp2(-jnp.abs(bh - bref[l][:, sl]))
            a = jnp.where(lvl == l, _dot_nt((qh * e).astype(MXU_DTYPE), (kh * e).astype(MXU_DTYPE)), a)
        o = _dot(a.astype(MXU_DTYPE), vh)
        st = st_ref[d * nheads + h]
        o = o + _dot_nt((qh * jnp.exp2(bh)).astype(MXU_DTYPE), st.astype(MXU_DTYPE))
        btot = bh[tot_row:tot_row + 1, :]
        kd = (kh * jnp.exp2(btot - bh)).astype(MXU_DTYPE)
        st_ref[d * nheads + h] = st * jnp.exp2(btot) + _dot_tn(vh, kd)
        outs.append(o)
    return outs


def _hgrn_kernel(fblk_ref, rblk_ref, first_ref, qf_ref, if_ref, ff_ref, qr_ref, ir_ref, fr_ref, lb_ref, mall_ref, lvl_ref,
                 of_ref, or_ref, st_ref, *, c, nlev, nheads):
    @pl.when(first_ref[pl.program_id(0)] == 1)
    def _():
        st_ref[...] = jnp.zeros_like(st_ref)

    dirs = ((qf_ref, if_ref, ff_ref, of_ref, False), (qr_ref, ir_ref, fr_ref, or_ref, True))
    for d, (q_ref, i_ref, f_ref, o_ref, reverse) in enumerate(dirs):
        outs = _hgrn_direction(q_ref, i_ref, f_ref, lb_ref[d:d + 1, :], mall_ref[d], lvl_ref[d], st_ref, d,
                               c=c, nlev=nlev, nheads=nheads, reverse=reverse)
        for h in range(nheads):
            o_ref[:, h * HGRN_DK:(h + 1) * HGRN_DK] = outs[h]


def _hgrn_call(u, uf, lb, seqs, cols):
    t = u.shape[0]
    bw = lb.shape[-1]
    nheads = bw // HGRN_DK
    c = min([HGRN_CHUNK] + [n for _, _, n in seqs])
    fblk, rblk, first = [], [], []
    for off, cnt, n in seqs:
        nck = n // c
        for b in range(cnt):
            base = (off + b * n) // c
            fblk += [base + k for k in range(nck)]
            rblk += [base + nck - 1 - k for k in range(nck)]
            first += [1] + [0] * (nck - 1)
    consts = [_hgrn_consts(c, rev) for rev in (False, True)]
    nlev = consts[0][2]
    mall = jnp.asarray(np.stack([cc[0] for cc in consts]), MXU_DTYPE)
    lvl = jnp.asarray(np.stack([cc[1] for cc in consts]))
    cq, ci, cff, cfb = cols
    fwd = lambda col: pl.BlockSpec((c, bw), lambda s, fb, rb, fr: (fb[s], col))
    rev = lambda col: pl.BlockSpec((c, bw), lambda s, fb, rb, fr: (rb[s], col))
    const2 = lambda s, fb, rb, fr: (0, 0)
    const3 = lambda s, fb, rb, fr: (0, 0, 0)
    grid_spec = pltpu.PrefetchScalarGridSpec(
        num_scalar_prefetch=3,
        grid=(len(fblk),),
        in_specs=[fwd(cq), fwd(ci), fwd(cff), rev(cq), rev(ci), rev(cfb),
                  pl.BlockSpec((2, bw), const2), pl.BlockSpec(mall.shape, const3), pl.BlockSpec(lvl.shape, const3)],
        out_specs=[fwd(0), rev(0)],
        scratch_shapes=[pltpu.VMEM((2 * nheads, HGRN_DK, HGRN_DK), F32)],
    )
    tab = lambda z: jnp.asarray(np.asarray(z, np.int32))
    return pl.pallas_call(
        functools.partial(_hgrn_kernel, c=c, nlev=nlev, nheads=nheads),
        grid_spec=grid_spec,
        out_shape=[jax.ShapeDtypeStruct((t, bw), F32), jax.ShapeDtypeStruct((t, bw), F32)],
        compiler_params=_cparams(("arbitrary",)),
        name="hgrn",
    )(tab(fblk), tab(rblk), tab(first), u, u, uf, u, u, uf, lb, mall, lvl)


CONV_HALO = 16
CONV_ROWS = 32


def _conv_kernel(first_ref, last_ref, a_ref, g_ref, ap_ref, gp_ref, an_ref, gn_ref,
                 w_ref, b_ref, lg_ref, lb_ref, o_ref, hbuf, *, tb, width):
    i = pl.program_id(0)
    pad = width // 2

    def glu(a, g):
        return a.astype(F32) * _sigmoid(g.astype(F32))

    hbuf[CONV_HALO:CONV_HALO + tb, :] = glu(a_ref[...], g_ref[...])
    hp = glu(ap_ref[...], gp_ref[...])
    hbuf[0:CONV_HALO, :] = jnp.where(first_ref[i] == 1, 0.0, hp)
    hn = glu(an_ref[...], gn_ref[...])
    hbuf[CONV_HALO + tb:2 * CONV_HALO + tb, :] = jnp.where(last_ref[i] == 1, 0.0, hn)
    w = w_ref[...]
    for r0 in range(0, tb, CONV_ROWS):
        acc = jnp.zeros((CONV_ROWS, w.shape[1]), F32)
        for j in range(width):
            s = CONV_HALO + r0 + j - pad
            acc = acc + w[j:j + 1, :] * hbuf[s:s + CONV_ROWS, :]
        h = acc + b_ref[...]
        mu = jnp.mean(h, axis=-1, keepdims=True)
        hc = h - mu
        var = jnp.mean(hc * hc, axis=-1, keepdims=True)
        y = hc * lax.rsqrt(var + EPS) * lg_ref[...] + lb_ref[...]
        o_ref[r0:r0 + CONV_ROWS, :] = (y * _sigmoid(y)).astype(o_ref.dtype)


def _conv_call(u, conv_w, conv_b, ln_g, ln_b, seq_starts, seq_ends, col_a, col_g, tb):
    t = u.shape[0]
    width, ch = conv_w.shape
    assert width // 2 < CONV_HALO and tb % CONV_ROWS == 0
    nblk = t // tb
    hb = tb // CONV_HALO
    nh = t // CONV_HALO
    kern = functools.partial(_conv_kernel, tb=tb, width=width)
    cur = lambda col: pl.BlockSpec((tb, ch), lambda i, f, l: (i, col))
    prev = lambda col: pl.BlockSpec((CONV_HALO, ch), lambda i, f, l: (jnp.maximum(i * hb - 1, 0), col))
    nxt = lambda col: pl.BlockSpec((CONV_HALO, ch), lambda i, f, l: (jnp.minimum((i + 1) * hb, nh - 1), col))
    vec = lambda: pl.BlockSpec((1, ch), lambda i, f, l: (0, 0))
    grid_spec = pltpu.PrefetchScalarGridSpec(
        num_scalar_prefetch=2,
        grid=(nblk,),
        in_specs=[cur(col_a), cur(col_g), prev(col_a), prev(col_g), nxt(col_a), nxt(col_g),
                  pl.BlockSpec((width, ch), lambda i, f, l: (0, 0)), vec(), vec(), vec()],
        out_specs=pl.BlockSpec((tb, ch), lambda i, f, l: (i, 0)),
        scratch_shapes=[pltpu.VMEM((tb + 2 * CONV_HALO, ch), F32)],
    )
    return pl.pallas_call(
        kern,
        grid_spec=grid_spec,
        out_shape=jax.ShapeDtypeStruct((t, ch), ACT_DTYPE),
        compiler_params=_cparams(("parallel",)),
        name="conv",
    )(seq_starts, seq_ends, u, u, u, u, u, u, conv_w, conv_b.reshape(1, ch), ln_g.reshape(1, ch), ln_b.reshape(1, ch))


def _merge_kernel(ya_ref, of_ref, or_ref, og_ref, ng_ref, yc_ref, g0_ref, g1_ref, g2_ref, wb_ref, wo_ref, x_ref,
                  gn_ref, gt_ref, o_ref, yb_ref, m_ref, *, cw):
    d = o_ref.shape[1]
    o = of_ref[...] + or_ref[...]
    og = og_ref[...].astype(F32)
    gate = ng_ref[...] * (og * _sigmoid(og))
    for h in range(o.shape[1] // HGRN_DK):
        sl = slice(h * HGRN_DK, (h + 1) * HGRN_DK)
        oh = o[:, sl]
        oh = oh * lax.rsqrt(jnp.mean(oh * oh, axis=-1, keepdims=True) + EPS)
        yb_ref[:, sl] = (oh * gate[:, sl]).astype(yb_ref.dtype)
    ys = (ya_ref, yb_ref, yc_ref)
    gs = (g0_ref, g1_ref, g2_ref)
    for cb in range(d // cw):
        cs = slice(cb * cw, (cb + 1) * cw)
        acc = None
        for i in range(3):
            term = _sigmoid(gs[i][:, cs].astype(F32)) * _dot(ys[i][...], wb_ref[i, :, cs])
            acc = term if acc is None else acc + term
        m_ref[:, cs] = acc.astype(m_ref.dtype)
    y = _dot(m_ref[...], wo_ref[...])
    o_ref[...] = x_ref[...] + gt_ref[0] * _rms(y, gn_ref[...])


def _merge_call(ya, o_fwd, o_rev, yc, u, og_col, gate_col0, hgrn_g, wb, wo, x, gn, gt, lv, tm_pref=256):
    t, d = x.shape
    bw = ya.shape[1]
    tm = _tile(lv, tm_pref)
    cw = _tile(d, 512, LANE)
    row = lambda i: (i, 0)
    branch = lambda: pl.BlockSpec((tm, bw), row)
    gate = lambda k: pl.BlockSpec((tm, d), lambda i: (i, gate_col0 + k))
    const2 = lambda i: (0, 0)
    return pl.pallas_call(
        functools.partial(_merge_kernel, cw=cw),
        grid=(t // tm,),
        in_specs=[
            branch(), branch(), branch(),
            pl.BlockSpec((tm, bw), lambda i: (i, og_col)),
            pl.BlockSpec((1, bw), const2),
            branch(),
            gate(0), gate(1), gate(2),
            pl.BlockSpec(wb.shape, lambda i: (0, 0, 0)),
            pl.BlockSpec(wo.shape, const2),
            pl.BlockSpec((tm, d), row),
            pl.BlockSpec((1, d), const2),
            pl.BlockSpec((1, 1, d), lambda i: (i * tm // lv, 0, 0)),
        ],
        out_specs=pl.BlockSpec((tm, d), row),
        out_shape=jax.ShapeDtypeStruct((t, d), F32),
        scratch_shapes=[pltpu.VMEM((tm, bw), MXU_DTYPE), pltpu.VMEM((tm, d), MXU_DTYPE)],
        compiler_params=_cparams(("parallel",)),
        name="merge",
    )(ya, o_fwd, o_rev, u, hgrn_g.reshape(1, bw), yc, u, u, u, wb, wo, x, gn, gt)


def _ffn_kernel(x_ref, g_ref, sh_ref, sc_ref, w1_ref, w3_ref, w2_ref, gn_ref, gt_ref, o_ref, h_ref, acc_ref):
    j = pl.program_id(1)

    @pl.when(j == 0)
    def _():
        h_ref[...] = _norm_mod(x_ref[...], g_ref[...], sh_ref[0], sc_ref[0]).astype(h_ref.dtype)
        acc_ref[...] = jnp.zeros_like(acc_ref)

    h = h_ref[...]
    a = _dot(h, w1_ref[...])
    g = _dot(h, w3_ref[...])
    m = (a * _sigmoid(a) * g).astype(MXU_DTYPE)
    acc_ref[...] += _dot(m, w2_ref[...])

    @pl.when(j == pl.num_programs(1) - 1)
    def _():
        o_ref[...] = x_ref[...] + gt_ref[0] * _rms(acc_ref[...], gn_ref[...])


def _ffn_call(x, g, sh, sc, w13, w2, gn, gt, lv, tm_pref=512, tf_pref=512):
    t, d = x.shape
    f = w2.shape[0]
    tm = _tile(lv, tm_pref)
    tf = _tile(f, tf_pref, LANE)
    nf = f // tf
    row = lambda i, j: (i, 0)
    const2 = lambda i, j: (0, 0)
    mod = lambda i, j: (i * tm // lv, 0, 0)
    return pl.pallas_call(
        _ffn_kernel,
        grid=(t // tm, nf),
        in_specs=[
            pl.BlockSpec((tm, d), row),
            pl.BlockSpec((1, d), const2),
            pl.BlockSpec((1, 1, d), mod),
            pl.BlockSpec((1, 1, d), mod),
            pl.BlockSpec((d, tf), lambda i, j: (0, j)),
            pl.BlockSpec((d, tf), lambda i, j: (0, nf + j)),
            pl.BlockSpec((tf, d), lambda i, j: (j, 0)),
            pl.BlockSpec((1, d), const2),
            pl.BlockSpec((1, 1, d), mod),
        ],
        out_specs=pl.BlockSpec((tm, d), row),
        out_shape=jax.ShapeDtypeStruct((t, d), F32),
        scratch_shapes=[pltpu.VMEM((tm, d), MXU_DTYPE), pltpu.VMEM((tm, d), F32)],
        compiler_params=_cparams(("parallel", "arbitrary")),
        name="ffn",
    )(x, g, sh, sc, w13, w13, w2, gn, gt)


ROUTE_ROWS = 8
ROW_DMA_UNROLL = 8


def _pack_rows(x):
    half = x.shape[1] // 2
    bits = pltpu.bitcast(x.astype(jnp.bfloat16).astype(F32), jnp.uint32)
    return (bits[:, :half] >> 16) | bits[:, half:]


def _unpack_rows(p):
    lo = pltpu.bitcast(p << 16, F32)
    hi = pltpu.bitcast(p & jnp.uint32(0xFFFF0000), F32)
    return lo, hi


def _route_kernel(x_ref, g_ref, sh_ref, sc_ref, rw_ref, tri_ref, route_ref, rt_ref, cnt_ref, xs_hbm,
                  h_ref, run_ref, idx_ref, cntv_ref, cnts_ref, zrow_ref, sem_idx, sem, *, n_experts, cap, tb, tm):
    step = pl.program_id(0)
    slot = lax.rem(step, 2)

    @pl.when(step == 0)
    def _():
        run_ref[...] = jnp.zeros_like(run_ref)

    h = _norm_mod(x_ref[...], g_ref[...], sh_ref[0], sc_ref[0])
    h_ref[slot] = _pack_rows(h)
    logits = jnp.dot(h, rw_ref[...], preferred_element_type=F32, precision=lax.Precision.HIGHEST)
    lane = lax.broadcasted_iota(jnp.int32, logits.shape, 1)
    neg_inf = -jnp.inf
    l1 = jnp.where(lane < n_experts, logits, neg_inf)
    m1 = jnp.max(l1, axis=-1, keepdims=True)
    i1 = jnp.min(jnp.where(l1 == m1, lane, LANE), axis=-1, keepdims=True)
    l2 = jnp.where(lane == i1, neg_inf, l1)
    m2 = jnp.max(l2, axis=-1, keepdims=True)
    i2 = jnp.min(jnp.where(l2 == m2, lane, LANE), axis=-1, keepdims=True)
    e = jnp.exp(m2 - m1)
    w0 = 1.0 / (1.0 + e)
    w1 = e * w0
    sel1 = lane == i1
    sel2 = lane == i2
    member = jnp.where(sel1, 1.0, jnp.where(sel2, 1.0, 0.0))
    rank = _dot(tri_ref[...], member.astype(MXU_DTYPE))
    base = run_ref[...] + rank + lane.astype(F32) * float(cap)
    pos0 = jnp.sum(jnp.where(sel1, base, 0.0), axis=-1, keepdims=True)
    pos1 = jnp.sum(jnp.where(sel2, base, 0.0), axis=-1, keepdims=True)
    route = jnp.where(lane == 0, pos0, jnp.where(lane == 1, pos1, jnp.where(lane == 2, w0, jnp.where(lane == 3, w1, 0.0))))
    route_ref[...] = route
    rt_ref[...] = route.T[0:ROUTE_ROWS, :].astype(jnp.int32)
    run_ref[...] += jnp.sum(member, axis=0, keepdims=True)
    cnt_ref[...] = run_ref[...]

    cp = pltpu.make_async_copy(rt_ref, idx_ref, sem_idx)
    cp.start()
    cp.wait()

    def row_copy(sl, k, dst_row):
        return pltpu.make_async_copy(h_ref.at[sl].at[pl.ds(k, 1)], xs_hbm.at[pl.ds(dst_row, 1)], sem.at[sl])

    def issue(k, carry):
        row_copy(slot, k, idx_ref[0, k]).start()
        row_copy(slot, k, idx_ref[1, k]).start()
        return carry

    lax.fori_loop(0, tb, issue, 0, unroll=ROW_DMA_UNROLL)

    def drain(sl):
        def body(k, carry):
            row_copy(sl, 0, 0).wait()
            row_copy(sl, 0, 0).wait()
            return carry

        lax.fori_loop(0, tb, body, 0, unroll=ROW_DMA_UNROLL)

    @pl.when(step > 0)
    def _():
        drain(1 - slot)

    @pl.when(step == pl.num_programs(0) - 1)
    def _():
        drain(slot)
        zrow_ref[...] = jnp.zeros_like(zrow_ref)
        cntv_ref[...] = jnp.broadcast_to(run_ref[...], cntv_ref.shape).astype(jnp.int32)
        cpc = pltpu.make_async_copy(cntv_ref, cnts_ref, sem_idx)
        cpc.start()
        cpc.wait()

        def zero_copy(dst_row):
            return pltpu.make_async_copy(zrow_ref.at[pl.ds(0, 1)], xs_hbm.at[pl.ds(dst_row, 1)], sem.at[0])

        for e in range(n_experts):
            cnt = cnts_ref[0, e]
            end = ((cnt + tm - 1) // tm) * tm

            def zissue(r, carry, e=e):
                zero_copy(e * cap + r).start()
                return carry

            def zdrain(r, carry):
                zero_copy(0).wait()
                return carry

            lax.fori_loop(cnt, end, zissue, 0)
            lax.fori_loop(cnt, end, zdrain, 0)


def _route_call(x, g, sh, sc, router_w, lv, cap, tm, tb_pref=256):
    t, d = x.shape
    n_experts = router_w.shape[1]
    tb = _tile(lv, tb_pref, LANE)
    rw = jnp.zeros((d, LANE), F32).at[:, :n_experts].set(router_w)
    idx = np.arange(tb)
    tri = jnp.asarray((idx[None, :] < idx[:, None]).astype(np.float32), MXU_DTYPE)
    row = lambda i: (i, 0)
    const2 = lambda i: (0, 0)
    mod = lambda i: (i * tb // lv, 0, 0)
    return pl.pallas_call(
        functools.partial(_route_kernel, n_experts=n_experts, cap=cap, tb=tb, tm=tm),
        grid=(t // tb,),
        in_specs=[
            pl.BlockSpec((tb, d), row),
            pl.BlockSpec((1, d), const2),
            pl.BlockSpec((1, 1, d), mod),
            pl.BlockSpec((1, 1, d), mod),
            pl.BlockSpec((d, LANE), const2),
            pl.BlockSpec((tb, tb), const2),
        ],
        out_specs=[
            pl.BlockSpec((tb, LANE), row),
            pl.BlockSpec((ROUTE_ROWS, tb), lambda i: (0, i)),
            pl.BlockSpec((1, LANE), const2),
            pl.BlockSpec(memory_space=pl.ANY),
        ],
        out_shape=[
            jax.ShapeDtypeStruct((t, LANE), F32),
            jax.ShapeDtypeStruct((ROUTE_ROWS, t), jnp.int32),
            jax.ShapeDtypeStruct((1, LANE), F32),
            jax.ShapeDtypeStruct((n_experts * cap, d // 2), jnp.uint32),
        ],
        scratch_shapes=[
            pltpu.VMEM((2, tb, d // 2), jnp.uint32),
            pltpu.VMEM((1, LANE), F32),
            pltpu.SMEM((ROUTE_ROWS, tb), jnp.int32),
            pltpu.VMEM((SUBLANE, LANE), jnp.int32),
            pltpu.SMEM((SUBLANE, LANE), jnp.int32),
            pltpu.VMEM((SUBLANE, d // 2), jnp.uint32),
            pltpu.SemaphoreType.DMA,
            pltpu.SemaphoreType.DMA((2,)),
        ],
        compiler_params=_cparams(("arbitrary",)),
        name="route",
    )(x, g, sh, sc, rw, tri)


def _gmm_kernel(be_ref, br_ref, bv_ref, xs_ref, w1_ref, w3_ref, w2_ref, y_ref, xb_ref, acc_ref):
    i = pl.program_id(0)
    j = pl.program_id(1)

    @pl.when(bv_ref[i] == 1)
    def _():
        @pl.when(j == 0)
        def _():
            half = xs_ref.shape[1]
            lo, hi = _unpack_rows(xs_ref[...])
            xb_ref[:, :half] = lo.astype(xb_ref.dtype)
            xb_ref[:, half:] = hi.astype(xb_ref.dtype)
            acc_ref[...] = jnp.zeros_like(acc_ref)

        xb = xb_ref[...]
        a = _dot(xb, w1_ref[0])
        g = _dot(xb, w3_ref[0])
        m = (a * _sigmoid(a) * g).astype(MXU_DTYPE)
        acc_ref[...] += _dot(m, w2_ref[0])

        @pl.when(j == pl.num_programs(1) - 1)
        def _():
            y_ref[...] = _pack_rows(acc_ref[...])


def _gmm_call(blk_e, blk_row, blk_valid, xs, w13, w2, tm, tf_pref=1024):
    n_rows, half = xs.shape
    d = 2 * half
    f = w2.shape[1]
    tf = _tile(f, tf_pref, LANE)
    nf = f // tf
    n_blk = blk_e.shape[0]

    def jsel(i, j, bv):
        return jnp.where(bv[i] == 1, j, nf - 1)

    grid_spec = pltpu.PrefetchScalarGridSpec(
        num_scalar_prefetch=3,
        grid=(n_blk, nf),
        in_specs=[
            pl.BlockSpec((tm, half), lambda i, j, be, br, bv: (br[i], 0)),
            pl.BlockSpec((1, d, tf), lambda i, j, be, br, bv: (be[i], 0, jsel(i, j, bv))),
            pl.BlockSpec((1, d, tf), lambda i, j, be, br, bv: (be[i], 0, nf + jsel(i, j, bv))),
            pl.BlockSpec((1, tf, d), lambda i, j, be, br, bv: (be[i], jsel(i, j, bv), 0)),
        ],
        out_specs=pl.BlockSpec((tm, half), lambda i, j, be, br, bv: (br[i], 0)),
        scratch_shapes=[pltpu.VMEM((tm, d), MXU_DTYPE), pltpu.VMEM((tm, d), F32)],
    )
    return pl.pallas_call(
        _gmm_kernel,
        grid_spec=grid_spec,
        out_shape=jax.ShapeDtypeStruct((n_rows, half), jnp.uint32),
        compiler_params=_cparams(("arbitrary", "arbitrary")),
        name="moe_gmm",
    )(blk_e, blk_row, blk_valid, xs, w13, w13, w2)


def _combine_kernel(rt_ref, rtn_ref, route_ref, x_ref, gn_ref, gt_ref, y_hbm, o_ref, idx_ref, y0_ref, y1_ref,
                    sem_idx, sem, *, tb):
    step = pl.program_id(0)
    slot = lax.rem(step, 2)

    def row_copy(src_row, dst_ref, sl, k, s):
        return pltpu.make_async_copy(y_hbm.at[pl.ds(src_row, 1)], dst_ref.at[sl].at[pl.ds(k, 1)], sem.at[sl, s])

    def gather(table_ref, sl):
        cp = pltpu.make_async_copy(table_ref, idx_ref, sem_idx)
        cp.start()
        cp.wait()

        def issue(k, carry):
            row_copy(idx_ref[0, k], y0_ref, sl, k, 0).start()
            row_copy(idx_ref[1, k], y1_ref, sl, k, 1).start()
            return carry

        lax.fori_loop(0, tb, issue, 0, unroll=ROW_DMA_UNROLL)

    @pl.when(step == 0)
    def _():
        gather(rt_ref, 0)

    @pl.when(step + 1 < pl.num_programs(0))
    def _():
        gather(rtn_ref, 1 - slot)

    def drain(k, carry):
        row_copy(0, y0_ref, slot, k, 0).wait()
        row_copy(0, y1_ref, slot, k, 1).wait()
        return carry

    lax.fori_loop(0, tb, drain, 0, unroll=ROW_DMA_UNROLL)
    route = route_ref[...]
    w0 = route[:, 2:3]
    w1 = route[:, 3:4]
    lo0, hi0 = _unpack_rows(y0_ref[slot])
    lo1, hi1 = _unpack_rows(y1_ref[slot])
    y = jnp.concatenate([w0 * lo0 + w1 * lo1, w0 * hi0 + w1 * hi1], axis=1)
    o_ref[...] = x_ref[...] + gt_ref[0] * _rms(y, gn_ref[...])


def _combine_call(rt, route, x, gn, gt, y, lv, tb_pref=256):
    t, d = x.shape
    tb = _tile(lv, tb_pref, LANE)
    nblk = t // tb
    row = lambda i: (i, 0)
    return pl.pallas_call(
        functools.partial(_combine_kernel, tb=tb),
        grid=(nblk,),
        in_specs=[
            pl.BlockSpec((ROUTE_ROWS, tb), lambda i: (0, i)),
            pl.BlockSpec((ROUTE_ROWS, tb), lambda i: (0, jnp.minimum(i + 1, nblk - 1))),
            pl.BlockSpec((tb, LANE), row),
            pl.BlockSpec((tb, d), row),
            pl.BlockSpec((1, d), lambda i: (0, 0)),
            pl.BlockSpec((1, 1, d), lambda i: (i * tb // lv, 0, 0)),
            pl.BlockSpec(memory_space=pl.ANY),
        ],
        out_specs=pl.BlockSpec((tb, d), row),
        out_shape=jax.ShapeDtypeStruct((t, d), F32),
        scratch_shapes=[
            pltpu.SMEM((ROUTE_ROWS, tb), jnp.int32),
            pltpu.VMEM((2, tb, d // 2), jnp.uint32),
            pltpu.VMEM((2, tb, d // 2), jnp.uint32),
            pltpu.SemaphoreType.DMA,
            pltpu.SemaphoreType.DMA((2, 2)),
        ],
        compiler_params=_cparams(("arbitrary",)),
        name="moe_combine",
    )(rt, rt, route, x, gn, gt, y)


def _moe(x, g, sh, sc, router_w, w13, w2, gn, gt, lv, tm_pref=512):
    t, d = x.shape
    n_experts = router_w.shape[1]
    tm = _tile(t, tm_pref, LANE)
    cap = -(-t // tm) * tm
    route, rt, cnt, xs = _route_call(x, g, sh, sc, router_w, lv, cap, tm)
    counts = cnt[0, :n_experts].astype(jnp.int32)
    nb = (counts + tm - 1) // tm
    ends = jnp.cumsum(nb)
    n_blk = TOP_K * t // tm + n_experts
    bi = jnp.arange(n_blk, dtype=jnp.int32)
    valid = bi < ends[-1]
    bi_c = jnp.minimum(bi, ends[-1] - 1)
    be = jnp.minimum(jnp.searchsorted(ends, bi_c, side="right"), n_experts - 1).astype(jnp.int32)
    br = be * (cap // tm) + (bi_c - (ends - nb)[be])
    y = _gmm_call(be, br.astype(jnp.int32), valid.astype(jnp.int32), xs, w13, w2, tm)
    return _combine_call(rt, route, x, gn, gt, y, lv)


def kernel(x_prompt, x_sample, c_prompt, c_sample, ada_w, ada_b, norm_g, w_in, na_rpb, hgrn_lb, hgrn_norm_g,
           conv_w, conv_b, conv_ln_g, conv_ln_b, w_branch, w_out, ffn_w13, ffn_w2, router_w, moe_w13, moe_w2):
    bp, lp, d = x_prompt.shape
    bs, ls, _ = x_sample.shape
    depth = ada_w.shape[0]
    bw = w_branch.shape[2]
    tp = bp * lp
    t = tp + bs * ls
    lv = math.gcd(lp, ls)
    n_vseq = t // lv
    assert 8 * bw % d == 0 and d % LANE == 0 and bw % LANE == 0

    x = jnp.concatenate([x_prompt.reshape(tp, d), x_sample.reshape(bs * ls, d)], axis=0)

    nb = bp + bs
    rows = -(-nb // SUBLANE) * SUBLANE
    c_all = jnp.zeros((rows, d), F32).at[:nb].set(jnp.concatenate([c_prompt, c_sample], axis=0))
    mod = _ada(c_all, ada_w, ada_b)
    starts = np.arange(n_vseq) * lv
    vb = np.where(starts < tp, starts // lp, bp + (starts - tp) // ls)
    mod_v = mod[:, vb, :].reshape(depth, n_vseq, 1, 6, d)
    part = lambda l, k: mod_v[l, :, :, k, :]

    sm = jax.nn.softmax(hgrn_lb.astype(F32), axis=0)
    lb_all = jnp.clip(jnp.cumsum(sm, axis=0) - sm[0], 0.0, 1.0)

    cb = lambda k: slice(k * bw, (k + 1) * bw)
    order = [0, 1, 2, 3, 4, 7, 8, 9]
    conv_tb = _tile(lv, 256)
    blk_start = np.arange(t // conv_tb) * conv_tb
    seq_pos = np.where(blk_start < tp, blk_start % lp, (blk_start - tp) % ls)
    seq_len_of = np.where(blk_start < tp, lp, ls)
    seq_starts = jnp.asarray((seq_pos == 0).astype(np.int32))
    seq_ends = jnp.asarray((seq_pos + conv_tb == seq_len_of).astype(np.int32))

    for l in range(depth):
        wl = w_in[l]
        w_all = jnp.concatenate([wl[:, cb(k)] for k in order] + [wl[:, 10 * bw:], wl[:, 5 * bw:7 * bw]],
                                axis=1).astype(MXU_DTYPE)
        u, uf = _nm_matmul(x, norm_g[l, 0].reshape(1, d), part(l, 0), part(l, 1), w_all, w_all.shape[1] - 2 * bw, lv)

        bias_tab = _na_bias_table(na_rpb[l])
        ya = _na_call(u, bias_tab, None, 0, bp, lp)
        ya = _na_call(u, bias_tab, ya, tp, bs, ls)

        o_fwd, o_rev = _hgrn_call(u, uf, lb_all[l], ((0, bp, lp), (tp, bs, ls)), (3, 4, 0, 1))

        yc = _conv_call(u, conv_w[l], conv_b[l], conv_ln_g[l], conv_ln_b[l], seq_starts, seq_ends, 6, 7, conv_tb)

        x = _merge_call(ya, o_fwd, o_rev, yc, u, 5, 8 * bw // d, hgrn_norm_g[l], w_branch[l].astype(MXU_DTYPE),
                        w_out[l].astype(MXU_DTYPE), x, norm_g[l, 1].reshape(1, d), part(l, 2), lv)

        g2 = norm_g[l, 2].reshape(1, d)
        g3 = norm_g[l, 3].reshape(1, d)
        if l % 2 == 0:
            x = _ffn_call(x, g2, part(l, 3), part(l, 4), ffn_w13[l // 2].astype(MXU_DTYPE),
                          ffn_w2[l // 2].astype(MXU_DTYPE), g3, part(l, 5), lv)
        else:
            x = _moe(x, g2, part(l, 3), part(l, 4), router_w[l // 2], moe_w13[l // 2].astype(MXU_DTYPE),
                     moe_w2[l // 2].astype(MXU_DTYPE), g3, part(l, 5), lv)

    return (x[:tp].reshape(bp, lp, d), x[tp:].reshape(bs, ls, d))
```

```python
import functools
import math

import numpy as np
import jax
import jax.numpy as jnp
from jax import lax
from jax.experimental import pallas as pl
from jax.experimental.pallas import tpu as pltpu

GRID_W = 64
NA_HEAD_DIM = 64
NA_ROWS = 8
NA_COLS = 16
HGRN_DK = 128
TOP_K = 2
EPS = 1e-6
F_MIN = 1e-30
NEG_BIG = -1e30

LANE = 128
SUBLANE = 8
VMEM_LIMIT_BYTES = 56 * 1024 * 1024

MXU_DTYPE = jnp.bfloat16
ACT_DTYPE = jnp.bfloat16

HGRN_CHUNK = 128
HGRN_MATMUL_LEVELS = 3
NA_ROW_UNROLL = 4
NA_QGROUP = 16
F32 = jnp.float32


def _cparams(sem, vmem=VMEM_LIMIT_BYTES):
    return pltpu.CompilerParams(dimension_semantics=sem, vmem_limit_bytes=vmem)


def _sigmoid(x):
    return 1.0 / (1.0 + jnp.exp(-x))


def _dot(a, b):
    return jnp.dot(a, b, preferred_element_type=F32)


def _dot_nt(a, b):
    return lax.dot_general(a, b, (((1,), (1,)), ((), ())), preferred_element_type=F32)


def _dot_tn(a, b):
    return lax.dot_general(a, b, (((0,), (0,)), ((), ())), preferred_element_type=F32)


def _rms(x, g):
    return x * lax.rsqrt(jnp.mean(x * x, axis=-1, keepdims=True) + EPS) * g


def _norm_mod(x, g, sh, sc):
    return _rms(x, g) * (1.0 + sc) + sh


def _tile(n, pref, mult=SUBLANE):
    if n <= pref:
        return n
    t = (pref // mult) * mult
    while t >= mult:
        if n % t == 0:
            return t
        t -= mult
    return n


def _ada_kernel(c_ref, w_ref, b_ref, o_ref):
    c = c_ref[...]
    cs = (c * _sigmoid(c)).astype(MXU_DTYPE)
    o_ref[0] = _dot(cs, w_ref[0].astype(MXU_DTYPE)) + b_ref[0]


def _ada(c_all, ada_w, ada_b):
    depth, d, n = ada_w.shape
    rows = c_all.shape[0]
    tn = _tile(n, 1024, LANE)
    return pl.pallas_call(
        _ada_kernel,
        grid=(depth, n // tn),
        in_specs=[
            pl.BlockSpec((rows, d), lambda l, j: (0, 0)),
            pl.BlockSpec((1, d, tn), lambda l, j: (l, 0, j)),
            pl.BlockSpec((1, 1, tn), lambda l, j: (l, 0, j)),
        ],
        out_specs=pl.BlockSpec((1, rows, tn), lambda l, j: (l, 0, j)),
        out_shape=jax.ShapeDtypeStruct((depth, rows, n), F32),
        compiler_params=_cparams(("parallel", "parallel")),
        name="ada",
    )(c_all, ada_w, ada_b.reshape(depth, 1, n))


def _row_stream_specs(xs, tm, rank):
    pad = (lambda f: (lambda i: (f(i), 0))) if rank == 1 else (lambda f: (lambda i, j: (f(i), 0)))
    if not isinstance(xs, tuple):
        return [pl.BlockSpec((tm, xs.shape[1]), pad(lambda i: i))], (xs,), None
    xa, xb = xs
    d = xa.shape[1]
    nba, nbb = xa.shape[0] // tm, xb.shape[0] // tm
    first = pad(lambda i: jnp.minimum(i, nba - 1))
    second = pad(lambda i: jnp.clip(i - nba, 0, nbb - 1))
    one = pl.Buffered(1)
    return ([pl.BlockSpec((tm, d), first, pipeline_mode=one), pl.BlockSpec((tm, d), second, pipeline_mode=one)],
            (xa, xb), nba)


def _row_stream_block(x_refs, nba):
    if nba is None:
        return x_refs[0][...]
    return jnp.where(pl.program_id(0) < nba, x_refs[0][...], x_refs[1][...])


def _row_out_specs(t, d, tm, split_rows, rank):
    pad = (lambda f: (lambda i: (f(i), 0))) if rank == 1 else (lambda f: (lambda i, j: (f(i), 0)))
    if split_rows is None:
        return pl.BlockSpec((tm, d), pad(lambda i: i)), jax.ShapeDtypeStruct((t, d), F32), None
    nba, nbb = split_rows // tm, (t - split_rows) // tm
    specs = [pl.BlockSpec((tm, d), pad(lambda i: jnp.minimum(i, nba - 1))),
             pl.BlockSpec((tm, d), pad(lambda i: jnp.clip(i - nba, 0, nbb - 1)))]
    shapes = [jax.ShapeDtypeStruct((split_rows, d), F32), jax.ShapeDtypeStruct((t - split_rows, d), F32)]
    return specs, shapes, nba


def _row_out_store(o_refs, nba, val):
    if nba is None:
        o_refs[0][...] = val
        return

    @pl.when(pl.program_id(0) < nba)
    def _():
        o_refs[0][...] = val

    @pl.when(pl.program_id(0) >= nba)
    def _():
        o_refs[1][...] = val


def _nm_matmul_kernel(*refs, nja, nba):
    *x_refs, g_ref, sh_ref, sc_ref, w_ref, oa_ref, of_ref, h_ref = refs
    j = pl.program_id(1)

    @pl.when(j == 0)
    def _():
        x = _row_stream_block(x_refs, nba)
        h_ref[...] = _norm_mod(x, g_ref[...], sh_ref[0], sc_ref[0]).astype(h_ref.dtype)

    r = _dot(h_ref[...], w_ref[...])

    @pl.when(j < nja)
    def _():
        oa_ref[...] = r.astype(oa_ref.dtype)

    @pl.when(j >= nja)
    def _():
        of_ref[...] = r


def _nm_matmul(x, g, sh, sc, w, n_a, lv, tm_pref=1024, tn_pref=1024):
    d, n = w.shape
    n_f = n - n_a
    tm = _tile(lv, tm_pref)
    tn = _tile(math.gcd(n_a, n_f), tn_pref, LANE)
    nja = n_a // tn
    x_specs, x_args, nba = _row_stream_specs(x, tm, 2)
    t = sum(a.shape[0] for a in x_args)
    return pl.pallas_call(
        functools.partial(_nm_matmul_kernel, nja=nja, nba=nba),
        grid=(t // tm, n // tn),
        in_specs=x_specs + [
            pl.BlockSpec((1, d), lambda i, j: (0, 0)),
            pl.BlockSpec((1, 1, d), lambda i, j: (i * tm // lv, 0, 0)),
            pl.BlockSpec((1, 1, d), lambda i, j: (i * tm // lv, 0, 0)),
            pl.BlockSpec((d, tn), lambda i, j: (0, j)),
        ],
        out_specs=[
            pl.BlockSpec((tm, tn), lambda i, j: (i, jnp.minimum(j, nja - 1))),
            pl.BlockSpec((tm, tn), lambda i, j: (i, jnp.maximum(j - nja, 0))),
        ],
        out_shape=[jax.ShapeDtypeStruct((t, n_a), ACT_DTYPE), jax.ShapeDtypeStruct((t, n_f), F32)],
        scratch_shapes=[pltpu.VMEM((tm, d), MXU_DTYPE)],
        compiler_params=_cparams(("parallel", "arbitrary")),
        name="nm_matmul",
    )(*x_args, g, sh, sc, w)


def _na_bias_table(rpb):
    cols = np.arange(GRID_W)
    col_start = np.clip(cols - NA_COLS // 2, 0, GRID_W - NA_COLS)
    in_win = (cols[None, :] >= col_start[:, None]) & (cols[None, :] < col_start[:, None] + NA_COLS)
    dc = cols[None, :] - cols[:, None] + (NA_COLS - 1)
    onehot = ((dc[None] == np.arange(2 * NA_COLS - 1)[:, None, None]) & in_win[None]).astype(np.float32)
    full = jnp.einsum("hrd,dqk->hrqk", rpb.astype(F32), jnp.asarray(onehot), precision=lax.Precision.HIGHEST)
    tabs = []
    for delta in range(NA_ROWS):
        rows = full[:, NA_ROWS - 1 - delta:2 * NA_ROWS - 1 - delta]
        tabs.append(rows.transpose(0, 2, 1, 3).reshape(rpb.shape[0], GRID_W, NA_ROWS * GRID_W))
    mask = jnp.asarray(np.tile(in_win, (1, NA_ROWS)))
    return jnp.where(mask[None, None], jnp.stack(tabs, axis=0), NEG_BIG)


def _na_kernel(q_ref, k_ref, v_ref, bias_ref, o_ref, s_ref, p_ref, *, rows):
    win = NA_ROWS * GRID_W
    lane = lax.broadcasted_iota(jnp.int32, (1, LANE), 1)
    head_masks = [lane < NA_HEAD_DIM, lane >= NA_HEAD_DIM]
    scale = NA_HEAD_DIM ** -0.5

    def row_start(r):
        return jnp.clip(r - NA_ROWS // 2, 0, rows - NA_ROWS)

    def scores(g, slot):
        for u in range(NA_ROW_UNROLL):
            r = g * NA_ROW_UNROLL + u
            q2 = q_ref[pl.ds(pl.multiple_of(r * GRID_W, GRID_W), GRID_W), :]
            q2 = q2 * jnp.asarray(scale, q2.dtype)
            kw = k_ref[pl.ds(pl.multiple_of(row_start(r) * GRID_W, GRID_W), win), :]
            qq = jnp.concatenate([jnp.where(head_masks[h], q2, jnp.zeros_like(q2)) for h in range(2)], axis=0)
            s_ref[slot, u] = _dot_nt(qq, kw)

    def finish(g, slot):
        rden = []
        for u in range(NA_ROW_UNROLL):
            r = g * NA_ROW_UNROLL + u
            delta = r - row_start(r)
            parts = []
            for qg in range(2 * GRID_W // NA_QGROUP):
                h, gq = divmod(qg * NA_QGROUP, GRID_W)
                rows_g = pl.ds(qg * NA_QGROUP, NA_QGROUP)
                s = s_ref[slot, u, rows_g, :] + bias_ref[delta, h, pl.ds(gq, NA_QGROUP), :]
                m = jnp.max(s, axis=-1, keepdims=True)
                e = jnp.exp(s - m)
                parts.append(1.0 / jnp.sum(e, axis=-1, keepdims=True))
                p_ref[slot, u, rows_g, :] = e.astype(p_ref.dtype)
            rden.append(jnp.concatenate(parts, axis=0))
        for u in range(NA_ROW_UNROLL):
            r = g * NA_ROW_UNROLL + u
            vw = v_ref[pl.ds(pl.multiple_of(row_start(r) * GRID_W, GRID_W), win), :]
            o2 = _dot(p_ref[slot, u], vw) * rden[u]
            out = jnp.where(head_masks[0], o2[:GRID_W], o2[GRID_W:])
            o_ref[pl.ds(pl.multiple_of(r * GRID_W, GRID_W), GRID_W), :] = out.astype(o_ref.dtype)

    ngroups = rows // NA_ROW_UNROLL
    scores(0, 0)

    def body(i, carry):
        g = 2 * i
        scores(g + 1, 1)
        finish(g, 0)
        scores(jnp.minimum(g + 2, ngroups - 1), 0)
        finish(g + 1, 1)
        return carry

    lax.fori_loop(0, ngroups // 2, body, 0)


def _na_call(u, bias_tab, y_prev, tok_off, nseq, seq_len):
    t = u.shape[0]
    bw = bias_tab.shape[1] * NA_HEAD_DIM
    npair = bw // LANE
    rows = seq_len // GRID_W
    assert rows >= NA_ROWS and rows % (2 * NA_ROW_UNROLL) == 0 and seq_len % GRID_W == 0 and tok_off % seq_len == 0
    s0 = tok_off // seq_len
    kern = functools.partial(_na_kernel, rows=rows)
    in_specs = [
        pl.BlockSpec((seq_len, LANE), lambda b, p: (s0 + b, p)),
        pl.BlockSpec((seq_len, LANE), lambda b, p: (s0 + b, npair + p)),
        pl.BlockSpec((seq_len, LANE), lambda b, p: (s0 + b, 2 * npair + p)),
        pl.BlockSpec((NA_ROWS, 2, GRID_W, NA_ROWS * GRID_W), lambda b, p: (0, p, 0, 0)),
    ]
    args = [u, u, u, bias_tab]
    aliases = {}
    if y_prev is not None:
        in_specs.append(pl.BlockSpec(memory_space=pl.ANY))
        args.append(y_prev)
        aliases = {4: 0}
        kern_fn = lambda q, k, v, b, _prev, o, s, p: kern(q, k, v, b, o, s, p)
    else:
        kern_fn = kern
    return pl.pallas_call(
        kern_fn,
        grid=(nseq, npair),
        in_specs=in_specs,
        out_specs=pl.BlockSpec((seq_len, LANE), lambda b, p: (s0 + b, p)),
        out_shape=jax.ShapeDtypeStruct((t, bw), ACT_DTYPE),
        scratch_shapes=[
            pltpu.VMEM((2, NA_ROW_UNROLL, 2 * GRID_W, NA_ROWS * GRID_W), F32),
            pltpu.VMEM((2, NA_ROW_UNROLL, 2 * GRID_W, NA_ROWS * GRID_W), MXU_DTYPE),
        ],
        input_output_aliases=aliases,
        compiler_params=_cparams(("parallel", "parallel")),
        name="na",
    )(*args)


def _hgrn_consts(c, reverse):
    nlev = int(math.log2(c))
    nmat = min(HGRN_MATMUL_LEVELS, nlev)
    idx = np.arange(c)
    tri = (idx[None, :] <= idx[:, None]).astype(np.float32)
    mats = [tri]
    for l in range(nmat):
        m = 1 << l
        ref = (idx // (2 * m)) * (2 * m) + m - 1
        mats.append(tri[ref])
    mall = np.concatenate(mats, axis=0)
    x = idx[:, None] ^ idx[None, :]
    lvl = np.where(x > 0, np.floor(np.log2(np.maximum(x, 1))), -1).astype(np.int32)
    lvl = np.where(idx[None, :] > idx[:, None], -2, lvl)
    if reverse:
        mall = mall.reshape(nmat + 1, c, c)[:, ::-1, ::-1].reshape((nmat + 1) * c, c)
        lvl = lvl[::-1, ::-1]
    return np.ascontiguousarray(np.tile(mall, (1, 3))), np.ascontiguousarray(lvl), nlev


def _hgrn_direction(q_ref, i_ref, f_ref, lb, mall, lvl, st_ref, d, *, c, nlev, nheads, reverse):
    fg = jnp.maximum(lb + (1.0 - lb) * _sigmoid(f_ref[...]), F_MIN)
    lg = jnp.log2(fg)
    kk = 1.0 - fg
    hi = lg.astype(MXU_DTYPE)
    r1 = lg - hi.astype(F32)
    mid = r1.astype(MXU_DTYPE)
    lo = (r1 - mid.astype(F32)).astype(MXU_DTYPE)
    ball = _dot(mall, jnp.concatenate([hi, mid, lo], axis=0))
    nmat = min(HGRN_MATMUL_LEVELS, nlev)
    b_all = ball[0:c]
    bref = [ball[(l + 1) * c:(l + 2) * c] for l in range(nmat)]
    for l in range(nmat, nlev):
        m = 1 << l
        blocks = []
        for g0 in range(0, c, 2 * m):
            r = g0 + m if reverse else g0 + m - 1
            blocks.append(jnp.broadcast_to(b_all[r:r + 1, :], (2 * m, b_all.shape[1])))
        bref.append(blocks[0] if len(blocks) == 1 else jnp.concatenate(blocks, axis=0))
    q = q_ref[...].astype(F32) * (HGRN_DK ** -0.5)
    v = i_ref[...].astype(MXU_DTYPE)
    tot_row = 0 if reverse else c - 1
    outs = []
    for h in range(nheads):
        sl = slice(h * HGRN_DK, (h + 1) * HGRN_DK)
        bh = b_all[:, sl]
        qh = q[:, sl]
        kh = kk[:, sl]
        vh = v[:, sl]
        qb = qh.astype(MXU_DTYPE)
        kb = kh.astype(MXU_DTYPE)
        a = jnp.where(lvl == -1, _dot_nt(qb, kb), 0.0)
        for l in range(nlev):
            e = jnp.exp2(-jnp.abs(bh - bref[l][:, sl])).astype(MXU_DTYPE)
            a = jnp.where(lvl == l, _dot_nt(qb * e, kb * e), a)
        o = _dot(a.astype(MXU_DTYPE), vh)
        st = st_ref[d * nheads + h]
        o = o + _dot_nt((qh * jnp.exp2(bh)).astype(MXU_DTYPE), st.astype(MXU_DTYPE))
        btot = bh[tot_row:tot_row + 1, :]
        kd = (kh * jnp.exp2(btot - bh)).astype(MXU_DTYPE)
        st_ref[d * nheads + h] = st * jnp.exp2(btot) + _dot_tn(vh, kd)
        outs.append(o)
    return outs


def _hgrn_kernel(fblk_ref, rblk_ref, first_ref, qf_ref, if_ref, ff_ref, qr_ref, ir_ref, fr_ref, lb_ref, mall_ref, lvl_ref,
                 of_ref, or_ref, st_ref, *, c, nlev, nheads):
    @pl.when(first_ref[pl.program_id(0)] == 1)
    def _():
        st_ref[...] = jnp.zeros_like(st_ref)

    dirs = ((qf_ref, if_ref, ff_ref, of_ref, False), (qr_ref, ir_ref, fr_ref, or_ref, True))
    for d, (q_ref, i_ref, f_ref, o_ref, reverse) in enumerate(dirs):
        outs = _hgrn_direction(q_ref, i_ref, f_ref, lb_ref[d:d + 1, :], mall_ref[d], lvl_ref[d], st_ref, d,
                               c=c, nlev=nlev, nheads=nheads, reverse=reverse)
        for h in range(nheads):
            o_ref[:, h * HGRN_DK:(h + 1) * HGRN_DK] = outs[h]


def _hgrn_call(u, uf, lb, seqs, cols):
    t = u.shape[0]
    bw = lb.shape[-1]
    nheads = bw // HGRN_DK
    c = min([HGRN_CHUNK] + [n for _, _, n in seqs])
    fblk, rblk, first = [], [], []
    for off, cnt, n in seqs:
        nck = n // c
        for b in range(cnt):
            base = (off + b * n) // c
            fblk += [base + k for k in range(nck)]
            rblk += [base + nck - 1 - k for k in range(nck)]
            first += [1] + [0] * (nck - 1)
    consts = [_hgrn_consts(c, rev) for rev in (False, True)]
    nlev = consts[0][2]
    mall = jnp.asarray(np.stack([cc[0] for cc in consts]), MXU_DTYPE)
    lvl = jnp.asarray(np.stack([cc[1] for cc in consts]))
    cq, ci, cff, cfb = cols
    fwd = lambda col: pl.BlockSpec((c, bw), lambda s, fb, rb, fr: (fb[s], col))
    rev = lambda col: pl.BlockSpec((c, bw), lambda s, fb, rb, fr: (rb[s], col))
    const2 = lambda s, fb, rb, fr: (0, 0)
    const3 = lambda s, fb, rb, fr: (0, 0, 0)
    grid_spec = pltpu.PrefetchScalarGridSpec(
        num_scalar_prefetch=3,
        grid=(len(fblk),),
        in_specs=[fwd(cq), fwd(ci), fwd(cff), rev(cq), rev(ci), rev(cfb),
                  pl.BlockSpec((2, bw), const2), pl.BlockSpec(mall.shape, const3), pl.BlockSpec(lvl.shape, const3)],
        out_specs=[fwd(0), rev(0)],
        scratch_shapes=[pltpu.VMEM((2 * nheads, HGRN_DK, HGRN_DK), F32)],
    )
    tab = lambda z: jnp.asarray(np.asarray(z, np.int32))
    return pl.pallas_call(
        functools.partial(_hgrn_kernel, c=c, nlev=nlev, nheads=nheads),
        grid_spec=grid_spec,
        out_shape=[jax.ShapeDtypeStruct((t, bw), F32), jax.ShapeDtypeStruct((t, bw), F32)],
        compiler_params=_cparams(("arbitrary",)),
        name="hgrn",
    )(tab(fblk), tab(rblk), tab(first), u, u, uf, u, u, uf, lb, mall, lvl)


CONV_HALO = 16
CONV_ROWS = 32


def _conv_kernel(first_ref, last_ref, a_ref, g_ref, ap_ref, gp_ref, an_ref, gn_ref,
                 w_ref, b_ref, lg_ref, lb_ref, o_ref, hbuf, *, tb, width):
    i = pl.program_id(0)
    pad = width // 2

    def glu(a, g):
        return a.astype(F32) * _sigmoid(g.astype(F32))

    hbuf[0, CONV_HALO:CONV_HALO + tb, :] = glu(a_ref[...], g_ref[...])
    hp = glu(ap_ref[...], gp_ref[...])
    hbuf[0, 0:CONV_HALO, :] = jnp.where(first_ref[i] == 1, 0.0, hp)
    hn = glu(an_ref[...], gn_ref[...])
    hbuf[0, CONV_HALO + tb:2 * CONV_HALO + tb, :] = jnp.where(last_ref[i] == 1, 0.0, hn)
    span = tb + 2 * CONV_HALO - SUBLANE
    for k in range(1, SUBLANE):
        hbuf[k, 0:span, :] = hbuf[0, k:k + span, :]
    w = w_ref[...]
    for r0 in range(0, tb, CONV_ROWS):
        acc = jnp.zeros((CONV_ROWS, w.shape[1]), F32)
        for j in range(width):
            s = CONV_HALO + r0 + j - pad
            k = s % SUBLANE
            acc = acc + w[j:j + 1, :] * hbuf[k, s - k:s - k + CONV_ROWS, :]
        h = acc + b_ref[...]
        mu = jnp.mean(h, axis=-1, keepdims=True)
        hc = h - mu
        var = jnp.mean(hc * hc, axis=-1, keepdims=True)
        y = hc * lax.rsqrt(var + EPS) * lg_ref[...] + lb_ref[...]
        o_ref[r0:r0 + CONV_ROWS, :] = (y * _sigmoid(y)).astype(o_ref.dtype)


def _conv_call(u, conv_w, conv_b, ln_g, ln_b, seq_starts, seq_ends, col_a, col_g, tb):
    t = u.shape[0]
    width, ch = conv_w.shape
    assert width // 2 < CONV_HALO and tb % CONV_ROWS == 0
    nblk = t // tb
    hb = tb // CONV_HALO
    nh = t // CONV_HALO
    kern = functools.partial(_conv_kernel, tb=tb, width=width)
    cur = lambda col: pl.BlockSpec((tb, ch), lambda i, f, l: (i, col))
    prev = lambda col: pl.BlockSpec((CONV_HALO, ch), lambda i, f, l: (jnp.maximum(i * hb - 1, 0), col))
    nxt = lambda col: pl.BlockSpec((CONV_HALO, ch), lambda i, f, l: (jnp.minimum((i + 1) * hb, nh - 1), col))
    vec = lambda: pl.BlockSpec((1, ch), lambda i, f, l: (0, 0))
    grid_spec = pltpu.PrefetchScalarGridSpec(
        num_scalar_prefetch=2,
        grid=(nblk,),
        in_specs=[cur(col_a), cur(col_g), prev(col_a), prev(col_g), nxt(col_a), nxt(col_g),
                  pl.BlockSpec((width, ch), lambda i, f, l: (0, 0)), vec(), vec(), vec()],
        out_specs=pl.BlockSpec((tb, ch), lambda i, f, l: (i, 0)),
        scratch_shapes=[pltpu.VMEM((SUBLANE, tb + 2 * CONV_HALO, ch), F32)],
    )
    return pl.pallas_call(
        kern,
        grid_spec=grid_spec,
        out_shape=jax.ShapeDtypeStruct((t, ch), ACT_DTYPE),
        compiler_params=_cparams(("parallel",)),
        name="conv",
    )(seq_starts, seq_ends, u, u, u, u, u, u, conv_w, conv_b.reshape(1, ch), ln_g.reshape(1, ch), ln_b.reshape(1, ch))


def _merge_kernel(ya_ref, of_ref, or_ref, og_ref, ng_ref, yc_ref, g0_ref, g1_ref, g2_ref, wb_ref, wo_ref, *rest,
                  cw, nba):
    *x_refs, gn_ref, gt_ref, o_ref, yb_ref, m_ref = rest
    d = o_ref.shape[1]
    o = of_ref[...] + or_ref[...]
    og = og_ref[...].astype(F32)
    gate = ng_ref[...] * (og * _sigmoid(og))
    for h in range(o.shape[1] // HGRN_DK):
        sl = slice(h * HGRN_DK, (h + 1) * HGRN_DK)
        oh = o[:, sl]
        oh = oh * lax.rsqrt(jnp.mean(oh * oh, axis=-1, keepdims=True) + EPS)
        yb_ref[:, sl] = (oh * gate[:, sl]).astype(yb_ref.dtype)
    ys = (ya_ref, yb_ref, yc_ref)
    gs = (g0_ref, g1_ref, g2_ref)
    for cb in range(d // cw):
        cs = slice(cb * cw, (cb + 1) * cw)
        acc = None
        for i in range(3):
            term = _sigmoid(gs[i][:, cs].astype(F32)) * _dot(ys[i][...], wb_ref[i, :, cs])
            acc = term if acc is None else acc + term
        m_ref[:, cs] = acc.astype(m_ref.dtype)
    y = _dot(m_ref[...], wo_ref[...])
    o_ref[...] = _row_stream_block(x_refs, nba) + gt_ref[0] * _rms(y, gn_ref[...])


def _merge_call(ya, o_fwd, o_rev, yc, u, og_col, gate_col0, hgrn_g, wb, wo, x, gn, gt, lv, tm_pref=256):
    t, bw = ya.shape
    d = wo.shape[1]
    tm = _tile(lv, tm_pref)
    cw = _tile(d, 512, LANE)
    x_specs, x_args, nba = _row_stream_specs(x, tm, 1)
    row = lambda i: (i, 0)
    branch = lambda: pl.BlockSpec((tm, bw), row)
    gate = lambda k: pl.BlockSpec((tm, d), lambda i: (i, gate_col0 + k))
    const2 = lambda i: (0, 0)
    return pl.pallas_call(
        functools.partial(_merge_kernel, cw=cw, nba=nba),
        grid=(t // tm,),
        in_specs=[
            branch(), branch(), branch(),
            pl.BlockSpec((tm, bw), lambda i: (i, og_col)),
            pl.BlockSpec((1, bw), const2),
            branch(),
            gate(0), gate(1), gate(2),
            pl.BlockSpec(wb.shape, lambda i: (0, 0, 0)),
            pl.BlockSpec(wo.shape, const2),
        ] + x_specs + [
            pl.BlockSpec((1, d), const2),
            pl.BlockSpec((1, 1, d), lambda i: (i * tm // lv, 0, 0)),
        ],
        out_specs=pl.BlockSpec((tm, d), row),
        out_shape=jax.ShapeDtypeStruct((t, d), F32),
        scratch_shapes=[pltpu.VMEM((tm, bw), MXU_DTYPE), pltpu.VMEM((tm, d), MXU_DTYPE)],
        compiler_params=_cparams(("parallel",)),
        name="merge",
    )(ya, o_fwd, o_rev, u, hgrn_g.reshape(1, bw), yc, u, u, u, wb, wo, *x_args, gn, gt)


def _ffn_kernel(x_ref, g_ref, sh_ref, sc_ref, w1_ref, w3_ref, w2_ref, gn_ref, gt_ref, *rest, nba):
    *o_refs, h_ref, acc_ref = rest
    j = pl.program_id(1)

    @pl.when(j == 0)
    def _():
        h_ref[...] = _norm_mod(x_ref[...], g_ref[...], sh_ref[0], sc_ref[0]).astype(h_ref.dtype)
        acc_ref[...] = jnp.zeros_like(acc_ref)

    h = h_ref[...]
    a = _dot(h, w1_ref[...])
    g = _dot(h, w3_ref[...])
    m = (a * _sigmoid(a) * g).astype(MXU_DTYPE)
    acc_ref[...] += _dot(m, w2_ref[...])

    @pl.when(j == pl.num_programs(1) - 1)
    def _():
        _row_out_store(o_refs, nba, x_ref[...] + gt_ref[0] * _rms(acc_ref[...], gn_ref[...]))


def _ffn_call(x, g, sh, sc, w13, w2, gn, gt, lv, split_rows, tm_pref=512, tf_pref=512):
    t, d = x.shape
    f = w2.shape[0]
    tm = _tile(lv, tm_pref)
    tf = _tile(f, tf_pref, LANE)
    nf = f // tf
    row = lambda i, j: (i, 0)
    const2 = lambda i, j: (0, 0)
    mod = lambda i, j: (i * tm // lv, 0, 0)
    out_specs, out_shape, nba = _row_out_specs(t, d, tm, split_rows, 2)
    return pl.pallas_call(
        functools.partial(_ffn_kernel, nba=nba),
        grid=(t // tm, nf),
        in_specs=[
            pl.BlockSpec((tm, d), row),
            pl.BlockSpec((1, d), const2),
            pl.BlockSpec((1, 1, d), mod),
            pl.BlockSpec((1, 1, d), mod),
            pl.BlockSpec((d, tf), lambda i, j: (0, j)),
            pl.BlockSpec((d, tf), lambda i, j: (0, nf + j)),
            pl.BlockSpec((tf, d), lambda i, j: (j, 0)),
            pl.BlockSpec((1, d), const2),
            pl.BlockSpec((1, 1, d), mod),
        ],
        out_specs=out_specs,
        out_shape=out_shape,
        scratch_shapes=[pltpu.VMEM((tm, d), MXU_DTYPE), pltpu.VMEM((tm, d), F32)],
        compiler_params=_cparams(("parallel", "arbitrary")),
        name="ffn",
    )(x, g, sh, sc, w13, w13, w2, gn, gt)


ROUTE_ROWS = 8
ROW_DMA_UNROLL = 8


def _pack_rows(x):
    half = x.shape[1] // 2
    bits = pltpu.bitcast(x.astype(jnp.bfloat16).astype(F32), jnp.uint32)
    return (bits[:, :half] >> 16) | bits[:, half:]


def _unpack_rows(p):
    lo = pltpu.bitcast(p << 16, F32)
    hi = pltpu.bitcast(p & jnp.uint32(0xFFFF0000), F32)
    return lo, hi


def _route_kernel(x_ref, g_ref, sh_ref, sc_ref, rw_ref, tri_ref, route_ref, rt_ref, cnt_ref, xs_hbm,
                  h_ref, run_ref, idx_ref, cntv_ref, cnts_ref, zrow_ref, sem_idx, sem, *, n_experts, cap, tb, tm):
    step = pl.program_id(0)
    slot = lax.rem(step, 2)

    @pl.when(step == 0)
    def _():
        run_ref[...] = jnp.zeros_like(run_ref)

    h = _norm_mod(x_ref[...], g_ref[...], sh_ref[0], sc_ref[0])
    h_ref[slot] = _pack_rows(h)
    logits = jnp.dot(h, rw_ref[...], preferred_element_type=F32, precision=lax.Precision.HIGHEST)
    lane = lax.broadcasted_iota(jnp.int32, logits.shape, 1)
    neg_inf = -jnp.inf
    l1 = jnp.where(lane < n_experts, logits, neg_inf)
    m1 = jnp.max(l1, axis=-1, keepdims=True)
    i1 = jnp.min(jnp.where(l1 == m1, lane, LANE), axis=-1, keepdims=True)
    l2 = jnp.where(lane == i1, neg_inf, l1)
    m2 = jnp.max(l2, axis=-1, keepdims=True)
    i2 = jnp.min(jnp.where(l2 == m2, lane, LANE), axis=-1, keepdims=True)
    e = jnp.exp(m2 - m1)
    w0 = 1.0 / (1.0 + e)
    w1 = e * w0
    sel1 = lane == i1
    sel2 = lane == i2
    member = jnp.where(sel1, 1.0, jnp.where(sel2, 1.0, 0.0))
    rank = _dot(tri_ref[...], member.astype(MXU_DTYPE))
    base = run_ref[...] + rank + lane.astype(F32) * float(cap)
    pos0 = jnp.sum(jnp.where(sel1, base, 0.0), axis=-1, keepdims=True)
    pos1 = jnp.sum(jnp.where(sel2, base, 0.0), axis=-1, keepdims=True)
    route = jnp.where(lane == 0, pos0, jnp.where(lane == 1, pos1, jnp.where(lane == 2, w0, jnp.where(lane == 3, w1, 0.0))))
    route_ref[...] = route
    rt_ref[...] = route.T[0:ROUTE_ROWS, :].astype(jnp.int32)
    run_ref[...] += jnp.sum(member, axis=0, keepdims=True)
    cnt_ref[...] = run_ref[...]

    cp = pltpu.make_async_copy(rt_ref, idx_ref, sem_idx)
    cp.start()
    cp.wait()

    def row_copy(sl, k, dst_row):
        return pltpu.make_async_copy(h_ref.at[sl].at[pl.ds(k, 1)], xs_hbm.at[pl.ds(dst_row, 1)], sem.at[sl])

    def issue(k, carry):
        row_copy(slot, k, idx_ref[0, k]).start()
        row_copy(slot, k, idx_ref[1, k]).start()
        return carry

    lax.fori_loop(0, tb, issue, 0, unroll=ROW_DMA_UNROLL)

    def drain(sl):
        def body(k, carry):
            row_copy(sl, 0, 0).wait()
            row_copy(sl, 0, 0).wait()
            return carry

        lax.fori_loop(0, tb, body, 0, unroll=ROW_DMA_UNROLL)

    @pl.when(step > 0)
    def _():
        drain(1 - slot)

    @pl.when(step == pl.num_programs(0) - 1)
    def _():
        drain(slot)
        zrow_ref[...] = jnp.zeros_like(zrow_ref)
        cntv_ref[...] = jnp.broadcast_to(run_ref[...], cntv_ref.shape).astype(jnp.int32)
        cpc = pltpu.make_async_copy(cntv_ref, cnts_ref, sem_idx)
        cpc.start()
        cpc.wait()

        def zero_copy(dst_row):
            return pltpu.make_async_copy(zrow_ref.at[pl.ds(0, 1)], xs_hbm.at[pl.ds(dst_row, 1)], sem.at[0])

        for e in range(n_experts):
            cnt = cnts_ref[0, e]
            end = ((cnt + tm - 1) // tm) * tm

            def zissue(r, carry, e=e):
                zero_copy(e * cap + r).start()
                return carry

            def zdrain(r, carry):
                zero_copy(0).wait()
                return carry

            lax.fori_loop(cnt, end, zissue, 0)
            lax.fori_loop(cnt, end, zdrain, 0)


def _route_call(x, g, sh, sc, router_w, lv, cap, tm, tb_pref=256):
    t, d = x.shape
    n_experts = router_w.shape[1]
    tb = _tile(lv, tb_pref, LANE)
    rw = jnp.zeros((d, LANE), F32).at[:, :n_experts].set(router_w)
    idx = np.arange(tb)
    tri = jnp.asarray((idx[None, :] < idx[:, None]).astype(np.float32), MXU_DTYPE)
    row = lambda i: (i, 0)
    const2 = lambda i: (0, 0)
    mod = lambda i: (i * tb // lv, 0, 0)
    return pl.pallas_call(
        functools.partial(_route_kernel, n_experts=n_experts, cap=cap, tb=tb, tm=tm),
        grid=(t // tb,),
        in_specs=[
            pl.BlockSpec((tb, d), row),
            pl.BlockSpec((1, d), const2),
            pl.BlockSpec((1, 1, d), mod),
            pl.BlockSpec((1, 1, d), mod),
            pl.BlockSpec((d, LANE), const2),
            pl.BlockSpec((tb, tb), const2),
        ],
        out_specs=[
            pl.BlockSpec((tb, LANE), row),
            pl.BlockSpec((ROUTE_ROWS, tb), lambda i: (0, i)),
            pl.BlockSpec((1, LANE), const2),
            pl.BlockSpec(memory_space=pl.ANY),
        ],
        out_shape=[
            jax.ShapeDtypeStruct((t, LANE), F32),
            jax.ShapeDtypeStruct((ROUTE_ROWS, t), jnp.int32),
            jax.ShapeDtypeStruct((1, LANE), F32),
            jax.ShapeDtypeStruct((n_experts * cap, d // 2), jnp.uint32),
        ],
        scratch_shapes=[
            pltpu.VMEM((2, tb, d // 2), jnp.uint32),
            pltpu.VMEM((1, LANE), F32),
            pltpu.SMEM((ROUTE_ROWS, tb), jnp.int32),
            pltpu.VMEM((SUBLANE, LANE), jnp.int32),
            pltpu.SMEM((SUBLANE, LANE), jnp.int32),
            pltpu.VMEM((SUBLANE, d // 2), jnp.uint32),
            pltpu.SemaphoreType.DMA,
            pltpu.SemaphoreType.DMA((2,)),
        ],
        compiler_params=_cparams(("arbitrary",)),
        name="route",
    )(x, g, sh, sc, rw, tri)


def _gmm_kernel(be_ref, br_ref, bv_ref, xs_ref, w1_ref, w3_ref, w2_ref, y_ref, xb_ref, acc_ref):
    i = pl.program_id(0)
    j = pl.program_id(1)

    @pl.when(bv_ref[i] == 1)
    def _():
        @pl.when(j == 0)
        def _():
            half = xs_ref.shape[1]
            lo, hi = _unpack_rows(xs_ref[...])
            xb_ref[:, :half] = lo.astype(xb_ref.dtype)
            xb_ref[:, half:] = hi.astype(xb_ref.dtype)
            acc_ref[...] = jnp.zeros_like(acc_ref)

        xb = xb_ref[...]
        a = _dot(xb, w1_ref[0])
        g = _dot(xb, w3_ref[0])
        m = (a * _sigmoid(a) * g).astype(MXU_DTYPE)
        acc_ref[...] += _dot(m, w2_ref[0])

        @pl.when(j == pl.num_programs(1) - 1)
        def _():
            y_ref[...] = _pack_rows(acc_ref[...])


def _gmm_call(blk_e, blk_row, blk_valid, xs, w13, w2, tm, tf_pref=1024):
    n_rows, half = xs.shape
    d = 2 * half
    f = w2.shape[1]
    tf = _tile(f, tf_pref, LANE)
    nf = f // tf
    n_blk = blk_e.shape[0]

    def jsel(i, j, bv):
        return jnp.where(bv[i] == 1, j, nf - 1)

    grid_spec = pltpu.PrefetchScalarGridSpec(
        num_scalar_prefetch=3,
        grid=(n_blk, nf),
        in_specs=[
            pl.BlockSpec((tm, half), lambda i, j, be, br, bv: (br[i], 0)),
            pl.BlockSpec((1, d, tf), lambda i, j, be, br, bv: (be[i], 0, jsel(i, j, bv))),
            pl.BlockSpec((1, d, tf), lambda i, j, be, br, bv: (be[i], 0, nf + jsel(i, j, bv))),
            pl.BlockSpec((1, tf, d), lambda i, j, be, br, bv: (be[i], jsel(i, j, bv), 0)),
        ],
        out_specs=pl.BlockSpec((tm, half), lambda i, j, be, br, bv: (br[i], 0)),
        scratch_shapes=[pltpu.VMEM((tm, d), MXU_DTYPE), pltpu.VMEM((tm, d), F32)],
    )
    return pl.pallas_call(
        _gmm_kernel,
        grid_spec=grid_spec,
        out_shape=jax.ShapeDtypeStruct((n_rows, half), jnp.uint32),
        compiler_params=_cparams(("arbitrary", "arbitrary")),
        name="moe_gmm",
    )(blk_e, blk_row, blk_valid, xs, w13, w13, w2)


def _combine_kernel(rt_ref, rtn_ref, route_ref, x_ref, gn_ref, gt_ref, y_hbm, *rest, tb, nba):
    *o_refs, idx_ref, y0_ref, y1_ref, sem_idx, sem = rest
    step = pl.program_id(0)
    slot = lax.rem(step, 2)

    def row_copy(src_row, dst_ref, sl, k, s):
        return pltpu.make_async_copy(y_hbm.at[pl.ds(src_row, 1)], dst_ref.at[sl].at[pl.ds(k, 1)], sem.at[sl, s])

    def gather(table_ref, sl):
        cp = pltpu.make_async_copy(table_ref, idx_ref, sem_idx)
        cp.start()
        cp.wait()

        def issue(k, carry):
            row_copy(idx_ref[0, k], y0_ref, sl, k, 0).start()
            row_copy(idx_ref[1, k], y1_ref, sl, k, 1).start()
            return carry

        lax.fori_loop(0, tb, issue, 0, unroll=ROW_DMA_UNROLL)

    @pl.when(step == 0)
    def _():
        gather(rt_ref, 0)

    @pl.when(step + 1 < pl.num_programs(0))
    def _():
        gather(rtn_ref, 1 - slot)

    def drain(k, carry):
        row_copy(0, y0_ref, slot, k, 0).wait()
        row_copy(0, y1_ref, slot, k, 1).wait()
        return carry

    lax.fori_loop(0, tb, drain, 0, unroll=ROW_DMA_UNROLL)
    route = route_ref[...]
    w0 = route[:, 2:3]
    w1 = route[:, 3:4]
    lo0, hi0 = _unpack_rows(y0_ref[slot])
    lo1, hi1 = _unpack_rows(y1_ref[slot])
    y = jnp.concatenate([w0 * lo0 + w1 * lo1, w0 * hi0 + w1 * hi1], axis=1)
    _row_out_store(o_refs, nba, x_ref[...] + gt_ref[0] * _rms(y, gn_ref[...]))


def _combine_call(rt, route, x, gn, gt, y, lv, split_rows, tb_pref=256):
    t, d = x.shape
    tb = _tile(lv, tb_pref, LANE)
    nblk = t // tb
    row = lambda i: (i, 0)
    out_specs, out_shape, nba = _row_out_specs(t, d, tb, split_rows, 1)
    return pl.pallas_call(
        functools.partial(_combine_kernel, tb=tb, nba=nba),
        grid=(nblk,),
        in_specs=[
            pl.BlockSpec((ROUTE_ROWS, tb), lambda i: (0, i)),
            pl.BlockSpec((ROUTE_ROWS, tb), lambda i: (0, jnp.minimum(i + 1, nblk - 1))),
            pl.BlockSpec((tb, LANE), row),
            pl.BlockSpec((tb, d), row),
            pl.BlockSpec((1, d), lambda i: (0, 0)),
            pl.BlockSpec((1, 1, d), lambda i: (i * tb // lv, 0, 0)),
            pl.BlockSpec(memory_space=pl.ANY),
        ],
        out_specs=out_specs,
        out_shape=out_shape,
        scratch_shapes=[
            pltpu.SMEM((ROUTE_ROWS, tb), jnp.int32),
            pltpu.VMEM((2, tb, d // 2), jnp.uint32),
            pltpu.VMEM((2, tb, d // 2), jnp.uint32),
            pltpu.SemaphoreType.DMA,
            pltpu.SemaphoreType.DMA((2, 2)),
        ],
        compiler_params=_cparams(("arbitrary",)),
        name="moe_combine",
    )(rt, rt, route, x, gn, gt, y)


def _moe(x, g, sh, sc, router_w, w13, w2, gn, gt, lv, split_rows=None, tm_pref=512):
    t, d = x.shape
    n_experts = router_w.shape[1]
    tm = _tile(t, tm_pref, LANE)
    cap = -(-t // tm) * tm
    route, rt, cnt, xs = _route_call(x, g, sh, sc, router_w, lv, cap, tm)
    counts = cnt[0, :n_experts].astype(jnp.int32)
    nb = (counts + tm - 1) // tm
    ends = jnp.cumsum(nb)
    n_blk = TOP_K * t // tm + n_experts
    bi = jnp.arange(n_blk, dtype=jnp.int32)
    valid = bi < ends[-1]
    bi_c = jnp.minimum(bi, ends[-1] - 1)
    be = jnp.minimum(jnp.searchsorted(ends, bi_c, side="right"), n_experts - 1).astype(jnp.int32)
    br = be * (cap // tm) + (bi_c - (ends - nb)[be])
    y = _gmm_call(be, br.astype(jnp.int32), valid.astype(jnp.int32), xs, w13, w2, tm)
    return _combine_call(rt, route, x, gn, gt, y, lv, split_rows)


def kernel(x_prompt, x_sample, c_prompt, c_sample, ada_w, ada_b, norm_g, w_in, na_rpb, hgrn_lb, hgrn_norm_g,
           conv_w, conv_b, conv_ln_g, conv_ln_b, w_branch, w_out, ffn_w13, ffn_w2, router_w, moe_w13, moe_w2):
    bp, lp, d = x_prompt.shape
    bs, ls, _ = x_sample.shape
    depth = ada_w.shape[0]
    bw = w_branch.shape[2]
    tp = bp * lp
    t = tp + bs * ls
    lv = math.gcd(lp, ls)
    n_vseq = t // lv
    assert 8 * bw % d == 0 and d % LANE == 0 and bw % LANE == 0

    x = (x_prompt.reshape(tp, d), x_sample.reshape(bs * ls, d))

    nb = bp + bs
    rows = -(-nb // SUBLANE) * SUBLANE
    c_all = jnp.zeros((rows, d), F32).at[:nb].set(jnp.concatenate([c_prompt, c_sample], axis=0))
    mod = _ada(c_all, ada_w, ada_b)
    starts = np.arange(n_vseq) * lv
    vb = np.where(starts < tp, starts // lp, bp + (starts - tp) // ls)
    mod_v = mod[:, vb, :].reshape(depth, n_vseq, 1, 6, d)
    part = lambda l, k: mod_v[l, :, :, k, :]

    sm = jax.nn.softmax(hgrn_lb.astype(F32), axis=0)
    lb_all = jnp.clip(jnp.cumsum(sm, axis=0) - sm[0], 0.0, 1.0)

    cb = lambda k: slice(k * bw, (k + 1) * bw)
    order = [0, 1, 2, 3, 4, 7, 8, 9]
    conv_tb = _tile(lv, 256)
    blk_start = np.arange(t // conv_tb) * conv_tb
    seq_pos = np.where(blk_start < tp, blk_start % lp, (blk_start - tp) % ls)
    seq_len_of = np.where(blk_start < tp, lp, ls)
    seq_starts = jnp.asarray((seq_pos == 0).astype(np.int32))
    seq_ends = jnp.asarray((seq_pos + conv_tb == seq_len_of).astype(np.int32))

    for l in range(depth):
        wl = w_in[l]
        w_all = jnp.concatenate([wl[:, cb(k)] for k in order] + [wl[:, 10 * bw:], wl[:, 5 * bw:7 * bw]],
                                axis=1).astype(MXU_DTYPE)
        u, uf = _nm_matmul(x, norm_g[l, 0].reshape(1, d), part(l, 0), part(l, 1), w_all, w_all.shape[1] - 2 * bw, lv)

        bias_tab = _na_bias_table(na_rpb[l])
        ya = _na_call(u, bias_tab, None, 0, bp, lp)
        ya = _na_call(u, bias_tab, ya, tp, bs, ls)

        o_fwd, o_rev = _hgrn_call(u, uf, lb_all[l], ((0, bp, lp), (tp, bs, ls)), (3, 4, 0, 1))

        yc = _conv_call(u, conv_w[l], conv_b[l], conv_ln_g[l], conv_ln_b[l], seq_starts, seq_ends, 6, 7, conv_tb)

        x = _merge_call(ya, o_fwd, o_rev, yc, u, 5, 8 * bw // d, hgrn_norm_g[l], w_branch[l].astype(MXU_DTYPE),
                        w_out[l].astype(MXU_DTYPE), x, norm_g[l, 1].reshape(1, d), part(l, 2), lv)

        g2 = norm_g[l, 2].reshape(1, d)
        g3 = norm_g[l, 3].reshape(1, d)
        split_rows = tp if l == depth - 1 else None
        if l % 2 == 0:
            x = _ffn_call(x, g2, part(l, 3), part(l, 4), ffn_w13[l // 2].astype(MXU_DTYPE),
                          ffn_w2[l // 2].astype(MXU_DTYPE), g3, part(l, 5), lv, split_rows)
        else:
            x = _moe(x, g2, part(l, 3), part(l, 4), router_w[l // 2], moe_w13[l // 2].astype(MXU_DTYPE),
                     moe_w2[l // 2].astype(MXU_DTYPE), g3, part(l, 5), lv, split_rows)

    y_prompt, y_sample = x
    return (y_prompt.reshape(bp, lp, d), y_sample.reshape(bs, ls, d))
```

```python
import functools
import math

import numpy as np
import jax
import jax.numpy as jnp
from jax import lax
from jax.experimental import pallas as pl
from jax.experimental.pallas import tpu as pltpu

GRID_W = 64
NA_HEAD_DIM = 64
NA_ROWS = 8
NA_COLS = 16
HGRN_DK = 128
TOP_K = 2
EPS = 1e-6
F_MIN = 1e-30
NEG_BIG = -1e30

LANE = 128
SUBLANE = 8
VMEM_LIMIT_BYTES = 56 * 1024 * 1024

MXU_DTYPE = jnp.bfloat16
ACT_DTYPE = jnp.bfloat16

HGRN_CHUNK = 128
HGRN_CHUNKS_PER_STEP = 2
HGRN_MATMUL_LEVELS = 3
NA_ROW_UNROLL = 4
NA_QGROUP = 16
F32 = jnp.float32


def _cparams(sem, vmem=VMEM_LIMIT_BYTES):
    return pltpu.CompilerParams(dimension_semantics=sem, vmem_limit_bytes=vmem)


def _sigmoid(x):
    return 1.0 / (1.0 + jnp.exp(-x))


def _dot(a, b):
    return jnp.dot(a, b, preferred_element_type=F32)


def _dot_nt(a, b):
    return lax.dot_general(a, b, (((1,), (1,)), ((), ())), preferred_element_type=F32)


def _dot_tn(a, b):
    return lax.dot_general(a, b, (((0,), (0,)), ((), ())), preferred_element_type=F32)


def _rms(x, g):
    return x * lax.rsqrt(jnp.mean(x * x, axis=-1, keepdims=True) + EPS) * g


def _norm_mod(x, g, sh, sc):
    return _rms(x, g) * (1.0 + sc) + sh


def _tile(n, pref, mult=SUBLANE):
    if n <= pref:
        return n
    t = (pref // mult) * mult
    while t >= mult:
        if n % t == 0:
            return t
        t -= mult
    return n


def _ada_kernel(c_ref, w_ref, b_ref, o_ref):
    c = c_ref[...]
    cs = (c * _sigmoid(c)).astype(MXU_DTYPE)
    o_ref[0] = _dot(cs, w_ref[0].astype(MXU_DTYPE)) + b_ref[0]


def _ada(c_all, ada_w, ada_b):
    depth, d, n = ada_w.shape
    rows = c_all.shape[0]
    tn = _tile(n, 1024, LANE)
    return pl.pallas_call(
        _ada_kernel,
        grid=(depth, n // tn),
        in_specs=[
            pl.BlockSpec((rows, d), lambda l, j: (0, 0)),
            pl.BlockSpec((1, d, tn), lambda l, j: (l, 0, j)),
            pl.BlockSpec((1, 1, tn), lambda l, j: (l, 0, j)),
        ],
        out_specs=pl.BlockSpec((1, rows, tn), lambda l, j: (l, 0, j)),
        out_shape=jax.ShapeDtypeStruct((depth, rows, n), F32),
        compiler_params=_cparams(("parallel", "parallel")),
        name="ada",
    )(c_all, ada_w, ada_b.reshape(depth, 1, n))


def _row_out_specs(t, d, tm, split_rows, rank):
    pad = (lambda f: (lambda i: (f(i), 0))) if rank == 1 else (lambda f: (lambda i, j: (f(i), 0)))
    if split_rows is None:
        return pl.BlockSpec((tm, d), pad(lambda i: i)), jax.ShapeDtypeStruct((t, d), F32), None
    nba, nbb = split_rows // tm, (t - split_rows) // tm
    specs = [pl.BlockSpec((tm, d), pad(lambda i: jnp.minimum(i, nba - 1))),
             pl.BlockSpec((tm, d), pad(lambda i: jnp.clip(i - nba, 0, nbb - 1)))]
    shapes = [jax.ShapeDtypeStruct((split_rows, d), F32), jax.ShapeDtypeStruct((t - split_rows, d), F32)]
    return specs, shapes, nba


def _row_out_store(o_refs, nba, val):
    if nba is None:
        o_refs[0][...] = val
        return

    @pl.when(pl.program_id(0) < nba)
    def _():
        o_refs[0][...] = val

    @pl.when(pl.program_id(0) >= nba)
    def _():
        o_refs[1][...] = val


def _nm_matmul_kernel(x_ref, g_ref, sh_ref, sc_ref, w_ref, oa_ref, of_ref, h_ref, *, nja):
    j = pl.program_id(1)

    @pl.when(j == 0)
    def _():
        h_ref[...] = _norm_mod(x_ref[...], g_ref[...], sh_ref[0], sc_ref[0]).astype(h_ref.dtype)

    r = _dot(h_ref[...], w_ref[...])

    @pl.when(j < nja)
    def _():
        oa_ref[...] = r.astype(oa_ref.dtype)

    @pl.when(j >= nja)
    def _():
        of_ref[...] = r


def _nm_matmul(x, g, sh, sc, w, n_a, lv, tm_pref=1024, tn_pref=1024):
    t, d = x.shape
    n = w.shape[1]
    n_f = n - n_a
    tm = _tile(lv, tm_pref)
    tn = _tile(math.gcd(n_a, n_f), tn_pref, LANE)
    nja = n_a // tn
    return pl.pallas_call(
        functools.partial(_nm_matmul_kernel, nja=nja),
        grid=(t // tm, n // tn),
        in_specs=[
            pl.BlockSpec((tm, d), lambda i, j: (i, 0)),
            pl.BlockSpec((1, d), lambda i, j: (0, 0)),
            pl.BlockSpec((1, 1, d), lambda i, j: (i * tm // lv, 0, 0)),
            pl.BlockSpec((1, 1, d), lambda i, j: (i * tm // lv, 0, 0)),
            pl.BlockSpec((d, tn), lambda i, j: (0, j)),
        ],
        out_specs=[
            pl.BlockSpec((tm, tn), lambda i, j: (i, jnp.minimum(j, nja - 1))),
            pl.BlockSpec((tm, tn), lambda i, j: (i, jnp.maximum(j - nja, 0))),
        ],
        out_shape=[jax.ShapeDtypeStruct((t, n_a), ACT_DTYPE), jax.ShapeDtypeStruct((t, n_f), F32)],
        scratch_shapes=[pltpu.VMEM((tm, d), MXU_DTYPE)],
        compiler_params=_cparams(("parallel", "arbitrary")),
        name="nm_matmul",
    )(x, g, sh, sc, w)


def _na_bias_table(rpb):
    cols = np.arange(GRID_W)
    col_start = np.clip(cols - NA_COLS // 2, 0, GRID_W - NA_COLS)
    in_win = (cols[None, :] >= col_start[:, None]) & (cols[None, :] < col_start[:, None] + NA_COLS)
    dc = cols[None, :] - cols[:, None] + (NA_COLS - 1)
    onehot = ((dc[None] == np.arange(2 * NA_COLS - 1)[:, None, None]) & in_win[None]).astype(np.float32)
    full = jnp.einsum("hrd,dqk->hrqk", rpb.astype(F32), jnp.asarray(onehot), precision=lax.Precision.HIGHEST)
    tabs = []
    for delta in range(NA_ROWS):
        rows = full[:, NA_ROWS - 1 - delta:2 * NA_ROWS - 1 - delta]
        tabs.append(rows.transpose(0, 2, 1, 3).reshape(rpb.shape[0], GRID_W, NA_ROWS * GRID_W))
    mask = jnp.asarray(np.tile(in_win, (1, NA_ROWS)))
    return jnp.where(mask[None, None], jnp.stack(tabs, axis=0), NEG_BIG)


def _na_kernel(q_ref, k_ref, v_ref, bias_ref, o_ref, s_ref, p_ref, *, rows):
    win = NA_ROWS * GRID_W
    lane = lax.broadcasted_iota(jnp.int32, (1, LANE), 1)
    head_masks = [lane < NA_HEAD_DIM, lane >= NA_HEAD_DIM]
    scale = NA_HEAD_DIM ** -0.5

    def row_start(r):
        return jnp.clip(r - NA_ROWS // 2, 0, rows - NA_ROWS)

    def scores(g, slot):
        for u in range(NA_ROW_UNROLL):
            r = g * NA_ROW_UNROLL + u
            q2 = q_ref[pl.ds(pl.multiple_of(r * GRID_W, GRID_W), GRID_W), :]
            q2 = q2 * jnp.asarray(scale, q2.dtype)
            kw = k_ref[pl.ds(pl.multiple_of(row_start(r) * GRID_W, GRID_W), win), :]
            qq = jnp.concatenate([jnp.where(head_masks[h], q2, jnp.zeros_like(q2)) for h in range(2)], axis=0)
            s_ref[slot, u] = _dot_nt(qq, kw)

    def finish(g, slot):
        rden = []
        for u in range(NA_ROW_UNROLL):
            r = g * NA_ROW_UNROLL + u
            delta = r - row_start(r)
            parts = []
            for qg in range(2 * GRID_W // NA_QGROUP):
                h, gq = divmod(qg * NA_QGROUP, GRID_W)
                rows_g = pl.ds(qg * NA_QGROUP, NA_QGROUP)
                s = s_ref[slot, u, rows_g, :] + bias_ref[delta, h, pl.ds(gq, NA_QGROUP), :]
                m = jnp.max(s, axis=-1, keepdims=True)
                e = jnp.exp(s - m)
                parts.append(1.0 / jnp.sum(e, axis=-1, keepdims=True))
                p_ref[slot, u, rows_g, :] = e.astype(p_ref.dtype)
            rden.append(jnp.concatenate(parts, axis=0))
        for u in range(NA_ROW_UNROLL):
            r = g * NA_ROW_UNROLL + u
            vw = v_ref[pl.ds(pl.multiple_of(row_start(r) * GRID_W, GRID_W), win), :]
            o2 = _dot(p_ref[slot, u], vw) * rden[u]
            out = jnp.where(head_masks[0], o2[:GRID_W], o2[GRID_W:])
            o_ref[pl.ds(pl.multiple_of(r * GRID_W, GRID_W), GRID_W), :] = out.astype(o_ref.dtype)

    ngroups = rows // NA_ROW_UNROLL
    scores(0, 0)

    def body(i, carry):
        g = 2 * i
        scores(g + 1, 1)
        finish(g, 0)
        scores(jnp.minimum(g + 2, ngroups - 1), 0)
        finish(g + 1, 1)
        return carry

    lax.fori_loop(0, ngroups // 2, body, 0)


def _na_call(u, bias_tab, y_prev, tok_off, nseq, seq_len):
    t = u.shape[0]
    bw = bias_tab.shape[1] * NA_HEAD_DIM
    npair = bw // LANE
    rows = seq_len // GRID_W
    assert rows >= NA_ROWS and rows % (2 * NA_ROW_UNROLL) == 0 and seq_len % GRID_W == 0 and tok_off % seq_len == 0
    s0 = tok_off // seq_len
    kern = functools.partial(_na_kernel, rows=rows)
    in_specs = [
        pl.BlockSpec((seq_len, LANE), lambda b, p: (s0 + b, p)),
        pl.BlockSpec((seq_len, LANE), lambda b, p: (s0 + b, npair + p)),
        pl.BlockSpec((seq_len, LANE), lambda b, p: (s0 + b, 2 * npair + p)),
        pl.BlockSpec((NA_ROWS, 2, GRID_W, NA_ROWS * GRID_W), lambda b, p: (0, p, 0, 0)),
    ]
    args = [u, u, u, bias_tab]
    aliases = {}
    if y_prev is not None:
        in_specs.append(pl.BlockSpec(memory_space=pl.ANY))
        args.append(y_prev)
        aliases = {4: 0}
        kern_fn = lambda q, k, v, b, _prev, o, s, p: kern(q, k, v, b, o, s, p)
    else:
        kern_fn = kern
    return pl.pallas_call(
        kern_fn,
        grid=(nseq, npair),
        in_specs=in_specs,
        out_specs=pl.BlockSpec((seq_len, LANE), lambda b, p: (s0 + b, p)),
        out_shape=jax.ShapeDtypeStruct((t, bw), ACT_DTYPE),
        scratch_shapes=[
            pltpu.VMEM((2, NA_ROW_UNROLL, 2 * GRID_W, NA_ROWS * GRID_W), F32),
            pltpu.VMEM((2, NA_ROW_UNROLL, 2 * GRID_W, NA_ROWS * GRID_W), MXU_DTYPE),
        ],
        input_output_aliases=aliases,
        compiler_params=_cparams(("parallel", "parallel")),
        name="na",
    )(*args)


def _hgrn_consts(c, reverse):
    nlev = int(math.log2(c))
    nmat = min(HGRN_MATMUL_LEVELS, nlev)
    idx = np.arange(c)
    tri = (idx[None, :] <= idx[:, None]).astype(np.float32)
    mats = [tri]
    for l in range(nmat):
        m = 1 << l
        ref = (idx // (2 * m)) * (2 * m) + m - 1
        mats.append(tri[ref])
    mall = np.concatenate(mats, axis=0)
    x = idx[:, None] ^ idx[None, :]
    lvl = np.where(x > 0, np.floor(np.log2(np.maximum(x, 1))), -1).astype(np.int32)
    lvl = np.where(idx[None, :] > idx[:, None], -2, lvl)
    if reverse:
        mall = mall.reshape(nmat + 1, c, c)[:, ::-1, ::-1].reshape((nmat + 1) * c, c)
        lvl = lvl[::-1, ::-1]
    return np.ascontiguousarray(np.tile(mall, (1, 3))), np.ascontiguousarray(lvl), nlev


def _hgrn_direction(q_ref, i_ref, f_ref, rows, lb, mall, lvl, st_ref, d, *, c, nlev, nheads, reverse):
    fg = jnp.maximum(lb + (1.0 - lb) * _sigmoid(f_ref[rows, :]), F_MIN)
    lg = jnp.log2(fg)
    kk = 1.0 - fg
    hi = lg.astype(MXU_DTYPE)
    r1 = lg - hi.astype(F32)
    mid = r1.astype(MXU_DTYPE)
    lo = (r1 - mid.astype(F32)).astype(MXU_DTYPE)
    ball = _dot(mall, jnp.concatenate([hi, mid, lo], axis=0))
    nmat = min(HGRN_MATMUL_LEVELS, nlev)
    b_all = ball[0:c]
    bref = [ball[(l + 1) * c:(l + 2) * c] for l in range(nmat)]
    for l in range(nmat, nlev):
        m = 1 << l
        blocks = []
        for g0 in range(0, c, 2 * m):
            r = g0 + m if reverse else g0 + m - 1
            blocks.append(jnp.broadcast_to(b_all[r:r + 1, :], (2 * m, b_all.shape[1])))
        bref.append(blocks[0] if len(blocks) == 1 else jnp.concatenate(blocks, axis=0))
    q = q_ref[rows, :].astype(F32) * (HGRN_DK ** -0.5)
    v = i_ref[rows, :].astype(MXU_DTYPE)
    tot_row = 0 if reverse else c - 1
    outs = []
    for h in range(nheads):
        sl = slice(h * HGRN_DK, (h + 1) * HGRN_DK)
        bh = b_all[:, sl]
        qh = q[:, sl]
        kh = kk[:, sl]
        vh = v[:, sl]
        qb = qh.astype(MXU_DTYPE)
        kb = kh.astype(MXU_DTYPE)
        a = jnp.where(lvl == -1, _dot_nt(qb, kb), 0.0)
        for l in range(nlev):
            e = jnp.exp2(-jnp.abs(bh - bref[l][:, sl])).astype(MXU_DTYPE)
            a = jnp.where(lvl == l, _dot_nt(qb * e, kb * e), a)
        o = _dot(a.astype(MXU_DTYPE), vh)
        st = st_ref[d * nheads + h]
        o = o + _dot_nt((qh * jnp.exp2(bh)).astype(MXU_DTYPE), st.astype(MXU_DTYPE))
        btot = bh[tot_row:tot_row + 1, :]
        kd = (kh * jnp.exp2(btot - bh)).astype(MXU_DTYPE)
        st_ref[d * nheads + h] = st * jnp.exp2(btot) + _dot_tn(vh, kd)
        outs.append(o)
    return outs


def _hgrn_kernel(fblk_ref, rblk_ref, first_ref, qf_ref, if_ref, ff_ref, qr_ref, ir_ref, fr_ref, lb_ref, mall_ref, lvl_ref,
                 of_ref, or_ref, st_ref, *, c, nlev, nheads):
    @pl.when(first_ref[pl.program_id(0)] == 1)
    def _():
        st_ref[...] = jnp.zeros_like(st_ref)

    dirs = ((qf_ref, if_ref, ff_ref, of_ref, False), (qr_ref, ir_ref, fr_ref, or_ref, True))
    nsub = qf_ref.shape[0] // c
    for k in range(nsub):
        for d, (q_ref, i_ref, f_ref, o_ref, reverse) in enumerate(dirs):
            rows = pl.ds((nsub - 1 - k if reverse else k) * c, c)
            outs = _hgrn_direction(q_ref, i_ref, f_ref, rows, lb_ref[d:d + 1, :], mall_ref[d], lvl_ref[d], st_ref, d,
                                   c=c, nlev=nlev, nheads=nheads, reverse=reverse)
            for h in range(nheads):
                o_ref[rows, h * HGRN_DK:(h + 1) * HGRN_DK] = outs[h]


def _hgrn_call(u, uf, lb, seqs, cols):
    t = u.shape[0]
    bw = lb.shape[-1]
    nheads = bw // HGRN_DK
    c = min([HGRN_CHUNK] + [n for _, _, n in seqs])
    blk = c * HGRN_CHUNKS_PER_STEP
    assert all(n % blk == 0 and off % blk == 0 for off, _, n in seqs)
    fblk, rblk, first = [], [], []
    for off, cnt, n in seqs:
        nck = n // blk
        for b in range(cnt):
            base = (off + b * n) // blk
            fblk += [base + k for k in range(nck)]
            rblk += [base + nck - 1 - k for k in range(nck)]
            first += [1] + [0] * (nck - 1)
    consts = [_hgrn_consts(c, rev) for rev in (False, True)]
    nlev = consts[0][2]
    mall = jnp.asarray(np.stack([cc[0] for cc in consts]), MXU_DTYPE)
    lvl = jnp.asarray(np.stack([cc[1] for cc in consts]))
    cq, ci, cff, cfb = cols
    fwd = lambda col: pl.BlockSpec((blk, bw), lambda s, fb, rb, fr: (fb[s], col))
    rev = lambda col: pl.BlockSpec((blk, bw), lambda s, fb, rb, fr: (rb[s], col))
    const2 = lambda s, fb, rb, fr: (0, 0)
    const3 = lambda s, fb, rb, fr: (0, 0, 0)
    grid_spec = pltpu.PrefetchScalarGridSpec(
        num_scalar_prefetch=3,
        grid=(len(fblk),),
        in_specs=[fwd(cq), fwd(ci), fwd(cff), rev(cq), rev(ci), rev(cfb),
                  pl.BlockSpec((2, bw), const2), pl.BlockSpec(mall.shape, const3), pl.BlockSpec(lvl.shape, const3)],
        out_specs=[fwd(0), rev(0)],
        scratch_shapes=[pltpu.VMEM((2 * nheads, HGRN_DK, HGRN_DK), F32)],
    )
    tab = lambda z: jnp.asarray(np.asarray(z, np.int32))
    return pl.pallas_call(
        functools.partial(_hgrn_kernel, c=c, nlev=nlev, nheads=nheads),
        grid_spec=grid_spec,
        out_shape=[jax.ShapeDtypeStruct((t, bw), F32), jax.ShapeDtypeStruct((t, bw), F32)],
        compiler_params=_cparams(("arbitrary",)),
        name="hgrn",
    )(tab(fblk), tab(rblk), tab(first), u, u, uf, u, u, uf, lb, mall, lvl)


CONV_HALO = 16
CONV_ROWS = 32


def _conv_kernel(first_ref, last_ref, a_ref, g_ref, ap_ref, gp_ref, an_ref, gn_ref,
                 w_ref, b_ref, lg_ref, lb_ref, o_ref, hbuf, *, tb, width):
    i = pl.program_id(0)
    pad = width // 2

    def glu(a, g):
        return a.astype(F32) * _sigmoid(g.astype(F32))

    hbuf[0, CONV_HALO:CONV_HALO + tb, :] = glu(a_ref[...], g_ref[...])
    hp = glu(ap_ref[...], gp_ref[...])
    hbuf[0, 0:CONV_HALO, :] = jnp.where(first_ref[i] == 1, 0.0, hp)
    hn = glu(an_ref[...], gn_ref[...])
    hbuf[0, CONV_HALO + tb:2 * CONV_HALO + tb, :] = jnp.where(last_ref[i] == 1, 0.0, hn)
    span = tb + 2 * CONV_HALO - SUBLANE
    for k in range(1, SUBLANE):
        hbuf[k, 0:span, :] = hbuf[0, k:k + span, :]
    w = w_ref[...]
    for r0 in range(0, tb, CONV_ROWS):
        acc = jnp.zeros((CONV_ROWS, w.shape[1]), F32)
        for j in range(width):
            s = CONV_HALO + r0 + j - pad
            k = s % SUBLANE
            acc = acc + w[j:j + 1, :] * hbuf[k, s - k:s - k + CONV_ROWS, :]
        h = acc + b_ref[...]
        mu = jnp.mean(h, axis=-1, keepdims=True)
        hc = h - mu
        var = jnp.mean(hc * hc, axis=-1, keepdims=True)
        y = hc * lax.rsqrt(var + EPS) * lg_ref[...] + lb_ref[...]
        o_ref[r0:r0 + CONV_ROWS, :] = (y * _sigmoid(y)).astype(o_ref.dtype)


def _conv_call(u, conv_w, conv_b, ln_g, ln_b, seq_starts, seq_ends, col_a, col_g, tb):
    t = u.shape[0]
    width, ch = conv_w.shape
    assert width // 2 < CONV_HALO and tb % CONV_ROWS == 0
    nblk = t // tb
    hb = tb // CONV_HALO
    nh = t // CONV_HALO
    kern = functools.partial(_conv_kernel, tb=tb, width=width)
    cur = lambda col: pl.BlockSpec((tb, ch), lambda i, f, l: (i, col))
    prev = lambda col: pl.BlockSpec((CONV_HALO, ch), lambda i, f, l: (jnp.maximum(i * hb - 1, 0), col))
    nxt = lambda col: pl.BlockSpec((CONV_HALO, ch), lambda i, f, l: (jnp.minimum((i + 1) * hb, nh - 1), col))
    vec = lambda: pl.BlockSpec((1, ch), lambda i, f, l: (0, 0))
    grid_spec = pltpu.PrefetchScalarGridSpec(
        num_scalar_prefetch=2,
        grid=(nblk,),
        in_specs=[cur(col_a), cur(col_g), prev(col_a), prev(col_g), nxt(col_a), nxt(col_g),
                  pl.BlockSpec((width, ch), lambda i, f, l: (0, 0)), vec(), vec(), vec()],
        out_specs=pl.BlockSpec((tb, ch), lambda i, f, l: (i, 0)),
        scratch_shapes=[pltpu.VMEM((SUBLANE, tb + 2 * CONV_HALO, ch), F32)],
    )
    return pl.pallas_call(
        kern,
        grid_spec=grid_spec,
        out_shape=jax.ShapeDtypeStruct((t, ch), ACT_DTYPE),
        compiler_params=_cparams(("parallel",)),
        name="conv",
    )(seq_starts, seq_ends, u, u, u, u, u, u, conv_w, conv_b.reshape(1, ch), ln_g.reshape(1, ch), ln_b.reshape(1, ch))


def _merge_kernel(ya_ref, of_ref, or_ref, og_ref, ng_ref, yc_ref, g0_ref, g1_ref, g2_ref, wb_ref, wo_ref, x_ref,
                  gn_ref, gt_ref, o_ref, yb_ref, m_ref, *, cw):
    d = o_ref.shape[1]
    o = of_ref[...] + or_ref[...]
    og = og_ref[...].astype(F32)
    gate = ng_ref[...] * (og * _sigmoid(og))
    for h in range(o.shape[1] // HGRN_DK):
        sl = slice(h * HGRN_DK, (h + 1) * HGRN_DK)
        oh = o[:, sl]
        oh = oh * lax.rsqrt(jnp.mean(oh * oh, axis=-1, keepdims=True) + EPS)
        yb_ref[:, sl] = (oh * gate[:, sl]).astype(yb_ref.dtype)
    ys = (ya_ref, yb_ref, yc_ref)
    gs = (g0_ref, g1_ref, g2_ref)
    for cb in range(d // cw):
        cs = slice(cb * cw, (cb + 1) * cw)
        acc = None
        for i in range(3):
            term = _sigmoid(gs[i][:, cs].astype(F32)) * _dot(ys[i][...], wb_ref[i, :, cs])
            acc = term if acc is None else acc + term
        m_ref[:, cs] = acc.astype(m_ref.dtype)
    y = _dot(m_ref[...], wo_ref[...])
    o_ref[...] = x_ref[...] + gt_ref[0] * _rms(y, gn_ref[...])


def _merge_call(ya, o_fwd, o_rev, yc, u, og_col, gate_col0, hgrn_g, wb, wo, x, gn, gt, lv, tm_pref=256):
    t, bw = ya.shape
    d = wo.shape[1]
    tm = _tile(lv, tm_pref)
    cw = _tile(d, 512, LANE)
    row = lambda i: (i, 0)
    branch = lambda: pl.BlockSpec((tm, bw), row)
    gate = lambda k: pl.BlockSpec((tm, d), lambda i: (i, gate_col0 + k))
    const2 = lambda i: (0, 0)
    return pl.pallas_call(
        functools.partial(_merge_kernel, cw=cw),
        grid=(t // tm,),
        in_specs=[
            branch(), branch(), branch(),
            pl.BlockSpec((tm, bw), lambda i: (i, og_col)),
            pl.BlockSpec((1, bw), const2),
            branch(),
            gate(0), gate(1), gate(2),
            pl.BlockSpec(wb.shape, lambda i: (0, 0, 0)),
            pl.BlockSpec(wo.shape, const2),
            pl.BlockSpec((tm, d), row),
            pl.BlockSpec((1, d), const2),
            pl.BlockSpec((1, 1, d), lambda i: (i * tm // lv, 0, 0)),
        ],
        out_specs=pl.BlockSpec((tm, d), row),
        out_shape=jax.ShapeDtypeStruct((t, d), F32),
        scratch_shapes=[pltpu.VMEM((tm, bw), MXU_DTYPE), pltpu.VMEM((tm, d), MXU_DTYPE)],
        compiler_params=_cparams(("parallel",)),
        name="merge",
    )(ya, o_fwd, o_rev, u, hgrn_g.reshape(1, bw), yc, u, u, u, wb, wo, x, gn, gt)


def _ffn_kernel(x_ref, g_ref, sh_ref, sc_ref, w1_ref, w3_ref, w2_ref, gn_ref, gt_ref, *rest, nba):
    *o_refs, h_ref, acc_ref = rest
    j = pl.program_id(1)

    @pl.when(j == 0)
    def _():
        h_ref[...] = _norm_mod(x_ref[...], g_ref[...], sh_ref[0], sc_ref[0]).astype(h_ref.dtype)
        acc_ref[...] = jnp.zeros_like(acc_ref)

    h = h_ref[...]
    a = _dot(h, w1_ref[...])
    g = _dot(h, w3_ref[...])
    m = (a * _sigmoid(a) * g).astype(MXU_DTYPE)
    acc_ref[...] += _dot(m, w2_ref[...])

    @pl.when(j == pl.num_programs(1) - 1)
    def _():
        _row_out_store(o_refs, nba, x_ref[...] + gt_ref[0] * _rms(acc_ref[...], gn_ref[...]))


def _ffn_call(x, g, sh, sc, w13, w2, gn, gt, lv, split_rows, tm_pref=512, tf_pref=512):
    t, d = x.shape
    f = w2.shape[0]
    tm = _tile(lv, tm_pref)
    tf = _tile(f, tf_pref, LANE)
    nf = f // tf
    row = lambda i, j: (i, 0)
    const2 = lambda i, j: (0, 0)
    mod = lambda i, j: (i * tm // lv, 0, 0)
    out_specs, out_shape, nba = _row_out_specs(t, d, tm, split_rows, 2)
    return pl.pallas_call(
        functools.partial(_ffn_kernel, nba=nba),
        grid=(t // tm, nf),
        in_specs=[
            pl.BlockSpec((tm, d), row),
            pl.BlockSpec((1, d), const2),
            pl.BlockSpec((1, 1, d), mod),
            pl.BlockSpec((1, 1, d), mod),
            pl.BlockSpec((d, tf), lambda i, j: (0, j)),
            pl.BlockSpec((d, tf), lambda i, j: (0, nf + j)),
            pl.BlockSpec((tf, d), lambda i, j: (j, 0)),
            pl.BlockSpec((1, d), const2),
            pl.BlockSpec((1, 1, d), mod),
        ],
        out_specs=out_specs,
        out_shape=out_shape,
        scratch_shapes=[pltpu.VMEM((tm, d), MXU_DTYPE), pltpu.VMEM((tm, d), F32)],
        compiler_params=_cparams(("parallel", "arbitrary")),
        name="ffn",
    )(x, g, sh, sc, w13, w13, w2, gn, gt)


ROUTE_ROWS = 8
ROW_DMA_UNROLL = 8


def _pack_rows(x):
    half = x.shape[1] // 2
    bits = pltpu.bitcast(x.astype(jnp.bfloat16).astype(F32), jnp.uint32)
    return (bits[:, :half] >> 16) | bits[:, half:]


def _unpack_rows(p):
    lo = pltpu.bitcast(p << 16, F32)
    hi = pltpu.bitcast(p & jnp.uint32(0xFFFF0000), F32)
    return lo, hi


def _route_kernel(x_ref, g_ref, sh_ref, sc_ref, rw_ref, tri_ref, route_ref, rt_ref, cnt_ref, xs_hbm,
                  h_ref, run_ref, idx_ref, cntv_ref, cnts_ref, zrow_ref, sem_idx, sem, *, n_experts, cap, tb, tm):
    step = pl.program_id(0)
    slot = lax.rem(step, 2)

    @pl.when(step == 0)
    def _():
        run_ref[...] = jnp.zeros_like(run_ref)

    h = _norm_mod(x_ref[...], g_ref[...], sh_ref[0], sc_ref[0])
    h_ref[slot] = _pack_rows(h)
    h_hi = h.astype(MXU_DTYPE)
    h_lo = (h - h_hi.astype(F32)).astype(MXU_DTYPE)
    logits = _dot(h_hi, rw_ref[0]) + _dot(h_hi, rw_ref[1]) + _dot(h_lo, rw_ref[0])
    lane = lax.broadcasted_iota(jnp.int32, logits.shape, 1)
    neg_inf = -jnp.inf
    l1 = jnp.where(lane < n_experts, logits, neg_inf)
    m1 = jnp.max(l1, axis=-1, keepdims=True)
    i1 = jnp.min(jnp.where(l1 == m1, lane, LANE), axis=-1, keepdims=True)
    l2 = jnp.where(lane == i1, neg_inf, l1)
    m2 = jnp.max(l2, axis=-1, keepdims=True)
    i2 = jnp.min(jnp.where(l2 == m2, lane, LANE), axis=-1, keepdims=True)
    e = jnp.exp(m2 - m1)
    w0 = 1.0 / (1.0 + e)
    w1 = e * w0
    sel1 = lane == i1
    sel2 = lane == i2
    member = jnp.where(sel1, 1.0, jnp.where(sel2, 1.0, 0.0))
    rank = _dot(tri_ref[...], member.astype(MXU_DTYPE))
    base = run_ref[...] + rank + lane.astype(F32) * float(cap)
    pos0 = jnp.sum(jnp.where(sel1, base, 0.0), axis=-1, keepdims=True)
    pos1 = jnp.sum(jnp.where(sel2, base, 0.0), axis=-1, keepdims=True)
    route = jnp.where(lane == 0, pos0, jnp.where(lane == 1, pos1, jnp.where(lane == 2, w0, jnp.where(lane == 3, w1, 0.0))))
    route_ref[...] = route
    rt_ref[...] = route.T[0:ROUTE_ROWS, :].astype(jnp.int32)
    run_ref[...] += jnp.sum(member, axis=0, keepdims=True)
    cnt_ref[...] = run_ref[...]

    cp = pltpu.make_async_copy(rt_ref, idx_ref, sem_idx)
    cp.start()
    cp.wait()

    def row_copy(sl, k, dst_row):
        return pltpu.make_async_copy(h_ref.at[sl].at[pl.ds(k, 1)], xs_hbm.at[pl.ds(dst_row, 1)], sem.at[sl])

    def issue(k, carry):
        row_copy(slot, k, idx_ref[0, k]).start()
        row_copy(slot, k, idx_ref[1, k]).start()
        return carry

    lax.fori_loop(0, tb, issue, 0, unroll=ROW_DMA_UNROLL)

    def drain(sl):
        def body(k, carry):
            row_copy(sl, 0, 0).wait()
            row_copy(sl, 0, 0).wait()
            return carry

        lax.fori_loop(0, tb, body, 0, unroll=ROW_DMA_UNROLL)

    @pl.when(step > 0)
    def _():
        drain(1 - slot)

    @pl.when(step == pl.num_programs(0) - 1)
    def _():
        drain(slot)
        zrow_ref[...] = jnp.zeros_like(zrow_ref)
        cntv_ref[...] = jnp.broadcast_to(run_ref[...], cntv_ref.shape).astype(jnp.int32)
        cpc = pltpu.make_async_copy(cntv_ref, cnts_ref, sem_idx)
        cpc.start()
        cpc.wait()

        def zero_copy(dst_row):
            return pltpu.make_async_copy(zrow_ref.at[pl.ds(0, 1)], xs_hbm.at[pl.ds(dst_row, 1)], sem.at[0])

        for e in range(n_experts):
            cnt = cnts_ref[0, e]
            end = ((cnt + tm - 1) // tm) * tm

            def zissue(r, carry, e=e):
                zero_copy(e * cap + r).start()
                return carry

            def zdrain(r, carry):
                zero_copy(0).wait()
                return carry

            lax.fori_loop(cnt, end, zissue, 0)
            lax.fori_loop(cnt, end, zdrain, 0)


def _route_call(x, g, sh, sc, router_w, lv, cap, tm, tb_pref=256):
    t, d = x.shape
    n_experts = router_w.shape[1]
    tb = _tile(lv, tb_pref, LANE)
    rw = jnp.zeros((d, LANE), F32).at[:, :n_experts].set(router_w)
    rw_hi = rw.astype(MXU_DTYPE)
    rw = jnp.stack([rw_hi, (rw - rw_hi.astype(F32)).astype(MXU_DTYPE)])
    idx = np.arange(tb)
    tri = jnp.asarray((idx[None, :] < idx[:, None]).astype(np.float32), MXU_DTYPE)
    row = lambda i: (i, 0)
    const2 = lambda i: (0, 0)
    mod = lambda i: (i * tb // lv, 0, 0)
    return pl.pallas_call(
        functools.partial(_route_kernel, n_experts=n_experts, cap=cap, tb=tb, tm=tm),
        grid=(t // tb,),
        in_specs=[
            pl.BlockSpec((tb, d), row),
            pl.BlockSpec((1, d), const2),
            pl.BlockSpec((1, 1, d), mod),
            pl.BlockSpec((1, 1, d), mod),
            pl.BlockSpec((2, d, LANE), lambda i: (0, 0, 0)),
            pl.BlockSpec((tb, tb), const2),
        ],
        out_specs=[
            pl.BlockSpec((tb, LANE), row),
            pl.BlockSpec((ROUTE_ROWS, tb), lambda i: (0, i)),
            pl.BlockSpec((1, LANE), const2),
            pl.BlockSpec(memory_space=pl.ANY),
        ],
        out_shape=[
            jax.ShapeDtypeStruct((t, LANE), F32),
            jax.ShapeDtypeStruct((ROUTE_ROWS, t), jnp.int32),
            jax.ShapeDtypeStruct((1, LANE), F32),
            jax.ShapeDtypeStruct((n_experts * cap, d // 2), jnp.uint32),
        ],
        scratch_shapes=[
            pltpu.VMEM((2, tb, d // 2), jnp.uint32),
            pltpu.VMEM((1, LANE), F32),
            pltpu.SMEM((ROUTE_ROWS, tb), jnp.int32),
            pltpu.VMEM((SUBLANE, LANE), jnp.int32),
            pltpu.SMEM((SUBLANE, LANE), jnp.int32),
            pltpu.VMEM((SUBLANE, d // 2), jnp.uint32),
            pltpu.SemaphoreType.DMA,
            pltpu.SemaphoreType.DMA((2,)),
        ],
        compiler_params=_cparams(("arbitrary",)),
        name="route",
    )(x, g, sh, sc, rw, tri)


def _gmm_kernel(be_ref, br_ref, bv_ref, xs_ref, w1_ref, w3_ref, w2_ref, y_ref, xb_ref, acc_ref):
    i = pl.program_id(0)
    j = pl.program_id(1)

    @pl.when(bv_ref[i] == 1)
    def _():
        @pl.when(j == 0)
        def _():
            half = xs_ref.shape[1]
            lo, hi = _unpack_rows(xs_ref[...])
            xb_ref[:, :half] = lo.astype(xb_ref.dtype)
            xb_ref[:, half:] = hi.astype(xb_ref.dtype)
            acc_ref[...] = jnp.zeros_like(acc_ref)

        xb = xb_ref[...]
        a = _dot(xb, w1_ref[0])
        g = _dot(xb, w3_ref[0])
        m = (a * _sigmoid(a) * g).astype(MXU_DTYPE)
        acc_ref[...] += _dot(m, w2_ref[0])

        @pl.when(j == pl.num_programs(1) - 1)
        def _():
            y_ref[...] = _pack_rows(acc_ref[...])


def _gmm_call(blk_e, blk_row, blk_valid, xs, w13, w2, tm, tf_pref=1024):
    n_rows, half = xs.shape
    d = 2 * half
    f = w2.shape[1]
    tf = _tile(f, tf_pref, LANE)
    nf = f // tf
    n_blk = blk_e.shape[0]

    def jsel(i, j, bv):
        return jnp.where(bv[i] == 1, j, nf - 1)

    grid_spec = pltpu.PrefetchScalarGridSpec(
        num_scalar_prefetch=3,
        grid=(n_blk, nf),
        in_specs=[
            pl.BlockSpec((tm, half), lambda i, j, be, br, bv: (br[i], 0)),
            pl.BlockSpec((1, d, tf), lambda i, j, be, br, bv: (be[i], 0, jsel(i, j, bv))),
            pl.BlockSpec((1, d, tf), lambda i, j, be, br, bv: (be[i], 0, nf + jsel(i, j, bv))),
            pl.BlockSpec((1, tf, d), lambda i, j, be, br, bv: (be[i], jsel(i, j, bv), 0)),
        ],
        out_specs=pl.BlockSpec((tm, half), lambda i, j, be, br, bv: (br[i], 0)),
        scratch_shapes=[pltpu.VMEM((tm, d), MXU_DTYPE), pltpu.VMEM((tm, d), F32)],
    )
    return pl.pallas_call(
        _gmm_kernel,
        grid_spec=grid_spec,
        out_shape=jax.ShapeDtypeStruct((n_rows, half), jnp.uint32),
        compiler_params=_cparams(("arbitrary", "arbitrary")),
        name="moe_gmm",
    )(blk_e, blk_row, blk_valid, xs, w13, w13, w2)


def _combine_kernel(rt_ref, rtn_ref, route_ref, x_ref, gn_ref, gt_ref, y_hbm, *rest, tb, nba):
    *o_refs, idx_ref, y0_ref, y1_ref, sem_idx, sem = rest
    step = pl.program_id(0)
    slot = lax.rem(step, 2)

    def row_copy(src_row, dst_ref, sl, k, s):
        return pltpu.make_async_copy(y_hbm.at[pl.ds(src_row, 1)], dst_ref.at[sl].at[pl.ds(k, 1)], sem.at[sl, s])

    def gather(table_ref, sl):
        cp = pltpu.make_async_copy(table_ref, idx_ref, sem_idx)
        cp.start()
        cp.wait()

        def issue(k, carry):
            row_copy(idx_ref[0, k], y0_ref, sl, k, 0).start()
            row_copy(idx_ref[1, k], y1_ref, sl, k, 1).start()
            return carry

        lax.fori_loop(0, tb, issue, 0, unroll=ROW_DMA_UNROLL)

    @pl.when(step == 0)
    def _():
        gather(rt_ref, 0)

    @pl.when(step + 1 < pl.num_programs(0))
    def _():
        gather(rtn_ref, 1 - slot)

    def drain(k, carry):
        row_copy(0, y0_ref, slot, k, 0).wait()
        row_copy(0, y1_ref, slot, k, 1).wait()
        return carry

    lax.fori_loop(0, tb, drain, 0, unroll=ROW_DMA_UNROLL)
    route = route_ref[...]
    w0 = route[:, 2:3]
    w1 = route[:, 3:4]
    lo0, hi0 = _unpack_rows(y0_ref[slot])
    lo1, hi1 = _unpack_rows(y1_ref[slot])
    y = jnp.concatenate([w0 * lo0 + w1 * lo1, w0 * hi0 + w1 * hi1], axis=1)
    _row_out_store(o_refs, nba, x_ref[...] + gt_ref[0] * _rms(y, gn_ref[...]))


def _combine_call(rt, route, x, gn, gt, y, lv, split_rows, tb_pref=256):
    t, d = x.shape
    tb = _tile(lv, tb_pref, LANE)
    nblk = t // tb
    row = lambda i: (i, 0)
    out_specs, out_shape, nba = _row_out_specs(t, d, tb, split_rows, 1)
    return pl.pallas_call(
        functools.partial(_combine_kernel, tb=tb, nba=nba),
        grid=(nblk,),
        in_specs=[
            pl.BlockSpec((ROUTE_ROWS, tb), lambda i: (0, i)),
            pl.BlockSpec((ROUTE_ROWS, tb), lambda i: (0, jnp.minimum(i + 1, nblk - 1))),
            pl.BlockSpec((tb, LANE), row),
            pl.BlockSpec((tb, d), row),
            pl.BlockSpec((1, d), lambda i: (0, 0)),
            pl.BlockSpec((1, 1, d), lambda i: (i * tb // lv, 0, 0)),
            pl.BlockSpec(memory_space=pl.ANY),
        ],
        out_specs=out_specs,
        out_shape=out_shape,
        scratch_shapes=[
            pltpu.SMEM((ROUTE_ROWS, tb), jnp.int32),
            pltpu.VMEM((2, tb, d // 2), jnp.uint32),
            pltpu.VMEM((2, tb, d // 2), jnp.uint32),
            pltpu.SemaphoreType.DMA,
            pltpu.SemaphoreType.DMA((2, 2)),
        ],
        compiler_params=_cparams(("arbitrary",)),
        name="moe_combine",
    )(rt, rt, route, x, gn, gt, y)


def _moe(x, g, sh, sc, router_w, w13, w2, gn, gt, lv, split_rows=None, tm_pref=512):
    t, d = x.shape
    n_experts = router_w.shape[1]
    tm = _tile(t, tm_pref, LANE)
    cap = -(-t // tm) * tm
    route, rt, cnt, xs = _route_call(x, g, sh, sc, router_w, lv, cap, tm)
    counts = cnt[0, :n_experts].astype(jnp.int32)
    nb = (counts + tm - 1) // tm
    ends = jnp.cumsum(nb)
    n_blk = TOP_K * t // tm + n_experts
    bi = jnp.arange(n_blk, dtype=jnp.int32)
    valid = bi < ends[-1]
    bi_c = jnp.minimum(bi, ends[-1] - 1)
    be = jnp.minimum(jnp.searchsorted(ends, bi_c, side="right"), n_experts - 1).astype(jnp.int32)
    br = be * (cap // tm) + (bi_c - (ends - nb)[be])
    y = _gmm_call(be, br.astype(jnp.int32), valid.astype(jnp.int32), xs, w13, w2, tm)
    return _combine_call(rt, route, x, gn, gt, y, lv, split_rows)


def kernel(x_prompt, x_sample, c_prompt, c_sample, ada_w, ada_b, norm_g, w_in, na_rpb, hgrn_lb, hgrn_norm_g,
           conv_w, conv_b, conv_ln_g, conv_ln_b, w_branch, w_out, ffn_w13, ffn_w2, router_w, moe_w13, moe_w2):
    bp, lp, d = x_prompt.shape
    bs, ls, _ = x_sample.shape
    depth = ada_w.shape[0]
    bw = w_branch.shape[2]
    tp = bp * lp
    t = tp + bs * ls
    lv = math.gcd(lp, ls)
    n_vseq = t // lv
    assert 8 * bw % d == 0 and d % LANE == 0 and bw % LANE == 0

    x = jnp.concatenate([x_prompt.reshape(tp, d), x_sample.reshape(bs * ls, d)], axis=0)

    nb = bp + bs
    rows = -(-nb // SUBLANE) * SUBLANE
    c_all = jnp.zeros((rows, d), F32).at[:nb].set(jnp.concatenate([c_prompt, c_sample], axis=0))
    mod = _ada(c_all, ada_w, ada_b)
    starts = np.arange(n_vseq) * lv
    vb = np.where(starts < tp, starts // lp, bp + (starts - tp) // ls)
    mod_v = mod[:, vb, :].reshape(depth, n_vseq, 1, 6, d)
    part = lambda l, k: mod_v[l, :, :, k, :]

    sm = jax.nn.softmax(hgrn_lb.astype(F32), axis=0)
    lb_all = jnp.clip(jnp.cumsum(sm, axis=0) - sm[0], 0.0, 1.0)

    cb = lambda k: slice(k * bw, (k + 1) * bw)
    order = [0, 1, 2, 3, 4, 7, 8, 9]
    conv_tb = _tile(lv, 256)
    blk_start = np.arange(t // conv_tb) * conv_tb
    seq_pos = np.where(blk_start < tp, blk_start % lp, (blk_start - tp) % ls)
    seq_len_of = np.where(blk_start < tp, lp, ls)
    seq_starts = jnp.asarray((seq_pos == 0).astype(np.int32))
    seq_ends = jnp.asarray((seq_pos + conv_tb == seq_len_of).astype(np.int32))

    for l in range(depth):
        wl = w_in[l]
        w_all = jnp.concatenate([wl[:, cb(k)] for k in order] + [wl[:, 10 * bw:], wl[:, 5 * bw:7 * bw]],
                                axis=1).astype(MXU_DTYPE)
        u, uf = _nm_matmul(x, norm_g[l, 0].reshape(1, d), part(l, 0), part(l, 1), w_all, w_all.shape[1] - 2 * bw, lv)

        bias_tab = _na_bias_table(na_rpb[l])
        ya = _na_call(u, bias_tab, None, 0, bp, lp)
        ya = _na_call(u, bias_tab, ya, tp, bs, ls)

        o_fwd, o_rev = _hgrn_call(u, uf, lb_all[l], ((0, bp, lp), (tp, bs, ls)), (3, 4, 0, 1))

        yc = _conv_call(u, conv_w[l], conv_b[l], conv_ln_g[l], conv_ln_b[l], seq_starts, seq_ends, 6, 7, conv_tb)

        x = _merge_call(ya, o_fwd, o_rev, yc, u, 5, 8 * bw // d, hgrn_norm_g[l], w_branch[l].astype(MXU_DTYPE),
                        w_out[l].astype(MXU_DTYPE), x, norm_g[l, 1].reshape(1, d), part(l, 2), lv)

        g2 = norm_g[l, 2].reshape(1, d)
        g3 = norm_g[l, 3].reshape(1, d)
        split_rows = tp if l == depth - 1 else None
        if l % 2 == 0:
            x = _ffn_call(x, g2, part(l, 3), part(l, 4), ffn_w13[l // 2].astype(MXU_DTYPE),
                          ffn_w2[l // 2].astype(MXU_DTYPE), g3, part(l, 5), lv, split_rows)
        else:
            x = _moe(x, g2, part(l, 3), part(l, 4), router_w[l // 2], moe_w13[l // 2].astype(MXU_DTYPE),
                     moe_w2[l // 2].astype(MXU_DTYPE), g3, part(l, 5), lv, split_rows)

    y_prompt, y_sample = x
    return (y_prompt.reshape(bp, lp, d), y_sample.reshape(bs, ls, d))
```

```python
import functools
import math

import numpy as np
import jax
import jax.numpy as jnp
from jax import lax
from jax.experimental import pallas as pl
from jax.experimental.pallas import tpu as pltpu

GRID_W = 64
NA_HEAD_DIM = 64
NA_ROWS = 8
NA_COLS = 16
HGRN_DK = 128
TOP_K = 2
EPS = 1e-6
F_MIN = 1e-30
NEG_BIG = -1e30

LANE = 128
SUBLANE = 8
VMEM_LIMIT_BYTES = 56 * 1024 * 1024

MXU_DTYPE = jnp.bfloat16
ACT_DTYPE = jnp.bfloat16

HGRN_CHUNK = 128
HGRN_CHUNKS_PER_STEP = 2
HGRN_MATMUL_LEVELS = 3
NA_ROW_UNROLL = 4
NA_QGROUP = 16
F32 = jnp.float32


def _cparams(sem, vmem=VMEM_LIMIT_BYTES):
    return pltpu.CompilerParams(dimension_semantics=sem, vmem_limit_bytes=vmem)


def _sigmoid(x):
    return 1.0 / (1.0 + jnp.exp(-x))


def _neg_abs(x):
    return pltpu.bitcast(pltpu.bitcast(x, jnp.uint32) | jnp.uint32(0x80000000), F32)


def _dot(a, b):
    return jnp.dot(a, b, preferred_element_type=F32)


def _dot_nt(a, b):
    return lax.dot_general(a, b, (((1,), (1,)), ((), ())), preferred_element_type=F32)


def _dot_tn(a, b):
    return lax.dot_general(a, b, (((0,), (0,)), ((), ())), preferred_element_type=F32)


def _rms(x, g):
    return x * lax.rsqrt(jnp.mean(x * x, axis=-1, keepdims=True) + EPS) * g


def _norm_mod(x, g, sh, sc):
    return _rms(x, g) * (1.0 + sc) + sh


def _tile(n, pref, mult=SUBLANE):
    if n <= pref:
        return n
    t = (pref // mult) * mult
    while t >= mult:
        if n % t == 0:
            return t
        t -= mult
    return n


def _ada_kernel(c_ref, w_ref, b_ref, o_ref):
    c = c_ref[...]
    cs = (c * _sigmoid(c)).astype(MXU_DTYPE)
    o_ref[0] = _dot(cs, w_ref[0].astype(MXU_DTYPE)) + b_ref[0]


def _ada(c_all, ada_w, ada_b):
    depth, d, n = ada_w.shape
    rows = c_all.shape[0]
    tn = _tile(n, 1024, LANE)
    return pl.pallas_call(
        _ada_kernel,
        grid=(depth, n // tn),
        in_specs=[
            pl.BlockSpec((rows, d), lambda l, j: (0, 0)),
            pl.BlockSpec((1, d, tn), lambda l, j: (l, 0, j)),
            pl.BlockSpec((1, 1, tn), lambda l, j: (l, 0, j)),
        ],
        out_specs=pl.BlockSpec((1, rows, tn), lambda l, j: (l, 0, j)),
        out_shape=jax.ShapeDtypeStruct((depth, rows, n), F32),
        compiler_params=_cparams(("parallel", "parallel")),
        name="ada",
    )(c_all, ada_w, ada_b.reshape(depth, 1, n))


def _row_out_specs(t, d, tm, split_rows, rank):
    pad = (lambda f: (lambda i: (f(i), 0))) if rank == 1 else (lambda f: (lambda i, j: (f(i), 0)))
    if split_rows is None:
        return pl.BlockSpec((tm, d), pad(lambda i: i)), jax.ShapeDtypeStruct((t, d), F32), None
    nba, nbb = split_rows // tm, (t - split_rows) // tm
    specs = [pl.BlockSpec((tm, d), pad(lambda i: jnp.minimum(i, nba - 1))),
             pl.BlockSpec((tm, d), pad(lambda i: jnp.clip(i - nba, 0, nbb - 1)))]
    shapes = [jax.ShapeDtypeStruct((split_rows, d), F32), jax.ShapeDtypeStruct((t - split_rows, d), F32)]
    return specs, shapes, nba


def _row_out_store(o_refs, nba, val):
    if nba is None:
        o_refs[0][...] = val
        return

    @pl.when(pl.program_id(0) < nba)
    def _():
        o_refs[0][...] = val

    @pl.when(pl.program_id(0) >= nba)
    def _():
        o_refs[1][...] = val


def _nm_matmul_kernel(x_ref, g_ref, sh_ref, sc_ref, w_ref, oa_ref, of_ref, h_ref, *, nja):
    j = pl.program_id(1)

    @pl.when(j == 0)
    def _():
        h_ref[...] = _norm_mod(x_ref[...], g_ref[...], sh_ref[0], sc_ref[0]).astype(h_ref.dtype)

    r = _dot(h_ref[...], w_ref[...])

    @pl.when(j < nja)
    def _():
        oa_ref[...] = r.astype(oa_ref.dtype)

    @pl.when(j >= nja)
    def _():
        of_ref[...] = r


def _nm_matmul(x, g, sh, sc, w, n_a, lv, tm_pref=1024, tn_pref=1024):
    t, d = x.shape
    n = w.shape[1]
    n_f = n - n_a
    tm = _tile(lv, tm_pref)
    tn = _tile(math.gcd(n_a, n_f), tn_pref, LANE)
    nja = n_a // tn
    return pl.pallas_call(
        functools.partial(_nm_matmul_kernel, nja=nja),
        grid=(t // tm, n // tn),
        in_specs=[
            pl.BlockSpec((tm, d), lambda i, j: (i, 0)),
            pl.BlockSpec((1, d), lambda i, j: (0, 0)),
            pl.BlockSpec((1, 1, d), lambda i, j: (i * tm // lv, 0, 0)),
            pl.BlockSpec((1, 1, d), lambda i, j: (i * tm // lv, 0, 0)),
            pl.BlockSpec((d, tn), lambda i, j: (0, j)),
        ],
        out_specs=[
            pl.BlockSpec((tm, tn), lambda i, j: (i, jnp.minimum(j, nja - 1))),
            pl.BlockSpec((tm, tn), lambda i, j: (i, jnp.maximum(j - nja, 0))),
        ],
        out_shape=[jax.ShapeDtypeStruct((t, n_a), ACT_DTYPE), jax.ShapeDtypeStruct((t, n_f), F32)],
        scratch_shapes=[pltpu.VMEM((tm, d), MXU_DTYPE)],
        compiler_params=_cparams(("parallel", "arbitrary")),
        name="nm_matmul",
    )(x, g, sh, sc, w)


def _na_bias_table(rpb):
    cols = np.arange(GRID_W)
    col_start = np.clip(cols - NA_COLS // 2, 0, GRID_W - NA_COLS)
    in_win = (cols[None, :] >= col_start[:, None]) & (cols[None, :] < col_start[:, None] + NA_COLS)
    dc = cols[None, :] - cols[:, None] + (NA_COLS - 1)
    onehot = ((dc[None] == np.arange(2 * NA_COLS - 1)[:, None, None]) & in_win[None]).astype(np.float32)
    full = jnp.einsum("hrd,dqk->hrqk", rpb.astype(F32), jnp.asarray(onehot), precision=lax.Precision.HIGHEST)
    tabs = []
    for delta in range(NA_ROWS):
        rows = full[:, NA_ROWS - 1 - delta:2 * NA_ROWS - 1 - delta]
        tabs.append(rows.transpose(0, 2, 1, 3).reshape(rpb.shape[0], GRID_W, NA_ROWS * GRID_W))
    mask = jnp.asarray(np.tile(in_win, (1, NA_ROWS)))
    return jnp.where(mask[None, None], jnp.stack(tabs, axis=0), NEG_BIG)


def _na_kernel(q_ref, k_ref, v_ref, bias_ref, o_ref, s0_ref, s1_ref, p0_ref, p1_ref, *, rows):
    s_refs = (s0_ref, s1_ref)
    p_refs = (p0_ref, p1_ref)
    win = NA_ROWS * GRID_W
    lane = lax.broadcasted_iota(jnp.int32, (1, LANE), 1)
    head_masks = [lane < NA_HEAD_DIM, lane >= NA_HEAD_DIM]
    scale = NA_HEAD_DIM ** -0.5

    def row_start(r):
        return jnp.clip(r - NA_ROWS // 2, 0, rows - NA_ROWS)

    def scores(g, slot):
        for u in range(NA_ROW_UNROLL):
            r = g * NA_ROW_UNROLL + u
            q2 = q_ref[pl.ds(pl.multiple_of(r * GRID_W, GRID_W), GRID_W), :]
            q2 = q2 * jnp.asarray(scale, q2.dtype)
            kw = k_ref[pl.ds(pl.multiple_of(row_start(r) * GRID_W, GRID_W), win), :]
            qq = jnp.concatenate([jnp.where(head_masks[h], q2, jnp.zeros_like(q2)) for h in range(2)], axis=0)
            s_refs[slot][u] = _dot_nt(qq, kw)

    def finish(g, slot):
        rden = []
        for u in range(NA_ROW_UNROLL):
            r = g * NA_ROW_UNROLL + u
            delta = r - row_start(r)
            parts = []
            for qg in range(2 * GRID_W // NA_QGROUP):
                h, gq = divmod(qg * NA_QGROUP, GRID_W)
                rows_g = pl.ds(qg * NA_QGROUP, NA_QGROUP)
                s = s_refs[slot][u, rows_g, :] + bias_ref[delta, h, pl.ds(gq, NA_QGROUP), :]
                m = jnp.max(s, axis=-1, keepdims=True)
                e = jnp.exp(s - m)
                parts.append(1.0 / jnp.sum(e, axis=-1, keepdims=True))
                p_refs[slot][u, rows_g, :] = e.astype(p0_ref.dtype)
            rden.append(jnp.concatenate(parts, axis=0))
        for u in range(NA_ROW_UNROLL):
            r = g * NA_ROW_UNROLL + u
            vw = v_ref[pl.ds(pl.multiple_of(row_start(r) * GRID_W, GRID_W), win), :]
            o2 = _dot(p_refs[slot][u], vw) * rden[u]
            out = jnp.where(head_masks[0], o2[:GRID_W], o2[GRID_W:])
            o_ref[pl.ds(pl.multiple_of(r * GRID_W, GRID_W), GRID_W), :] = out.astype(o_ref.dtype)

    ngroups = rows // NA_ROW_UNROLL
    scores(0, 0)

    def body(i, carry):
        g = 2 * i
        scores(g + 1, 1)
        finish(g, 0)
        scores(jnp.minimum(g + 2, ngroups - 1), 0)
        finish(g + 1, 1)
        return carry

    lax.fori_loop(0, ngroups // 2, body, 0)


def _na_call(u, bias_tab, y_prev, tok_off, nseq, seq_len):
    t = u.shape[0]
    bw = bias_tab.shape[1] * NA_HEAD_DIM
    npair = bw // LANE
    rows = seq_len // GRID_W
    assert rows >= NA_ROWS and rows % (2 * NA_ROW_UNROLL) == 0 and seq_len % GRID_W == 0 and tok_off % seq_len == 0
    s0 = tok_off // seq_len
    kern = functools.partial(_na_kernel, rows=rows)
    in_specs = [
        pl.BlockSpec((seq_len, LANE), lambda b, p: (s0 + b, p)),
        pl.BlockSpec((seq_len, LANE), lambda b, p: (s0 + b, npair + p)),
        pl.BlockSpec((seq_len, LANE), lambda b, p: (s0 + b, 2 * npair + p)),
        pl.BlockSpec((NA_ROWS, 2, GRID_W, NA_ROWS * GRID_W), lambda b, p: (0, p, 0, 0)),
    ]
    args = [u, u, u, bias_tab]
    aliases = {}
    if y_prev is not None:
        in_specs.append(pl.BlockSpec(memory_space=pl.ANY))
        args.append(y_prev)
        aliases = {4: 0}
        kern_fn = lambda q, k, v, b, _prev, *rest: kern(q, k, v, b, *rest)
    else:
        kern_fn = kern
    return pl.pallas_call(
        kern_fn,
        grid=(nseq, npair),
        in_specs=in_specs,
        out_specs=pl.BlockSpec((seq_len, LANE), lambda b, p: (s0 + b, p)),
        out_shape=jax.ShapeDtypeStruct((t, bw), ACT_DTYPE),
        scratch_shapes=[
            pltpu.VMEM((NA_ROW_UNROLL, 2 * GRID_W, NA_ROWS * GRID_W), F32),
            pltpu.VMEM((NA_ROW_UNROLL, 2 * GRID_W, NA_ROWS * GRID_W), F32),
            pltpu.VMEM((NA_ROW_UNROLL, 2 * GRID_W, NA_ROWS * GRID_W), MXU_DTYPE),
            pltpu.VMEM((NA_ROW_UNROLL, 2 * GRID_W, NA_ROWS * GRID_W), MXU_DTYPE),
        ],
        input_output_aliases=aliases,
        compiler_params=_cparams(("parallel", "parallel")),
        name="na",
    )(*args)


def _hgrn_consts(c, reverse):
    nlev = int(math.log2(c))
    nmat = min(HGRN_MATMUL_LEVELS, nlev)
    idx = np.arange(c)
    tri = (idx[None, :] <= idx[:, None]).astype(np.float32)
    mats = [tri]
    for l in range(nmat):
        m = 1 << l
        ref = (idx // (2 * m)) * (2 * m) + m - 1
        mats.append(tri[ref])
    mall = np.concatenate(mats, axis=0)
    x = idx[:, None] ^ idx[None, :]
    lvl = np.where(x > 0, np.floor(np.log2(np.maximum(x, 1))), -1).astype(np.int32)
    lvl = np.where(idx[None, :] > idx[:, None], -2, lvl)
    if reverse:
        mall = mall.reshape(nmat + 1, c, c)[:, ::-1, ::-1].reshape((nmat + 1) * c, c)
        lvl = lvl[::-1, ::-1]
    return np.ascontiguousarray(np.tile(mall, (1, 3))), np.ascontiguousarray(lvl), nlev


def _hgrn_direction(q_ref, i_ref, f_ref, rows, lb, mall, lvl, st_ref, d, *, c, nlev, nheads, reverse):
    fg = jnp.maximum(lb + (1.0 - lb) * _sigmoid(f_ref[rows, :]), F_MIN)
    lg = jnp.log2(fg)
    kk = 1.0 - fg
    hi = lg.astype(MXU_DTYPE)
    r1 = lg - hi.astype(F32)
    mid = r1.astype(MXU_DTYPE)
    lo = (r1 - mid.astype(F32)).astype(MXU_DTYPE)
    ball = _dot(mall, jnp.concatenate([hi, mid, lo], axis=0))
    nmat = min(HGRN_MATMUL_LEVELS, nlev)
    b_all = ball[0:c]
    bref = [ball[(l + 1) * c:(l + 2) * c] for l in range(nmat)]
    for l in range(nmat, nlev):
        m = 1 << l
        blocks = []
        for g0 in range(0, c, 2 * m):
            r = g0 + m if reverse else g0 + m - 1
            blocks.append(jnp.broadcast_to(b_all[r:r + 1, :], (2 * m, b_all.shape[1])))
        bref.append(blocks[0] if len(blocks) == 1 else jnp.concatenate(blocks, axis=0))
    q = q_ref[rows, :].astype(F32) * (HGRN_DK ** -0.5)
    v = i_ref[rows, :].astype(MXU_DTYPE)
    tot_row = 0 if reverse else c - 1
    outs = []
    for h in range(nheads):
        sl = slice(h * HGRN_DK, (h + 1) * HGRN_DK)
        bh = b_all[:, sl]
        qh = q[:, sl]
        kh = kk[:, sl]
        vh = v[:, sl]
        qb = qh.astype(MXU_DTYPE)
        kb = kh.astype(MXU_DTYPE)
        a = jnp.where(lvl == -1, _dot_nt(qb, kb), 0.0)
        for l in range(nlev):
            e = jnp.exp2(_neg_abs(bh - bref[l][:, sl])).astype(MXU_DTYPE)
            a = jnp.where(lvl == l, _dot_nt(qb * e, kb * e), a)
        o = _dot(a.astype(MXU_DTYPE), vh)
        st = st_ref[d * nheads + h]
        o = o + _dot_nt((qh * jnp.exp2(bh)).astype(MXU_DTYPE), st.astype(MXU_DTYPE))
        btot = bh[tot_row:tot_row + 1, :]
        kd = (kh * jnp.exp2(btot - bh)).astype(MXU_DTYPE)
        st_ref[d * nheads + h] = st * jnp.exp2(btot) + _dot_tn(vh, kd)
        outs.append(o)
    return outs


def _hgrn_kernel(fblk_ref, rblk_ref, first_ref, qf_ref, if_ref, ff_ref, qr_ref, ir_ref, fr_ref, lb_ref, mall_ref, lvl_ref,
                 of_ref, or_ref, st_ref, *, c, nlev, nheads):
    @pl.when(first_ref[pl.program_id(0)] == 1)
    def _():
        st_ref[...] = jnp.zeros_like(st_ref)

    dirs = ((qf_ref, if_ref, ff_ref, of_ref, False), (qr_ref, ir_ref, fr_ref, or_ref, True))
    nsub = qf_ref.shape[0] // c
    for k in range(nsub):
        for d, (q_ref, i_ref, f_ref, o_ref, reverse) in enumerate(dirs):
            rows = pl.ds((nsub - 1 - k if reverse else k) * c, c)
            outs = _hgrn_direction(q_ref, i_ref, f_ref, rows, lb_ref[d:d + 1, :], mall_ref[d], lvl_ref[d], st_ref, d,
                                   c=c, nlev=nlev, nheads=nheads, reverse=reverse)
            for h in range(nheads):
                o_ref[rows, h * HGRN_DK:(h + 1) * HGRN_DK] = outs[h]


def _hgrn_call(u, uf, lb, seqs, cols):
    t = u.shape[0]
    bw = lb.shape[-1]
    nheads = bw // HGRN_DK
    c = min([HGRN_CHUNK] + [n for _, _, n in seqs])
    blk = c * HGRN_CHUNKS_PER_STEP
    assert all(n % blk == 0 and off % blk == 0 for off, _, n in seqs)
    fblk, rblk, first = [], [], []
    for off, cnt, n in seqs:
        nck = n // blk
        for b in range(cnt):
            base = (off + b * n) // blk
            fblk += [base + k for k in range(nck)]
            rblk += [base + nck - 1 - k for k in range(nck)]
            first += [1] + [0] * (nck - 1)
    consts = [_hgrn_consts(c, rev) for rev in (False, True)]
    nlev = consts[0][2]
    mall = jnp.asarray(np.stack([cc[0] for cc in consts]), MXU_DTYPE)
    lvl = jnp.asarray(np.stack([cc[1] for cc in consts]))
    cq, ci, cff, cfb = cols
    fwd = lambda col: pl.BlockSpec((blk, bw), lambda s, fb, rb, fr: (fb[s], col))
    rev = lambda col: pl.BlockSpec((blk, bw), lambda s, fb, rb, fr: (rb[s], col))
    const2 = lambda s, fb, rb, fr: (0, 0)
    const3 = lambda s, fb, rb, fr: (0, 0, 0)
    grid_spec = pltpu.PrefetchScalarGridSpec(
        num_scalar_prefetch=3,
        grid=(len(fblk),),
        in_specs=[fwd(cq), fwd(ci), fwd(cff), rev(cq), rev(ci), rev(cfb),
                  pl.BlockSpec((2, bw), const2), pl.BlockSpec(mall.shape, const3), pl.BlockSpec(lvl.shape, const3)],
        out_specs=[fwd(0), rev(0)],
        scratch_shapes=[pltpu.VMEM((2 * nheads, HGRN_DK, HGRN_DK), F32)],
    )
    tab = lambda z: jnp.asarray(np.asarray(z, np.int32))
    return pl.pallas_call(
        functools.partial(_hgrn_kernel, c=c, nlev=nlev, nheads=nheads),
        grid_spec=grid_spec,
        out_shape=[jax.ShapeDtypeStruct((t, bw), F32), jax.ShapeDtypeStruct((t, bw), F32)],
        compiler_params=_cparams(("arbitrary",)),
        name="hgrn",
    )(tab(fblk), tab(rblk), tab(first), u, u, uf, u, u, uf, lb, mall, lvl)


CONV_HALO = 16
CONV_ROWS = 32


def _conv_kernel(first_ref, last_ref, a_ref, g_ref, ap_ref, gp_ref, an_ref, gn_ref,
                 w_ref, b_ref, lg_ref, lb_ref, o_ref, hbuf, *, tb, width):
    i = pl.program_id(0)
    pad = width // 2

    def glu(a, g):
        return a.astype(F32) * _sigmoid(g.astype(F32))

    hbuf[0, CONV_HALO:CONV_HALO + tb, :] = glu(a_ref[...], g_ref[...])
    hp = glu(ap_ref[...], gp_ref[...])
    hbuf[0, 0:CONV_HALO, :] = jnp.where(first_ref[i] == 1, 0.0, hp)
    hn = glu(an_ref[...], gn_ref[...])
    hbuf[0, CONV_HALO + tb:2 * CONV_HALO + tb, :] = jnp.where(last_ref[i] == 1, 0.0, hn)
    span = tb + 2 * CONV_HALO - SUBLANE
    for k in range(1, SUBLANE):
        hbuf[k, 0:span, :] = hbuf[0, k:k + span, :]
    w = w_ref[...]
    for r0 in range(0, tb, CONV_ROWS):
        acc = jnp.zeros((CONV_ROWS, w.shape[1]), F32)
        for j in range(width):
            s = CONV_HALO + r0 + j - pad
            k = s % SUBLANE
            acc = acc + w[j:j + 1, :] * hbuf[k, s - k:s - k + CONV_ROWS, :]
        h = acc + b_ref[...]
        mu = jnp.mean(h, axis=-1, keepdims=True)
        hc = h - mu
        var = jnp.mean(hc * hc, axis=-1, keepdims=True)
        y = hc * lax.rsqrt(var + EPS) * lg_ref[...] + lb_ref[...]
        o_ref[r0:r0 + CONV_ROWS, :] = (y * _sigmoid(y)).astype(o_ref.dtype)


def _conv_call(u, conv_w, conv_b, ln_g, ln_b, seq_starts, seq_ends, col_a, col_g, tb):
    t = u.shape[0]
    width, ch = conv_w.shape
    assert width // 2 < CONV_HALO and tb % CONV_ROWS == 0
    nblk = t // tb
    hb = tb // CONV_HALO
    nh = t // CONV_HALO
    kern = functools.partial(_conv_kernel, tb=tb, width=width)
    cur = lambda col: pl.BlockSpec((tb, ch), lambda i, f, l: (i, col))
    prev = lambda col: pl.BlockSpec((CONV_HALO, ch), lambda i, f, l: (jnp.maximum(i * hb - 1, 0), col))
    nxt = lambda col: pl.BlockSpec((CONV_HALO, ch), lambda i, f, l: (jnp.minimum((i + 1) * hb, nh - 1), col))
    vec = lambda: pl.BlockSpec((1, ch), lambda i, f, l: (0, 0))
    grid_spec = pltpu.PrefetchScalarGridSpec(
        num_scalar_prefetch=2,
        grid=(nblk,),
        in_specs=[cur(col_a), cur(col_g), prev(col_a), prev(col_g), nxt(col_a), nxt(col_g),
                  pl.BlockSpec((width, ch), lambda i, f, l: (0, 0)), vec(), vec(), vec()],
        out_specs=pl.BlockSpec((tb, ch), lambda i, f, l: (i, 0)),
        scratch_shapes=[pltpu.VMEM((SUBLANE, tb + 2 * CONV_HALO, ch), F32)],
    )
    return pl.pallas_call(
        kern,
        grid_spec=grid_spec,
        out_shape=jax.ShapeDtypeStruct((t, ch), ACT_DTYPE),
        compiler_params=_cparams(("parallel",)),
        name="conv",
    )(seq_starts, seq_ends, u, u, u, u, u, u, conv_w, conv_b.reshape(1, ch), ln_g.reshape(1, ch), ln_b.reshape(1, ch))


def _merge_kernel(ya_ref, of_ref, or_ref, og_ref, ng_ref, yc_ref, g0_ref, g1_ref, g2_ref, wb_ref, wo_ref, x_ref,
                  gn_ref, gt_ref, o_ref, yb_ref, m_ref, *, cw):
    d = o_ref.shape[1]
    o = of_ref[...] + or_ref[...]
    og = og_ref[...].astype(F32)
    gate = ng_ref[...] * (og * _sigmoid(og))
    for h in range(o.shape[1] // HGRN_DK):
        sl = slice(h * HGRN_DK, (h + 1) * HGRN_DK)
        oh = o[:, sl]
        oh = oh * lax.rsqrt(jnp.mean(oh * oh, axis=-1, keepdims=True) + EPS)
        yb_ref[:, sl] = (oh * gate[:, sl]).astype(yb_ref.dtype)
    ys = (ya_ref, yb_ref, yc_ref)
    gs = (g0_ref, g1_ref, g2_ref)
    for cb in range(d // cw):
        cs = slice(cb * cw, (cb + 1) * cw)
        acc = None
        for i in range(3):
            term = _sigmoid(gs[i][:, cs].astype(F32)) * _dot(ys[i][...], wb_ref[i, :, cs])
            acc = term if acc is None else acc + term
        m_ref[:, cs] = acc.astype(m_ref.dtype)
    y = _dot(m_ref[...], wo_ref[...])
    o_ref[...] = x_ref[...] + gt_ref[0] * _rms(y, gn_ref[...])


def _merge_call(ya, o_fwd, o_rev, yc, u, og_col, gate_col0, hgrn_g, wb, wo, x, gn, gt, lv, tm_pref=256):
    t, bw = ya.shape
    d = wo.shape[1]
    tm = _tile(lv, tm_pref)
    cw = _tile(d, 512, LANE)
    row = lambda i: (i, 0)
    branch = lambda: pl.BlockSpec((tm, bw), row)
    gate = lambda k: pl.BlockSpec((tm, d), lambda i: (i, gate_col0 + k))
    const2 = lambda i: (0, 0)
    return pl.pallas_call(
        functools.partial(_merge_kernel, cw=cw),
        grid=(t // tm,),
        in_specs=[
            branch(), branch(), branch(),
            pl.BlockSpec((tm, bw), lambda i: (i, og_col)),
            pl.BlockSpec((1, bw), const2),
            branch(),
            gate(0), gate(1), gate(2),
            pl.BlockSpec(wb.shape, lambda i: (0, 0, 0)),
            pl.BlockSpec(wo.shape, const2),
            pl.BlockSpec((tm, d), row),
            pl.BlockSpec((1, d), const2),
            pl.BlockSpec((1, 1, d), lambda i: (i * tm // lv, 0, 0)),
        ],
        out_specs=pl.BlockSpec((tm, d), row),
        out_shape=jax.ShapeDtypeStruct((t, d), F32),
        scratch_shapes=[pltpu.VMEM((tm, bw), MXU_DTYPE), pltpu.VMEM((tm, d), MXU_DTYPE)],
        compiler_params=_cparams(("parallel",)),
        name="merge",
    )(ya, o_fwd, o_rev, u, hgrn_g.reshape(1, bw), yc, u, u, u, wb, wo, x, gn, gt)


def _ffn_kernel(x_ref, g_ref, sh_ref, sc_ref, w1_ref, w3_ref, w2_ref, gn_ref, gt_ref, *rest, nba):
    *o_refs, h_ref, acc_ref = rest
    j = pl.program_id(1)

    @pl.when(j == 0)
    def _():
        h_ref[...] = _norm_mod(x_ref[...], g_ref[...], sh_ref[0], sc_ref[0]).astype(h_ref.dtype)
        acc_ref[...] = jnp.zeros_like(acc_ref)

    h = h_ref[...]
    a = _dot(h, w1_ref[...])
    g = _dot(h, w3_ref[...])
    m = (a * _sigmoid(a) * g).astype(MXU_DTYPE)
    acc_ref[...] += _dot(m, w2_ref[...])

    @pl.when(j == pl.num_programs(1) - 1)
    def _():
        _row_out_store(o_refs, nba, x_ref[...] + gt_ref[0] * _rms(acc_ref[...], gn_ref[...]))


def _ffn_call(x, g, sh, sc, w13, w2, gn, gt, lv, split_rows, tm_pref=512, tf_pref=512):
    t, d = x.shape
    f = w2.shape[0]
    tm = _tile(lv, tm_pref)
    tf = _tile(f, tf_pref, LANE)
    nf = f // tf
    row = lambda i, j: (i, 0)
    const2 = lambda i, j: (0, 0)
    mod = lambda i, j: (i * tm // lv, 0, 0)
    out_specs, out_shape, nba = _row_out_specs(t, d, tm, split_rows, 2)
    return pl.pallas_call(
        functools.partial(_ffn_kernel, nba=nba),
        grid=(t // tm, nf),
        in_specs=[
            pl.BlockSpec((tm, d), row),
            pl.BlockSpec((1, d), const2),
            pl.BlockSpec((1, 1, d), mod),
            pl.BlockSpec((1, 1, d), mod),
            pl.BlockSpec((d, tf), lambda i, j: (0, j)),
            pl.BlockSpec((d, tf), lambda i, j: (0, nf + j)),
            pl.BlockSpec((tf, d), lambda i, j: (j, 0)),
            pl.BlockSpec((1, d), const2),
            pl.BlockSpec((1, 1, d), mod),
        ],
        out_specs=out_specs,
        out_shape=out_shape,
        scratch_shapes=[pltpu.VMEM((tm, d), MXU_DTYPE), pltpu.VMEM((tm, d), F32)],
        compiler_params=_cparams(("parallel", "arbitrary")),
        name="ffn",
    )(x, g, sh, sc, w13, w13, w2, gn, gt)


ROUTE_ROWS = 8
ROW_DMA_UNROLL = 8


def _pack_rows(x):
    half = x.shape[1] // 2
    bits = pltpu.bitcast(x.astype(jnp.bfloat16).astype(F32), jnp.uint32)
    return (bits[:, :half] >> 16) | bits[:, half:]


def _unpack_rows(p):
    lo = pltpu.bitcast(p << 16, F32)
    hi = pltpu.bitcast(p & jnp.uint32(0xFFFF0000), F32)
    return lo, hi


def _route_kernel(x_ref, g_ref, sh_ref, sc_ref, rw_ref, tri_ref, route_ref, rt_ref, cnt_ref, xs_hbm,
                  h_ref, run_ref, idx_ref, cntv_ref, cnts_ref, zrow_ref, sem_idx, sem, *, n_experts, cap, tb, tm):
    step = pl.program_id(0)
    last = pl.num_programs(0) - 1
    slot = lax.rem(step, 2)

    def row_copy(sl, k, dst_row):
        return pltpu.make_async_copy(h_ref.at[sl].at[pl.ds(k, 1)], xs_hbm.at[pl.ds(dst_row, 1)], sem.at[sl])

    def drain(sl):
        def body(k, carry):
            row_copy(sl, 0, 0).wait()
            row_copy(sl, 0, 0).wait()
            return carry

        lax.fori_loop(0, tb, body, 0, unroll=ROW_DMA_UNROLL)

    def compute():
        h = _norm_mod(x_ref[...], g_ref[...], sh_ref[0], sc_ref[0])
        h_hi = h.astype(MXU_DTYPE)
        h_lo = (h - h_hi.astype(F32)).astype(MXU_DTYPE)
        logits = _dot(h_hi, rw_ref[0]) + _dot(h_hi, rw_ref[1]) + _dot(h_lo, rw_ref[0])
        lane = lax.broadcasted_iota(jnp.int32, logits.shape, 1)
        neg_inf = -jnp.inf
        l1 = jnp.where(lane < n_experts, logits, neg_inf)
        m1 = jnp.max(l1, axis=-1, keepdims=True)
        i1 = jnp.min(jnp.where(l1 == m1, lane, LANE), axis=-1, keepdims=True)
        l2 = jnp.where(lane == i1, neg_inf, l1)
        m2 = jnp.max(l2, axis=-1, keepdims=True)
        i2 = jnp.min(jnp.where(l2 == m2, lane, LANE), axis=-1, keepdims=True)
        e = jnp.exp(m2 - m1)
        w0 = 1.0 / (1.0 + e)
        w1 = e * w0
        sel1 = lane == i1
        sel2 = lane == i2
        member = jnp.where(sel1, 1.0, jnp.where(sel2, 1.0, 0.0))
        rank = _dot(tri_ref[...], member.astype(MXU_DTYPE))
        base = run_ref[...] + rank + lane.astype(F32) * float(cap)
        pos0 = jnp.sum(jnp.where(sel1, base, 0.0), axis=-1, keepdims=True)
        pos1 = jnp.sum(jnp.where(sel2, base, 0.0), axis=-1, keepdims=True)
        route = jnp.where(lane == 0, pos0,
                          jnp.where(lane == 1, pos1, jnp.where(lane == 2, w0, jnp.where(lane == 3, w1, 0.0))))
        route_ref[...] = route
        rt_ref[...] = route.T[0:ROUTE_ROWS, :].astype(jnp.int32)
        run_ref[...] += jnp.sum(member, axis=0, keepdims=True)
        cnt_ref[...] = run_ref[...]
        h_ref[slot] = _pack_rows(h)

    @pl.when(step == 0)
    def _():
        run_ref[...] = jnp.zeros_like(run_ref)
        compute()

    @pl.when(step > 0)
    def _():
        @pl.when(step > 1)
        def _():
            drain(slot)

        for k in range(tb):
            row_copy(1 - slot, k, idx_ref[1 - slot, 0, k]).start()
            row_copy(1 - slot, k, idx_ref[1 - slot, 1, k]).start()
        compute()

    cp = pltpu.make_async_copy(rt_ref, idx_ref.at[slot], sem_idx)
    cp.start()
    cp.wait()

    @pl.when(step == last)
    def _():
        def issue(k, carry):
            row_copy(slot, k, idx_ref[slot, 0, k]).start()
            row_copy(slot, k, idx_ref[slot, 1, k]).start()
            return carry

        lax.fori_loop(0, tb, issue, 0, unroll=ROW_DMA_UNROLL)

        @pl.when(step > 0)
        def _():
            drain(1 - slot)

        drain(slot)
        zrow_ref[...] = jnp.zeros_like(zrow_ref)
        cntv_ref[...] = jnp.broadcast_to(run_ref[...], cntv_ref.shape).astype(jnp.int32)
        cpc = pltpu.make_async_copy(cntv_ref, cnts_ref, sem_idx)
        cpc.start()
        cpc.wait()

        def zero_copy(dst_row):
            return pltpu.make_async_copy(zrow_ref.at[pl.ds(0, 1)], xs_hbm.at[pl.ds(dst_row, 1)], sem.at[0])

        for e in range(n_experts):
            cnt = cnts_ref[0, e]
            end = ((cnt + tm - 1) // tm) * tm

            def zissue(r, carry, e=e):
                zero_copy(e * cap + r).start()
                return carry

            def zdrain(r, carry):
                zero_copy(0).wait()
                return carry

            lax.fori_loop(cnt, end, zissue, 0)
            lax.fori_loop(cnt, end, zdrain, 0)


def _route_call(x, g, sh, sc, router_w, lv, cap, tm, tb_pref=256):
    t, d = x.shape
    n_experts = router_w.shape[1]
    tb = _tile(lv, tb_pref, LANE)
    rw = jnp.zeros((d, LANE), F32).at[:, :n_experts].set(router_w)
    rw_hi = rw.astype(MXU_DTYPE)
    rw = jnp.stack([rw_hi, (rw - rw_hi.astype(F32)).astype(MXU_DTYPE)])
    idx = np.arange(tb)
    tri = jnp.asarray((idx[None, :] < idx[:, None]).astype(np.float32), MXU_DTYPE)
    row = lambda i: (i, 0)
    const2 = lambda i: (0, 0)
    mod = lambda i: (i * tb // lv, 0, 0)
    return pl.pallas_call(
        functools.partial(_route_kernel, n_experts=n_experts, cap=cap, tb=tb, tm=tm),
        grid=(t // tb,),
        in_specs=[
            pl.BlockSpec((tb, d), row),
            pl.BlockSpec((1, d), const2),
            pl.BlockSpec((1, 1, d), mod),
            pl.BlockSpec((1, 1, d), mod),
            pl.BlockSpec((2, d, LANE), lambda i: (0, 0, 0)),
            pl.BlockSpec((tb, tb), const2),
        ],
        out_specs=[
            pl.BlockSpec((tb, LANE), row),
            pl.BlockSpec((ROUTE_ROWS, tb), lambda i: (0, i)),
            pl.BlockSpec((1, LANE), const2),
            pl.BlockSpec(memory_space=pl.ANY),
        ],
        out_shape=[
            jax.ShapeDtypeStruct((t, LANE), F32),
            jax.ShapeDtypeStruct((ROUTE_ROWS, t), jnp.int32),
            jax.ShapeDtypeStruct((1, LANE), F32),
            jax.ShapeDtypeStruct((n_experts * cap, d // 2), jnp.uint32),
        ],
        scratch_shapes=[
            pltpu.VMEM((2, tb, d // 2), jnp.uint32),
            pltpu.VMEM((1, LANE), F32),
            pltpu.SMEM((2, ROUTE_ROWS, tb), jnp.int32),
            pltpu.VMEM((SUBLANE, LANE), jnp.int32),
            pltpu.SMEM((SUBLANE, LANE), jnp.int32),
            pltpu.VMEM((SUBLANE, d // 2), jnp.uint32),
            pltpu.SemaphoreType.DMA,
            pltpu.SemaphoreType.DMA((2,)),
        ],
        compiler_params=_cparams(("arbitrary",)),
        name="route",
    )(x, g, sh, sc, rw, tri)


def _gmm_kernel(be_ref, br_ref, bv_ref, xs_ref, w1_ref, w3_ref, w2_ref, y_ref, xb_ref, acc_ref):
    i = pl.program_id(0)
    j = pl.program_id(1)

    @pl.when(bv_ref[i] == 1)
    def _():
        @pl.when(j == 0)
        def _():
            half = xs_ref.shape[1]
            lo, hi = _unpack_rows(xs_ref[...])
            xb_ref[:, :half] = lo.astype(xb_ref.dtype)
            xb_ref[:, half:] = hi.astype(xb_ref.dtype)
            acc_ref[...] = jnp.zeros_like(acc_ref)

        xb = xb_ref[...]
        a = _dot(xb, w1_ref[0])
        g = _dot(xb, w3_ref[0])
        m = (a * _sigmoid(a) * g).astype(MXU_DTYPE)
        acc_ref[...] += _dot(m, w2_ref[0])

        @pl.when(j == pl.num_programs(1) - 1)
        def _():
            y_ref[...] = _pack_rows(acc_ref[...])


def _gmm_call(blk_e, blk_row, blk_valid, xs, w13, w2, tm, tf_pref=1024):
    n_rows, half = xs.shape
    d = 2 * half
    f = w2.shape[1]
    tf = _tile(f, tf_pref, LANE)
    nf = f // tf
    n_blk = blk_e.shape[0]

    def jsel(i, j, bv):
        return jnp.where(bv[i] == 1, j, nf - 1)

    grid_spec = pltpu.PrefetchScalarGridSpec(
        num_scalar_prefetch=3,
        grid=(n_blk, nf),
        in_specs=[
            pl.BlockSpec((tm, half), lambda i, j, be, br, bv: (br[i], 0)),
            pl.BlockSpec((1, d, tf), lambda i, j, be, br, bv: (be[i], 0, jsel(i, j, bv))),
            pl.BlockSpec((1, d, tf), lambda i, j, be, br, bv: (be[i], 0, nf + jsel(i, j, bv))),
            pl.BlockSpec((1, tf, d), lambda i, j, be, br, bv: (be[i], jsel(i, j, bv), 0)),
        ],
        out_specs=pl.BlockSpec((tm, half), lambda i, j, be, br, bv: (br[i], 0)),
        scratch_shapes=[pltpu.VMEM((tm, d), MXU_DTYPE), pltpu.VMEM((tm, d), F32)],
    )
    return pl.pallas_call(
        _gmm_kernel,
        grid_spec=grid_spec,
        out_shape=jax.ShapeDtypeStruct((n_rows, half), jnp.uint32),
        compiler_params=_cparams(("arbitrary", "arbitrary")),
        name="moe_gmm",
    )(blk_e, blk_row, blk_valid, xs, w13, w13, w2)


def _combine_kernel(rt_ref, rtn_ref, route_ref, x_ref, gn_ref, gt_ref, y_hbm, *rest, tb, nba):
    *o_refs, idx_ref, y0_ref, y1_ref, sem_idx, sem = rest
    step = pl.program_id(0)
    slot = lax.rem(step, 2)

    def row_copy(src_row, dst_ref, sl, k, s):
        return pltpu.make_async_copy(y_hbm.at[pl.ds(src_row, 1)], dst_ref.at[sl].at[pl.ds(k, 1)], sem.at[sl, s])

    def load_table(table_ref):
        cp = pltpu.make_async_copy(table_ref, idx_ref, sem_idx)
        cp.start()
        cp.wait()

    def drain(sl):
        def body(k, carry):
            row_copy(0, y0_ref, sl, k, 0).wait()
            row_copy(0, y1_ref, sl, k, 1).wait()
            return carry

        lax.fori_loop(0, tb, body, 0, unroll=ROW_DMA_UNROLL)

    @pl.when(step == 0)
    def _():
        load_table(rt_ref)

        def issue(k, carry):
            row_copy(idx_ref[0, k], y0_ref, 0, k, 0).start()
            row_copy(idx_ref[1, k], y1_ref, 0, k, 1).start()
            return carry

        lax.fori_loop(0, tb, issue, 0, unroll=ROW_DMA_UNROLL)

    load_table(rtn_ref)
    drain(slot)
    for k in range(tb):
        row_copy(idx_ref[0, k], y0_ref, 1 - slot, k, 0).start()
        row_copy(idx_ref[1, k], y1_ref, 1 - slot, k, 1).start()
    route = route_ref[...]
    w0 = route[:, 2:3]
    w1 = route[:, 3:4]
    lo0, hi0 = _unpack_rows(y0_ref[slot])
    lo1, hi1 = _unpack_rows(y1_ref[slot])
    y = jnp.concatenate([w0 * lo0 + w1 * lo1, w0 * hi0 + w1 * hi1], axis=1)
    _row_out_store(o_refs, nba, x_ref[...] + gt_ref[0] * _rms(y, gn_ref[...]))

    @pl.when(step == pl.num_programs(0) - 1)
    def _():
        drain(1 - slot)


def _combine_call(rt, route, x, gn, gt, y, lv, split_rows, tb_pref=256):
    t, d = x.shape
    tb = _tile(lv, tb_pref, LANE)
    nblk = t // tb
    row = lambda i: (i, 0)
    out_specs, out_shape, nba = _row_out_specs(t, d, tb, split_rows, 1)
    return pl.pallas_call(
        functools.partial(_combine_kernel, tb=tb, nba=nba),
        grid=(nblk,),
        in_specs=[
            pl.BlockSpec((ROUTE_ROWS, tb), lambda i: (0, i)),
            pl.BlockSpec((ROUTE_ROWS, tb), lambda i: (0, jnp.minimum(i + 1, nblk - 1))),
            pl.BlockSpec((tb, LANE), row),
            pl.BlockSpec((tb, d), row),
            pl.BlockSpec((1, d), lambda i: (0, 0)),
            pl.BlockSpec((1, 1, d), lambda i: (i * tb // lv, 0, 0)),
            pl.BlockSpec(memory_space=pl.ANY),
        ],
        out_specs=out_specs,
        out_shape=out_shape,
        scratch_shapes=[
            pltpu.SMEM((ROUTE_ROWS, tb), jnp.int32),
            pltpu.VMEM((2, tb, d // 2), jnp.uint32),
            pltpu.VMEM((2, tb, d // 2), jnp.uint32),
            pltpu.SemaphoreType.DMA,
            pltpu.SemaphoreType.DMA((2, 2)),
        ],
        compiler_params=_cparams(("arbitrary",)),
        name="moe_combine",
    )(rt, rt, route, x, gn, gt, y)


def _moe(x, g, sh, sc, router_w, w13, w2, gn, gt, lv, split_rows=None, tm_pref=512):
    t, d = x.shape
    n_experts = router_w.shape[1]
    tm = _tile(t, tm_pref, LANE)
    cap = -(-t // tm) * tm
    route, rt, cnt, xs = _route_call(x, g, sh, sc, router_w, lv, cap, tm)
    counts = cnt[0, :n_experts].astype(jnp.int32)
    nb = (counts + tm - 1) // tm
    ends = jnp.cumsum(nb)
    n_blk = TOP_K * t // tm + n_experts
    bi = jnp.arange(n_blk, dtype=jnp.int32)
    valid = bi < ends[-1]
    bi_c = jnp.minimum(bi, ends[-1] - 1)
    be = jnp.minimum(jnp.searchsorted(ends, bi_c, side="right"), n_experts - 1).astype(jnp.int32)
    br = be * (cap // tm) + (bi_c - (ends - nb)[be])
    y = _gmm_call(be, br.astype(jnp.int32), valid.astype(jnp.int32), xs, w13, w2, tm)
    return _combine_call(rt, route, x, gn, gt, y, lv, split_rows)


def kernel(x_prompt, x_sample, c_prompt, c_sample, ada_w, ada_b, norm_g, w_in, na_rpb, hgrn_lb, hgrn_norm_g,
           conv_w, conv_b, conv_ln_g, conv_ln_b, w_branch, w_out, ffn_w13, ffn_w2, router_w, moe_w13, moe_w2):
    bp, lp, d = x_prompt.shape
    bs, ls, _ = x_sample.shape
    depth = ada_w.shape[0]
    bw = w_branch.shape[2]
    tp = bp * lp
    t = tp + bs * ls
    lv = math.gcd(lp, ls)
    n_vseq = t // lv
    assert 8 * bw % d == 0 and d % LANE == 0 and bw % LANE == 0

    x = jnp.concatenate([x_prompt.reshape(tp, d), x_sample.reshape(bs * ls, d)], axis=0)

    nb = bp + bs
    rows = -(-nb // SUBLANE) * SUBLANE
    c_all = jnp.zeros((rows, d), F32).at[:nb].set(jnp.concatenate([c_prompt, c_sample], axis=0))
    mod = _ada(c_all, ada_w, ada_b)
    starts = np.arange(n_vseq) * lv
    vb = np.where(starts < tp, starts // lp, bp + (starts - tp) // ls)
    mod_v = mod[:, vb, :].reshape(depth, n_vseq, 1, 6, d)
    part = lambda l, k: mod_v[l, :, :, k, :]

    sm = jax.nn.softmax(hgrn_lb.astype(F32), axis=0)
    lb_all = jnp.clip(jnp.cumsum(sm, axis=0) - sm[0], 0.0, 1.0)

    cb = lambda k: slice(k * bw, (k + 1) * bw)
    order = [0, 1, 2, 3, 4, 7, 8, 9]
    conv_tb = _tile(lv, 256)
    blk_start = np.arange(t // conv_tb) * conv_tb
    seq_pos = np.where(blk_start < tp, blk_start % lp, (blk_start - tp) % ls)
    seq_len_of = np.where(blk_start < tp, lp, ls)
    seq_starts = jnp.asarray((seq_pos == 0).astype(np.int32))
    seq_ends = jnp.asarray((seq_pos + conv_tb == seq_len_of).astype(np.int32))

    for l in range(depth):
        wl = w_in[l]
        w_all = jnp.concatenate([wl[:, cb(k)] for k in order] + [wl[:, 10 * bw:], wl[:, 5 * bw:7 * bw]],
                                axis=1).astype(MXU_DTYPE)
        u, uf = _nm_matmul(x, norm_g[l, 0].reshape(1, d), part(l, 0), part(l, 1), w_all, w_all.shape[1] - 2 * bw, lv)

        bias_tab = _na_bias_table(na_rpb[l])
        ya = _na_call(u, bias_tab, None, 0, bp, lp)
        ya = _na_call(u, bias_tab, ya, tp, bs, ls)

        o_fwd, o_rev = _hgrn_call(u, uf, lb_all[l], ((0, bp, lp), (tp, bs, ls)), (3, 4, 0, 1))

        yc = _conv_call(u, conv_w[l], conv_b[l], conv_ln_g[l], conv_ln_b[l], seq_starts, seq_ends, 6, 7, conv_tb)

        x = _merge_call(ya, o_fwd, o_rev, yc, u, 5, 8 * bw // d, hgrn_norm_g[l], w_branch[l].astype(MXU_DTYPE),
                        w_out[l].astype(MXU_DTYPE), x, norm_g[l, 1].reshape(1, d), part(l, 2), lv)

        g2 = norm_g[l, 2].reshape(1, d)
        g3 = norm_g[l, 3].reshape(1, d)
        split_rows = tp if l == depth - 1 else None
        if l % 2 == 0:
            x = _ffn_call(x, g2, part(l, 3), part(l, 4), ffn_w13[l // 2].astype(MXU_DTYPE),
                          ffn_w2[l // 2].astype(MXU_DTYPE), g3, part(l, 5), lv, split_rows)
        else:
            x = _moe(x, g2, part(l, 3), part(l, 4), router_w[l // 2], moe_w13[l // 2].astype(MXU_DTYPE),
                     moe_w2[l // 2].astype(MXU_DTYPE), g3, part(l, 5), lv, split_rows)

    y_prompt, y_sample = x
    return (y_prompt.reshape(bp, lp, d), y_sample.reshape(bs, ls, d))
```

```python
import functools
import math

import numpy as np
import jax
import jax.numpy as jnp
from jax import lax
from jax.experimental import pallas as pl
from jax.experimental.pallas import tpu as pltpu

GRID_W = 64
NA_HEAD_DIM = 64
NA_ROWS = 8
NA_COLS = 16
HGRN_DK = 128
TOP_K = 2
EPS = 1e-6
F_MIN = 1e-30
NEG_BIG = -1e30

LANE = 128
SUBLANE = 8
VMEM_LIMIT_BYTES = 56 * 1024 * 1024

MXU_DTYPE = jnp.bfloat16
ACT_DTYPE = jnp.bfloat16

HGRN_CHUNK = 128
HGRN_CHUNKS_PER_STEP = 4
HGRN_MATMUL_LEVELS = 3
NA_ROW_UNROLL = 4
NA_QGROUP = 16
F32 = jnp.float32


def _cparams(sem, vmem=VMEM_LIMIT_BYTES):
    return pltpu.CompilerParams(dimension_semantics=sem, vmem_limit_bytes=vmem)


def _sigmoid(x):
    return 1.0 / (1.0 + jnp.exp(-x))


def _neg_abs(x):
    return pltpu.bitcast(pltpu.bitcast(x, jnp.uint32) | jnp.uint32(0x80000000), F32)


def _dot(a, b):
    return jnp.dot(a, b, preferred_element_type=F32)


def _dot_nt(a, b):
    return lax.dot_general(a, b, (((1,), (1,)), ((), ())), preferred_element_type=F32)


def _dot_tn(a, b):
    return lax.dot_general(a, b, (((0,), (0,)), ((), ())), preferred_element_type=F32)


def _rms(x, g):
    return x * lax.rsqrt(jnp.mean(x * x, axis=-1, keepdims=True) + EPS) * g


def _norm_mod(x, g, sh, sc):
    return _rms(x, g) * (1.0 + sc) + sh


def _tile(n, pref, mult=SUBLANE):
    if n <= pref:
        return n
    t = (pref // mult) * mult
    while t >= mult:
        if n % t == 0:
            return t
        t -= mult
    return n


def _ada_kernel(c_ref, w_ref, b_ref, o_ref):
    c = c_ref[...]
    cs = (c * _sigmoid(c)).astype(MXU_DTYPE)
    o_ref[0] = _dot(cs, w_ref[0].astype(MXU_DTYPE)) + b_ref[0]


def _ada(c_all, ada_w, ada_b):
    depth, d, n = ada_w.shape
    rows = c_all.shape[0]
    tn = _tile(n, 1024, LANE)
    return pl.pallas_call(
        _ada_kernel,
        grid=(depth, n // tn),
        in_specs=[
            pl.BlockSpec((rows, d), lambda l, j: (0, 0)),
            pl.BlockSpec((1, d, tn), lambda l, j: (l, 0, j)),
            pl.BlockSpec((1, 1, tn), lambda l, j: (l, 0, j)),
        ],
        out_specs=pl.BlockSpec((1, rows, tn), lambda l, j: (l, 0, j)),
        out_shape=jax.ShapeDtypeStruct((depth, rows, n), F32),
        compiler_params=_cparams(("parallel", "parallel")),
        name="ada",
    )(c_all, ada_w, ada_b.reshape(depth, 1, n))


def _row_out_specs(t, d, tm, split_rows, rank):
    pad = (lambda f: (lambda i: (f(i), 0))) if rank == 1 else (lambda f: (lambda i, j: (f(i), 0)))
    if split_rows is None:
        return pl.BlockSpec((tm, d), pad(lambda i: i)), jax.ShapeDtypeStruct((t, d), F32), None
    nba, nbb = split_rows // tm, (t - split_rows) // tm
    specs = [pl.BlockSpec((tm, d), pad(lambda i: jnp.minimum(i, nba - 1))),
             pl.BlockSpec((tm, d), pad(lambda i: jnp.clip(i - nba, 0, nbb - 1)))]
    shapes = [jax.ShapeDtypeStruct((split_rows, d), F32), jax.ShapeDtypeStruct((t - split_rows, d), F32)]
    return specs, shapes, nba


def _row_out_store(o_refs, nba, val):
    if nba is None:
        o_refs[0][...] = val
        return

    @pl.when(pl.program_id(0) < nba)
    def _():
        o_refs[0][...] = val

    @pl.when(pl.program_id(0) >= nba)
    def _():
        o_refs[1][...] = val


def _nm_matmul_kernel(x_ref, g_ref, sh_ref, sc_ref, w_ref, oa_ref, of_ref, h_ref, *, nja):
    j = pl.program_id(1)

    @pl.when(j == 0)
    def _():
        h_ref[...] = _norm_mod(x_ref[...], g_ref[...], sh_ref[0], sc_ref[0]).astype(h_ref.dtype)

    r = _dot(h_ref[...], w_ref[...])

    @pl.when(j < nja)
    def _():
        oa_ref[...] = r.astype(oa_ref.dtype)

    @pl.when(j >= nja)
    def _():
        of_ref[...] = r


def _nm_matmul(x, g, sh, sc, w, n_a, lv, tm_pref=1024, tn_pref=1024):
    t, d = x.shape
    n = w.shape[1]
    n_f = n - n_a
    tm = _tile(lv, tm_pref)
    tn = _tile(math.gcd(n_a, n_f), tn_pref, LANE)
    nja = n_a // tn
    return pl.pallas_call(
        functools.partial(_nm_matmul_kernel, nja=nja),
        grid=(t // tm, n // tn),
        in_specs=[
            pl.BlockSpec((tm, d), lambda i, j: (i, 0)),
            pl.BlockSpec((1, d), lambda i, j: (0, 0)),
            pl.BlockSpec((1, 1, d), lambda i, j: (i * tm // lv, 0, 0)),
            pl.BlockSpec((1, 1, d), lambda i, j: (i * tm // lv, 0, 0)),
            pl.BlockSpec((d, tn), lambda i, j: (0, j)),
        ],
        out_specs=[
            pl.BlockSpec((tm, tn), lambda i, j: (i, jnp.minimum(j, nja - 1))),
            pl.BlockSpec((tm, tn), lambda i, j: (i, jnp.maximum(j - nja, 0))),
        ],
        out_shape=[jax.ShapeDtypeStruct((t, n_a), ACT_DTYPE), jax.ShapeDtypeStruct((t, n_f), F32)],
        scratch_shapes=[pltpu.VMEM((tm, d), MXU_DTYPE)],
        compiler_params=_cparams(("parallel", "arbitrary")),
        name="nm_matmul",
    )(x, g, sh, sc, w)


def _na_bias_table(rpb):
    cols = np.arange(GRID_W)
    col_start = np.clip(cols - NA_COLS // 2, 0, GRID_W - NA_COLS)
    in_win = (cols[None, :] >= col_start[:, None]) & (cols[None, :] < col_start[:, None] + NA_COLS)
    dc = cols[None, :] - cols[:, None] + (NA_COLS - 1)
    onehot = ((dc[None] == np.arange(2 * NA_COLS - 1)[:, None, None]) & in_win[None]).astype(np.float32)
    full = jnp.einsum("hrd,dqk->hrqk", rpb.astype(F32), jnp.asarray(onehot), precision=lax.Precision.HIGHEST)
    tabs = []
    for delta in range(NA_ROWS):
        rows = full[:, NA_ROWS - 1 - delta:2 * NA_ROWS - 1 - delta]
        tabs.append(rows.transpose(0, 2, 1, 3).reshape(rpb.shape[0], GRID_W, NA_ROWS * GRID_W))
    mask = jnp.asarray(np.tile(in_win, (1, NA_ROWS)))
    return jnp.where(mask[None, None], jnp.stack(tabs, axis=0), NEG_BIG)


def _na_kernel(q_ref, k_ref, v_ref, bias_ref, o_ref, s0_ref, s1_ref, p0_ref, p1_ref, *, rows):
    s_refs = (s0_ref, s1_ref)
    p_refs = (p0_ref, p1_ref)
    win = NA_ROWS * GRID_W
    lane = lax.broadcasted_iota(jnp.int32, (1, LANE), 1)
    head_masks = [lane < NA_HEAD_DIM, lane >= NA_HEAD_DIM]
    scale = NA_HEAD_DIM ** -0.5

    def row_start(r):
        return jnp.clip(r - NA_ROWS // 2, 0, rows - NA_ROWS)

    def scores(g, slot):
        for u in range(NA_ROW_UNROLL):
            r = g * NA_ROW_UNROLL + u
            q2 = q_ref[pl.ds(pl.multiple_of(r * GRID_W, GRID_W), GRID_W), :]
            q2 = q2 * jnp.asarray(scale, q2.dtype)
            kw = k_ref[pl.ds(pl.multiple_of(row_start(r) * GRID_W, GRID_W), win), :]
            qq = jnp.concatenate([jnp.where(head_masks[h], q2, jnp.zeros_like(q2)) for h in range(2)], axis=0)
            s_refs[slot][u] = _dot_nt(qq, kw)

    def finish(g, slot):
        rden = []
        for u in range(NA_ROW_UNROLL):
            r = g * NA_ROW_UNROLL + u
            delta = r - row_start(r)
            parts = []
            for qg in range(2 * GRID_W // NA_QGROUP):
                h, gq = divmod(qg * NA_QGROUP, GRID_W)
                rows_g = pl.ds(qg * NA_QGROUP, NA_QGROUP)
                s = s_refs[slot][u, rows_g, :] + bias_ref[delta, h, pl.ds(gq, NA_QGROUP), :]
                m = jnp.max(s, axis=-1, keepdims=True)
                e = jnp.exp(s - m)
                parts.append(1.0 / jnp.sum(e, axis=-1, keepdims=True))
                p_refs[slot][u, rows_g, :] = e.astype(p0_ref.dtype)
            rden.append(jnp.concatenate(parts, axis=0))
        for u in range(NA_ROW_UNROLL):
            r = g * NA_ROW_UNROLL + u
            vw = v_ref[pl.ds(pl.multiple_of(row_start(r) * GRID_W, GRID_W), win), :]
            o2 = _dot(p_refs[slot][u], vw) * rden[u]
            out = jnp.where(head_masks[0], o2[:GRID_W], o2[GRID_W:])
            o_ref[pl.ds(pl.multiple_of(r * GRID_W, GRID_W), GRID_W), :] = out.astype(o_ref.dtype)

    ngroups = rows // NA_ROW_UNROLL
    scores(0, 0)

    def body(i, carry):
        g = 2 * i
        scores(g + 1, 1)
        finish(g, 0)
        scores(jnp.minimum(g + 2, ngroups - 1), 0)
        finish(g + 1, 1)
        return carry

    lax.fori_loop(0, ngroups // 2, body, 0)


def _na_call(u, bias_tab, y_prev, tok_off, nseq, seq_len):
    t = u.shape[0]
    bw = bias_tab.shape[1] * NA_HEAD_DIM
    npair = bw // LANE
    rows = seq_len // GRID_W
    assert rows >= NA_ROWS and rows % (2 * NA_ROW_UNROLL) == 0 and seq_len % GRID_W == 0 and tok_off % seq_len == 0
    s0 = tok_off // seq_len
    kern = functools.partial(_na_kernel, rows=rows)
    in_specs = [
        pl.BlockSpec((seq_len, LANE), lambda b, p: (s0 + b, p)),
        pl.BlockSpec((seq_len, LANE), lambda b, p: (s0 + b, npair + p)),
        pl.BlockSpec((seq_len, LANE), lambda b, p: (s0 + b, 2 * npair + p)),
        pl.BlockSpec((NA_ROWS, 2, GRID_W, NA_ROWS * GRID_W), lambda b, p: (0, p, 0, 0)),
    ]
    args = [u, u, u, bias_tab]
    aliases = {}
    if y_prev is not None:
        in_specs.append(pl.BlockSpec(memory_space=pl.ANY))
        args.append(y_prev)
        aliases = {4: 0}
        kern_fn = lambda q, k, v, b, _prev, *rest: kern(q, k, v, b, *rest)
    else:
        kern_fn = kern
    return pl.pallas_call(
        kern_fn,
        grid=(nseq, npair),
        in_specs=in_specs,
        out_specs=pl.BlockSpec((seq_len, LANE), lambda b, p: (s0 + b, p)),
        out_shape=jax.ShapeDtypeStruct((t, bw), ACT_DTYPE),
        scratch_shapes=[
            pltpu.VMEM((NA_ROW_UNROLL, 2 * GRID_W, NA_ROWS * GRID_W), F32),
            pltpu.VMEM((NA_ROW_UNROLL, 2 * GRID_W, NA_ROWS * GRID_W), F32),
            pltpu.VMEM((NA_ROW_UNROLL, 2 * GRID_W, NA_ROWS * GRID_W), MXU_DTYPE),
            pltpu.VMEM((NA_ROW_UNROLL, 2 * GRID_W, NA_ROWS * GRID_W), MXU_DTYPE),
        ],
        input_output_aliases=aliases,
        compiler_params=_cparams(("parallel", "parallel")),
        name="na",
    )(*args)


def _hgrn_consts(c, reverse):
    nlev = int(math.log2(c))
    nmat = min(HGRN_MATMUL_LEVELS, nlev)
    idx = np.arange(c)
    tri = (idx[None, :] <= idx[:, None]).astype(np.float32)
    mats = [tri]
    for l in range(nmat):
        m = 1 << l
        ref = (idx // (2 * m)) * (2 * m) + m - 1
        mats.append(tri[ref])
    mall = np.concatenate(mats, axis=0)
    x = idx[:, None] ^ idx[None, :]
    lvl = np.where(x > 0, np.floor(np.log2(np.maximum(x, 1))), -1).astype(np.int32)
    lvl = np.where(idx[None, :] > idx[:, None], -2, lvl)
    if reverse:
        mall = mall.reshape(nmat + 1, c, c)[:, ::-1, ::-1].reshape((nmat + 1) * c, c)
        lvl = lvl[::-1, ::-1]
    return np.ascontiguousarray(np.tile(mall, (1, 3))), np.ascontiguousarray(lvl), nlev


def _hgrn_direction(q_ref, i_ref, f_ref, rows, lb, mall, lvl, st_ref, d, *, c, nlev, nheads, reverse):
    fg = jnp.maximum(lb + (1.0 - lb) * _sigmoid(f_ref[rows, :]), F_MIN)
    lg = jnp.log2(fg)
    kk = 1.0 - fg
    hi = lg.astype(MXU_DTYPE)
    r1 = lg - hi.astype(F32)
    mid = r1.astype(MXU_DTYPE)
    lo = (r1 - mid.astype(F32)).astype(MXU_DTYPE)
    ball = _dot(mall, jnp.concatenate([hi, mid, lo], axis=0))
    nmat = min(HGRN_MATMUL_LEVELS, nlev)
    b_all = ball[0:c]
    bref = [ball[(l + 1) * c:(l + 2) * c] for l in range(nmat)]
    for l in range(nmat, nlev):
        m = 1 << l
        blocks = []
        for g0 in range(0, c, 2 * m):
            r = g0 + m if reverse else g0 + m - 1
            blocks.append(jnp.broadcast_to(b_all[r:r + 1, :], (2 * m, b_all.shape[1])))
        bref.append(blocks[0] if len(blocks) == 1 else jnp.concatenate(blocks, axis=0))
    q = q_ref[rows, :].astype(F32) * (HGRN_DK ** -0.5)
    v = i_ref[rows, :].astype(MXU_DTYPE)
    tot_row = 0 if reverse else c - 1
    outs = []
    for h in range(nheads):
        sl = slice(h * HGRN_DK, (h + 1) * HGRN_DK)
        bh = b_all[:, sl]
        qh = q[:, sl]
        kh = kk[:, sl]
        vh = v[:, sl]
        qb = qh.astype(MXU_DTYPE)
        kb = kh.astype(MXU_DTYPE)
        a = jnp.where(lvl == -1, _dot_nt(qb, kb), 0.0)
        for l in range(nlev):
            e = jnp.exp2(_neg_abs(bh - bref[l][:, sl])).astype(MXU_DTYPE)
            a = jnp.where(lvl == l, _dot_nt(qb * e, kb * e), a)
        o = _dot(a.astype(MXU_DTYPE), vh)
        st = st_ref[d * nheads + h]
        o = o + _dot_nt((qh * jnp.exp2(bh)).astype(MXU_DTYPE), st.astype(MXU_DTYPE))
        btot = bh[tot_row:tot_row + 1, :]
        kd = (kh * jnp.exp2(btot - bh)).astype(MXU_DTYPE)
        st_ref[d * nheads + h] = st * jnp.exp2(btot) + _dot_tn(vh, kd)
        outs.append(o)
    return outs


def _hgrn_kernel(fblk_ref, rblk_ref, first_ref, qf_ref, if_ref, ff_ref, qr_ref, ir_ref, fr_ref, lb_ref, mall_ref, lvl_ref,
                 of_ref, or_ref, st_ref, *, c, nlev, nheads):
    @pl.when(first_ref[pl.program_id(0)] == 1)
    def _():
        st_ref[...] = jnp.zeros_like(st_ref)

    dirs = ((qf_ref, if_ref, ff_ref, of_ref, False), (qr_ref, ir_ref, fr_ref, or_ref, True))
    nsub = qf_ref.shape[0] // c
    for k in range(nsub):
        for d, (q_ref, i_ref, f_ref, o_ref, reverse) in enumerate(dirs):
            rows = pl.ds((nsub - 1 - k if reverse else k) * c, c)
            outs = _hgrn_direction(q_ref, i_ref, f_ref, rows, lb_ref[d:d + 1, :], mall_ref[d], lvl_ref[d], st_ref, d,
                                   c=c, nlev=nlev, nheads=nheads, reverse=reverse)
            for h in range(nheads):
                o_ref[rows, h * HGRN_DK:(h + 1) * HGRN_DK] = outs[h]


def _hgrn_call(u, uf, lb, seqs, cols):
    t = u.shape[0]
    bw = lb.shape[-1]
    nheads = bw // HGRN_DK
    c = min([HGRN_CHUNK] + [n for _, _, n in seqs])
    blk = c * HGRN_CHUNKS_PER_STEP
    assert all(n % blk == 0 and off % blk == 0 for off, _, n in seqs)
    fblk, rblk, first = [], [], []
    for off, cnt, n in seqs:
        nck = n // blk
        for b in range(cnt):
            base = (off + b * n) // blk
            fblk += [base + k for k in range(nck)]
            rblk += [base + nck - 1 - k for k in range(nck)]
            first += [1] + [0] * (nck - 1)
    consts = [_hgrn_consts(c, rev) for rev in (False, True)]
    nlev = consts[0][2]
    mall = jnp.asarray(np.stack([cc[0] for cc in consts]), MXU_DTYPE)
    lvl = jnp.asarray(np.stack([cc[1] for cc in consts]))
    cq, ci, cff, cfb = cols
    fwd = lambda col: pl.BlockSpec((blk, bw), lambda s, fb, rb, fr: (fb[s], col))
    rev = lambda col: pl.BlockSpec((blk, bw), lambda s, fb, rb, fr: (rb[s], col))
    const2 = lambda s, fb, rb, fr: (0, 0)
    const3 = lambda s, fb, rb, fr: (0, 0, 0)
    grid_spec = pltpu.PrefetchScalarGridSpec(
        num_scalar_prefetch=3,
        grid=(len(fblk),),
        in_specs=[fwd(cq), fwd(ci), fwd(cff), rev(cq), rev(ci), rev(cfb),
                  pl.BlockSpec((2, bw), const2), pl.BlockSpec(mall.shape, const3), pl.BlockSpec(lvl.shape, const3)],
        out_specs=[fwd(0), rev(0)],
        scratch_shapes=[pltpu.VMEM((2 * nheads, HGRN_DK, HGRN_DK), F32)],
    )
    tab = lambda z: jnp.asarray(np.asarray(z, np.int32))
    return pl.pallas_call(
        functools.partial(_hgrn_kernel, c=c, nlev=nlev, nheads=nheads),
        grid_spec=grid_spec,
        out_shape=[jax.ShapeDtypeStruct((t, bw), F32), jax.ShapeDtypeStruct((t, bw), F32)],
        compiler_params=_cparams(("arbitrary",)),
        name="hgrn",
    )(tab(fblk), tab(rblk), tab(first), u, u, uf, u, u, uf, lb, mall, lvl)


CONV_HALO = 16
CONV_ROWS = 32


def _conv_kernel(first_ref, last_ref, a_ref, g_ref, ap_ref, gp_ref, an_ref, gn_ref,
                 w_ref, b_ref, lg_ref, lb_ref, o_ref, hbuf, *, tb, width):
    i = pl.program_id(0)
    pad = width // 2

    def glu(a, g):
        return a.astype(F32) * _sigmoid(g.astype(F32))

    hbuf[0, CONV_HALO:CONV_HALO + tb, :] = glu(a_ref[...], g_ref[...])
    hp = glu(ap_ref[...], gp_ref[...])
    hbuf[0, 0:CONV_HALO, :] = jnp.where(first_ref[i] == 1, 0.0, hp)
    hn = glu(an_ref[...], gn_ref[...])
    hbuf[0, CONV_HALO + tb:2 * CONV_HALO + tb, :] = jnp.where(last_ref[i] == 1, 0.0, hn)
    span = tb + 2 * CONV_HALO - SUBLANE
    for k in range(1, SUBLANE):
        hbuf[k, 0:span, :] = hbuf[0, k:k + span, :]
    w = w_ref[...]
    for r0 in range(0, tb, CONV_ROWS):
        acc = jnp.zeros((CONV_ROWS, w.shape[1]), F32)
        for j in range(width):
            s = CONV_HALO + r0 + j - pad
            k = s % SUBLANE
            acc = acc + w[j:j + 1, :] * hbuf[k, s - k:s - k + CONV_ROWS, :]
        h = acc + b_ref[...]
        mu = jnp.mean(h, axis=-1, keepdims=True)
        hc = h - mu
        var = jnp.mean(hc * hc, axis=-1, keepdims=True)
        y = hc * lax.rsqrt(var + EPS) * lg_ref[...] + lb_ref[...]
        o_ref[r0:r0 + CONV_ROWS, :] = (y * _sigmoid(y)).astype(o_ref.dtype)


def _conv_call(u, conv_w, conv_b, ln_g, ln_b, seq_starts, seq_ends, col_a, col_g, tb):
    t = u.shape[0]
    width, ch = conv_w.shape
    assert width // 2 < CONV_HALO and tb % CONV_ROWS == 0
    nblk = t // tb
    hb = tb // CONV_HALO
    nh = t // CONV_HALO
    kern = functools.partial(_conv_kernel, tb=tb, width=width)
    cur = lambda col: pl.BlockSpec((tb, ch), lambda i, f, l: (i, col))
    prev = lambda col: pl.BlockSpec((CONV_HALO, ch), lambda i, f, l: (jnp.maximum(i * hb - 1, 0), col))
    nxt = lambda col: pl.BlockSpec((CONV_HALO, ch), lambda i, f, l: (jnp.minimum((i + 1) * hb, nh - 1), col))
    vec = lambda: pl.BlockSpec((1, ch), lambda i, f, l: (0, 0))
    grid_spec = pltpu.PrefetchScalarGridSpec(
        num_scalar_prefetch=2,
        grid=(nblk,),
        in_specs=[cur(col_a), cur(col_g), prev(col_a), prev(col_g), nxt(col_a), nxt(col_g),
                  pl.BlockSpec((width, ch), lambda i, f, l: (0, 0)), vec(), vec(), vec()],
        out_specs=pl.BlockSpec((tb, ch), lambda i, f, l: (i, 0)),
        scratch_shapes=[pltpu.VMEM((SUBLANE, tb + 2 * CONV_HALO, ch), F32)],
    )
    return pl.pallas_call(
        kern,
        grid_spec=grid_spec,
        out_shape=jax.ShapeDtypeStruct((t, ch), ACT_DTYPE),
        compiler_params=_cparams(("parallel",)),
        name="conv",
    )(seq_starts, seq_ends, u, u, u, u, u, u, conv_w, conv_b.reshape(1, ch), ln_g.reshape(1, ch), ln_b.reshape(1, ch))


def _merge_kernel(ya_ref, of_ref, or_ref, og_ref, ng_ref, yc_ref, g0_ref, g1_ref, g2_ref, wb_ref, wo_ref, x_ref,
                  gn_ref, gt_ref, o_ref, yb_ref, m_ref, *, cw):
    d = o_ref.shape[1]
    o = of_ref[...] + or_ref[...]
    og = og_ref[...].astype(F32)
    gate = ng_ref[...] * (og * _sigmoid(og))
    for h in range(o.shape[1] // HGRN_DK):
        sl = slice(h * HGRN_DK, (h + 1) * HGRN_DK)
        oh = o[:, sl]
        oh = oh * lax.rsqrt(jnp.mean(oh * oh, axis=-1, keepdims=True) + EPS)
        yb_ref[:, sl] = (oh * gate[:, sl]).astype(yb_ref.dtype)
    ys = (ya_ref, yb_ref, yc_ref)
    gs = (g0_ref, g1_ref, g2_ref)
    for cb in range(d // cw):
        cs = slice(cb * cw, (cb + 1) * cw)
        acc = None
        for i in range(3):
            term = _sigmoid(gs[i][:, cs].astype(F32)) * _dot(ys[i][...], wb_ref[i, :, cs])
            acc = term if acc is None else acc + term
        m_ref[:, cs] = acc.astype(m_ref.dtype)
    y = _dot(m_ref[...], wo_ref[...])
    o_ref[...] = x_ref[...] + gt_ref[0] * _rms(y, gn_ref[...])


def _merge_call(ya, o_fwd, o_rev, yc, u, og_col, gate_col0, hgrn_g, wb, wo, x, gn, gt, lv, tm_pref=256):
    t, bw = ya.shape
    d = wo.shape[1]
    tm = _tile(lv, tm_pref)
    cw = _tile(d, 512, LANE)
    row = lambda i: (i, 0)
    branch = lambda: pl.BlockSpec((tm, bw), row)
    gate = lambda k: pl.BlockSpec((tm, d), lambda i: (i, gate_col0 + k))
    const2 = lambda i: (0, 0)
    return pl.pallas_call(
        functools.partial(_merge_kernel, cw=cw),
        grid=(t // tm,),
        in_specs=[
            branch(), branch(), branch(),
            pl.BlockSpec((tm, bw), lambda i: (i, og_col)),
            pl.BlockSpec((1, bw), const2),
            branch(),
            gate(0), gate(1), gate(2),
            pl.BlockSpec(wb.shape, lambda i: (0, 0, 0)),
            pl.BlockSpec(wo.shape, const2),
            pl.BlockSpec((tm, d), row),
            pl.BlockSpec((1, d), const2),
            pl.BlockSpec((1, 1, d), lambda i: (i * tm // lv, 0, 0)),
        ],
        out_specs=pl.BlockSpec((tm, d), row),
        out_shape=jax.ShapeDtypeStruct((t, d), F32),
        scratch_shapes=[pltpu.VMEM((tm, bw), MXU_DTYPE), pltpu.VMEM((tm, d), MXU_DTYPE)],
        compiler_params=_cparams(("parallel",)),
        name="merge",
    )(ya, o_fwd, o_rev, u, hgrn_g.reshape(1, bw), yc, u, u, u, wb, wo, x, gn, gt)


def _ffn_kernel(x_ref, g_ref, sh_ref, sc_ref, w1_ref, w3_ref, w2_ref, gn_ref, gt_ref, *rest, nba):
    *o_refs, h_ref, acc_ref = rest
    j = pl.program_id(1)

    @pl.when(j == 0)
    def _():
        h_ref[...] = _norm_mod(x_ref[...], g_ref[...], sh_ref[0], sc_ref[0]).astype(h_ref.dtype)
        acc_ref[...] = jnp.zeros_like(acc_ref)

    h = h_ref[...]
    a = _dot(h, w1_ref[...])
    g = _dot(h, w3_ref[...])
    m = (a * _sigmoid(a) * g).astype(MXU_DTYPE)
    acc_ref[...] += _dot(m, w2_ref[...])

    @pl.when(j == pl.num_programs(1) - 1)
    def _():
        _row_out_store(o_refs, nba, x_ref[...] + gt_ref[0] * _rms(acc_ref[...], gn_ref[...]))


def _ffn_call(x, g, sh, sc, w13, w2, gn, gt, lv, split_rows, tm_pref=512, tf_pref=512):
    t, d = x.shape
    f = w2.shape[0]
    tm = _tile(lv, tm_pref)
    tf = _tile(f, tf_pref, LANE)
    nf = f // tf
    row = lambda i, j: (i, 0)
    const2 = lambda i, j: (0, 0)
    mod = lambda i, j: (i * tm // lv, 0, 0)
    out_specs, out_shape, nba = _row_out_specs(t, d, tm, split_rows, 2)
    return pl.pallas_call(
        functools.partial(_ffn_kernel, nba=nba),
        grid=(t // tm, nf),
        in_specs=[
            pl.BlockSpec((tm, d), row),
            pl.BlockSpec((1, d), const2),
            pl.BlockSpec((1, 1, d), mod),
            pl.BlockSpec((1, 1, d), mod),
            pl.BlockSpec((d, tf), lambda i, j: (0, j)),
            pl.BlockSpec((d, tf), lambda i, j: (0, nf + j)),
            pl.BlockSpec((tf, d), lambda i, j: (j, 0)),
            pl.BlockSpec((1, d), const2),
            pl.BlockSpec((1, 1, d), mod),
        ],
        out_specs=out_specs,
        out_shape=out_shape,
        scratch_shapes=[pltpu.VMEM((tm, d), MXU_DTYPE), pltpu.VMEM((tm, d), F32)],
        compiler_params=_cparams(("parallel", "arbitrary")),
        name="ffn",
    )(x, g, sh, sc, w13, w13, w2, gn, gt)


ROUTE_ROWS = 8
ROW_DMA_UNROLL = 8


def _pack_rows(x):
    half = x.shape[1] // 2
    bits = pltpu.bitcast(x.astype(jnp.bfloat16).astype(F32), jnp.uint32)
    return (bits[:, :half] >> 16) | bits[:, half:]


def _unpack_rows(p):
    lo = pltpu.bitcast(p << 16, F32)
    hi = pltpu.bitcast(p & jnp.uint32(0xFFFF0000), F32)
    return lo, hi


def _route_kernel(x_ref, g_ref, sh_ref, sc_ref, rw_ref, tri_ref, route_ref, rt_ref, cnt_ref, xs_hbm,
                  h_ref, run_ref, idx_ref, cntv_ref, cnts_ref, zrow_ref, sem_idx, sem, *, n_experts, cap, tb, tm):
    step = pl.program_id(0)
    last = pl.num_programs(0) - 1
    slot = lax.rem(step, 2)

    def row_copy(sl, k, dst_row):
        return pltpu.make_async_copy(h_ref.at[sl].at[pl.ds(k, 1)], xs_hbm.at[pl.ds(dst_row, 1)], sem.at[sl])

    def drain(sl):
        def body(k, carry):
            row_copy(sl, 0, 0).wait()
            row_copy(sl, 0, 0).wait()
            return carry

        lax.fori_loop(0, tb, body, 0, unroll=ROW_DMA_UNROLL)

    def compute():
        h = _norm_mod(x_ref[...], g_ref[...], sh_ref[0], sc_ref[0])
        h_hi = h.astype(MXU_DTYPE)
        h_lo = (h - h_hi.astype(F32)).astype(MXU_DTYPE)
        logits = _dot(h_hi, rw_ref[0]) + _dot(h_hi, rw_ref[1]) + _dot(h_lo, rw_ref[0])
        lane = lax.broadcasted_iota(jnp.int32, logits.shape, 1)
        neg_inf = -jnp.inf
        l1 = jnp.where(lane < n_experts, logits, neg_inf)
        m1 = jnp.max(l1, axis=-1, keepdims=True)
        i1 = jnp.min(jnp.where(l1 == m1, lane, LANE), axis=-1, keepdims=True)
        l2 = jnp.where(lane == i1, neg_inf, l1)
        m2 = jnp.max(l2, axis=-1, keepdims=True)
        i2 = jnp.min(jnp.where(l2 == m2, lane, LANE), axis=-1, keepdims=True)
        e = jnp.exp(m2 - m1)
        w0 = 1.0 / (1.0 + e)
        w1 = e * w0
        sel1 = lane == i1
        sel2 = lane == i2
        member = jnp.where(sel1, 1.0, jnp.where(sel2, 1.0, 0.0))
        rank = _dot(tri_ref[...], member.astype(MXU_DTYPE))
        base = run_ref[...] + rank + lane.astype(F32) * float(cap)
        pos0 = jnp.sum(jnp.where(sel1, base, 0.0), axis=-1, keepdims=True)
        pos1 = jnp.sum(jnp.where(sel2, base, 0.0), axis=-1, keepdims=True)
        route = jnp.where(lane == 0, pos0,
                          jnp.where(lane == 1, pos1, jnp.where(lane == 2, w0, jnp.where(lane == 3, w1, 0.0))))
        route_ref[...] = route
        rt_ref[...] = route.T[0:ROUTE_ROWS, :].astype(jnp.int32)
        run_ref[...] += jnp.sum(member, axis=0, keepdims=True)
        cnt_ref[...] = run_ref[...]
        h_ref[slot] = _pack_rows(h)

    @pl.when(step == 0)
    def _():
        run_ref[...] = jnp.zeros_like(run_ref)
        compute()

    @pl.when(step > 0)
    def _():
        @pl.when(step > 1)
        def _():
            drain(slot)

        for k in range(tb):
            row_copy(1 - slot, k, idx_ref[1 - slot, 0, k]).start()
            row_copy(1 - slot, k, idx_ref[1 - slot, 1, k]).start()
        compute()

    cp = pltpu.make_async_copy(rt_ref, idx_ref.at[slot], sem_idx)
    cp.start()
    cp.wait()

    @pl.when(step == last)
    def _():
        def issue(k, carry):
            row_copy(slot, k, idx_ref[slot, 0, k]).start()
            row_copy(slot, k, idx_ref[slot, 1, k]).start()
            return carry

        lax.fori_loop(0, tb, issue, 0, unroll=ROW_DMA_UNROLL)

        @pl.when(step > 0)
        def _():
            drain(1 - slot)

        drain(slot)
        zrow_ref[...] = jnp.zeros_like(zrow_ref)
        cntv_ref[...] = jnp.broadcast_to(run_ref[...], cntv_ref.shape).astype(jnp.int32)
        cpc = pltpu.make_async_copy(cntv_ref, cnts_ref, sem_idx)
        cpc.start()
        cpc.wait()

        def zero_copy(dst_row):
            return pltpu.make_async_copy(zrow_ref.at[pl.ds(0, 1)], xs_hbm.at[pl.ds(dst_row, 1)], sem.at[0])

        for e in range(n_experts):
            cnt = cnts_ref[0, e]
            end = ((cnt + tm - 1) // tm) * tm

            def zissue(r, carry, e=e):
                zero_copy(e * cap + r).start()
                return carry

            def zdrain(r, carry):
                zero_copy(0).wait()
                return carry

            lax.fori_loop(cnt, end, zissue, 0)
            lax.fori_loop(cnt, end, zdrain, 0)


def _route_call(x, g, sh, sc, router_w, lv, cap, tm, tb_pref=256):
    t, d = x.shape
    n_experts = router_w.shape[1]
    tb = _tile(lv, tb_pref, LANE)
    rw = jnp.zeros((d, LANE), F32).at[:, :n_experts].set(router_w)
    rw_hi = rw.astype(MXU_DTYPE)
    rw = jnp.stack([rw_hi, (rw - rw_hi.astype(F32)).astype(MXU_DTYPE)])
    idx = np.arange(tb)
    tri = jnp.asarray((idx[None, :] < idx[:, None]).astype(np.float32), MXU_DTYPE)
    row = lambda i: (i, 0)
    const2 = lambda i: (0, 0)
    mod = lambda i: (i * tb // lv, 0, 0)
    return pl.pallas_call(
        functools.partial(_route_kernel, n_experts=n_experts, cap=cap, tb=tb, tm=tm),
        grid=(t // tb,),
        in_specs=[
            pl.BlockSpec((tb, d), row),
            pl.BlockSpec((1, d), const2),
            pl.BlockSpec((1, 1, d), mod),
            pl.BlockSpec((1, 1, d), mod),
            pl.BlockSpec((2, d, LANE), lambda i: (0, 0, 0)),
            pl.BlockSpec((tb, tb), const2),
        ],
        out_specs=[
            pl.BlockSpec((tb, LANE), row),
            pl.BlockSpec((ROUTE_ROWS, tb), lambda i: (0, i)),
            pl.BlockSpec((1, LANE), const2),
            pl.BlockSpec(memory_space=pl.ANY),
        ],
        out_shape=[
            jax.ShapeDtypeStruct((t, LANE), F32),
            jax.ShapeDtypeStruct((ROUTE_ROWS, t), jnp.int32),
            jax.ShapeDtypeStruct((1, LANE), F32),
            jax.ShapeDtypeStruct((n_experts * cap, d // 2), jnp.uint32),
        ],
        scratch_shapes=[
            pltpu.VMEM((2, tb, d // 2), jnp.uint32),
            pltpu.VMEM((1, LANE), F32),
            pltpu.SMEM((2, ROUTE_ROWS, tb), jnp.int32),
            pltpu.VMEM((SUBLANE, LANE), jnp.int32),
            pltpu.SMEM((SUBLANE, LANE), jnp.int32),
            pltpu.VMEM((SUBLANE, d // 2), jnp.uint32),
            pltpu.SemaphoreType.DMA,
            pltpu.SemaphoreType.DMA((2,)),
        ],
        compiler_params=_cparams(("arbitrary",)),
        name="route",
    )(x, g, sh, sc, rw, tri)


def _gmm_kernel(be_ref, br_ref, bv_ref, xs_ref, w1_ref, w3_ref, w2_ref, y_ref, xb_ref, acc_ref):
    i = pl.program_id(0)
    j = pl.program_id(1)

    @pl.when(bv_ref[i] == 1)
    def _():
        @pl.when(j == 0)
        def _():
            half = xs_ref.shape[1]
            lo, hi = _unpack_rows(xs_ref[...])
            xb_ref[:, :half] = lo.astype(xb_ref.dtype)
            xb_ref[:, half:] = hi.astype(xb_ref.dtype)
            acc_ref[...] = jnp.zeros_like(acc_ref)

        xb = xb_ref[...]
        a = _dot(xb, w1_ref[0])
        g = _dot(xb, w3_ref[0])
        m = (a * _sigmoid(a) * g).astype(MXU_DTYPE)
        acc_ref[...] += _dot(m, w2_ref[0])

        @pl.when(j == pl.num_programs(1) - 1)
        def _():
            y_ref[...] = _pack_rows(acc_ref[...])


def _gmm_call(blk_e, blk_row, blk_valid, xs, w13, w2, tm, tf_pref=1024):
    n_rows, half = xs.shape
    d = 2 * half
    f = w2.shape[1]
    tf = _tile(f, tf_pref, LANE)
    nf = f // tf
    n_blk = blk_e.shape[0]

    def jsel(i, j, bv):
        return jnp.where(bv[i] == 1, j, nf - 1)

    grid_spec = pltpu.PrefetchScalarGridSpec(
        num_scalar_prefetch=3,
        grid=(n_blk, nf),
        in_specs=[
            pl.BlockSpec((tm, half), lambda i, j, be, br, bv: (br[i], 0)),
            pl.BlockSpec((1, d, tf), lambda i, j, be, br, bv: (be[i], 0, jsel(i, j, bv))),
            pl.BlockSpec((1, d, tf), lambda i, j, be, br, bv: (be[i], 0, nf + jsel(i, j, bv))),
            pl.BlockSpec((1, tf, d), lambda i, j, be, br, bv: (be[i], jsel(i, j, bv), 0)),
        ],
        out_specs=pl.BlockSpec((tm, half), lambda i, j, be, br, bv: (br[i], 0)),
        scratch_shapes=[pltpu.VMEM((tm, d), MXU_DTYPE), pltpu.VMEM((tm, d), F32)],
    )
    return pl.pallas_call(
        _gmm_kernel,
        grid_spec=grid_spec,
        out_shape=jax.ShapeDtypeStruct((n_rows, half), jnp.uint32),
        compiler_params=_cparams(("arbitrary", "arbitrary")),
        name="moe_gmm",
    )(blk_e, blk_row, blk_valid, xs, w13, w13, w2)


def _combine_kernel(rt_ref, rtn_ref, route_ref, x_ref, gn_ref, gt_ref, y_hbm, *rest, tb, nba):
    *o_refs, idx_ref, y0_ref, y1_ref, sem_idx, sem = rest
    step = pl.program_id(0)
    slot = lax.rem(step, 2)

    def row_copy(src_row, dst_ref, sl, k, s):
        return pltpu.make_async_copy(y_hbm.at[pl.ds(src_row, 1)], dst_ref.at[sl].at[pl.ds(k, 1)], sem.at[sl, s])

    def load_table(table_ref):
        cp = pltpu.make_async_copy(table_ref, idx_ref, sem_idx)
        cp.start()
        cp.wait()

    def drain(sl):
        def body(k, carry):
            row_copy(0, y0_ref, sl, k, 0).wait()
            row_copy(0, y1_ref, sl, k, 1).wait()
            return carry

        lax.fori_loop(0, tb, body, 0, unroll=ROW_DMA_UNROLL)

    @pl.when(step == 0)
    def _():
        load_table(rt_ref)

        def issue(k, carry):
            row_copy(idx_ref[0, k], y0_ref, 0, k, 0).start()
            row_copy(idx_ref[1, k], y1_ref, 0, k, 1).start()
            return carry

        lax.fori_loop(0, tb, issue, 0, unroll=ROW_DMA_UNROLL)

    load_table(rtn_ref)
    drain(slot)
    for k in range(tb):
        row_copy(idx_ref[0, k], y0_ref, 1 - slot, k, 0).start()
        row_copy(idx_ref[1, k], y1_ref, 1 - slot, k, 1).start()
    route = route_ref[...]
    w0 = route[:, 2:3]
    w1 = route[:, 3:4]
    lo0, hi0 = _unpack_rows(y0_ref[slot])
    lo1, hi1 = _unpack_rows(y1_ref[slot])
    y = jnp.concatenate([w0 * lo0 + w1 * lo1, w0 * hi0 + w1 * hi1], axis=1)
    _row_out_store(o_refs, nba, x_ref[...] + gt_ref[0] * _rms(y, gn_ref[...]))

    @pl.when(step == pl.num_programs(0) - 1)
    def _():
        drain(1 - slot)


def _combine_call(rt, route, x, gn, gt, y, lv, split_rows, tb_pref=256):
    t, d = x.shape
    tb = _tile(lv, tb_pref, LANE)
    nblk = t // tb
    row = lambda i: (i, 0)
    out_specs, out_shape, nba = _row_out_specs(t, d, tb, split_rows, 1)
    return pl.pallas_call(
        functools.partial(_combine_kernel, tb=tb, nba=nba),
        grid=(nblk,),
        in_specs=[
            pl.BlockSpec((ROUTE_ROWS, tb), lambda i: (0, i)),
            pl.BlockSpec((ROUTE_ROWS, tb), lambda i: (0, jnp.minimum(i + 1, nblk - 1))),
            pl.BlockSpec((tb, LANE), row),
            pl.BlockSpec((tb, d), row),
            pl.BlockSpec((1, d), lambda i: (0, 0)),
            pl.BlockSpec((1, 1, d), lambda i: (i * tb // lv, 0, 0)),
            pl.BlockSpec(memory_space=pl.ANY),
        ],
        out_specs=out_specs,
        out_shape=out_shape,
        scratch_shapes=[
            pltpu.SMEM((ROUTE_ROWS, tb), jnp.int32),
            pltpu.VMEM((2, tb, d // 2), jnp.uint32),
            pltpu.VMEM((2, tb, d // 2), jnp.uint32),
            pltpu.SemaphoreType.DMA,
            pltpu.SemaphoreType.DMA((2, 2)),
        ],
        compiler_params=_cparams(("arbitrary",)),
        name="moe_combine",
    )(rt, rt, route, x, gn, gt, y)


def _moe(x, g, sh, sc, router_w, w13, w2, gn, gt, lv, split_rows=None, tm_pref=512):
    t, d = x.shape
    n_experts = router_w.shape[1]
    tm = _tile(t, tm_pref, LANE)
    cap = -(-t // tm) * tm
    route, rt, cnt, xs = _route_call(x, g, sh, sc, router_w, lv, cap, tm)
    counts = cnt[0, :n_experts].astype(jnp.int32)
    nb = (counts + tm - 1) // tm
    ends = jnp.cumsum(nb)
    n_blk = TOP_K * t // tm + n_experts
    bi = jnp.arange(n_blk, dtype=jnp.int32)
    valid = bi < ends[-1]
    bi_c = jnp.minimum(bi, ends[-1] - 1)
    be = jnp.minimum(jnp.searchsorted(ends, bi_c, side="right"), n_experts - 1).astype(jnp.int32)
    br = be * (cap // tm) + (bi_c - (ends - nb)[be])
    y = _gmm_call(be, br.astype(jnp.int32), valid.astype(jnp.int32), xs, w13, w2, tm)
    return _combine_call(rt, route, x, gn, gt, y, lv, split_rows)


def kernel(x_prompt, x_sample, c_prompt, c_sample, ada_w, ada_b, norm_g, w_in, na_rpb, hgrn_lb, hgrn_norm_g,
           conv_w, conv_b, conv_ln_g, conv_ln_b, w_branch, w_out, ffn_w13, ffn_w2, router_w, moe_w13, moe_w2):
    bp, lp, d = x_prompt.shape
    bs, ls, _ = x_sample.shape
    depth = ada_w.shape[0]
    bw = w_branch.shape[2]
    tp = bp * lp
    t = tp + bs * ls
    lv = math.gcd(lp, ls)
    n_vseq = t // lv
    assert 8 * bw % d == 0 and d % LANE == 0 and bw % LANE == 0

    x = jnp.concatenate([x_prompt.reshape(tp, d), x_sample.reshape(bs * ls, d)], axis=0)

    nb = bp + bs
    rows = -(-nb // SUBLANE) * SUBLANE
    c_all = jnp.zeros((rows, d), F32).at[:nb].set(jnp.concatenate([c_prompt, c_sample], axis=0))
    mod = _ada(c_all, ada_w, ada_b)
    starts = np.arange(n_vseq) * lv
    vb = np.where(starts < tp, starts // lp, bp + (starts - tp) // ls)
    mod_v = mod[:, vb, :].reshape(depth, n_vseq, 1, 6, d)
    part = lambda l, k: mod_v[l, :, :, k, :]

    sm = jax.nn.softmax(hgrn_lb.astype(F32), axis=0)
    lb_all = jnp.clip(jnp.cumsum(sm, axis=0) - sm[0], 0.0, 1.0)

    cb = lambda k: slice(k * bw, (k + 1) * bw)
    order = [0, 1, 2, 3, 4, 7, 8, 9]
    conv_tb = _tile(lv, 512)
    blk_start = np.arange(t // conv_tb) * conv_tb
    seq_pos = np.where(blk_start < tp, blk_start % lp, (blk_start - tp) % ls)
    seq_len_of = np.where(blk_start < tp, lp, ls)
    seq_starts = jnp.asarray((seq_pos == 0).astype(np.int32))
    seq_ends = jnp.asarray((seq_pos + conv_tb == seq_len_of).astype(np.int32))

    for l in range(depth):
        wl = w_in[l]
        w_all = jnp.concatenate([wl[:, cb(k)] for k in order] + [wl[:, 10 * bw:], wl[:, 5 * bw:7 * bw]],
                                axis=1).astype(MXU_DTYPE)
        u, uf = _nm_matmul(x, norm_g[l, 0].reshape(1, d), part(l, 0), part(l, 1), w_all, w_all.shape[1] - 2 * bw, lv)

        bias_tab = _na_bias_table(na_rpb[l])
        ya = _na_call(u, bias_tab, None, 0, bp, lp)
        ya = _na_call(u, bias_tab, ya, tp, bs, ls)

        o_fwd, o_rev = _hgrn_call(u, uf, lb_all[l], ((0, bp, lp), (tp, bs, ls)), (3, 4, 0, 1))

        yc = _conv_call(u, conv_w[l], conv_b[l], conv_ln_g[l], conv_ln_b[l], seq_starts, seq_ends, 6, 7, conv_tb)

        x = _merge_call(ya, o_fwd, o_rev, yc, u, 5, 8 * bw // d, hgrn_norm_g[l], w_branch[l].astype(MXU_DTYPE),
                        w_out[l].astype(MXU_DTYPE), x, norm_g[l, 1].reshape(1, d), part(l, 2), lv)

        g2 = norm_g[l, 2].reshape(1, d)
        g3 = norm_g[l, 3].reshape(1, d)
        split_rows = tp if l == depth - 1 else None
        if l % 2 == 0:
            x = _ffn_call(x, g2, part(l, 3), part(l, 4), ffn_w13[l // 2].astype(MXU_DTYPE),
                          ffn_w2[l // 2].astype(MXU_DTYPE), g3, part(l, 5), lv, split_rows)
        else:
            x = _moe(x, g2, part(l, 3), part(l, 4), router_w[l // 2], moe_w13[l // 2].astype(MXU_DTYPE),
                     moe_w2[l // 2].astype(MXU_DTYPE), g3, part(l, 5), lv, split_rows)

    y_prompt, y_sample = x
    return (y_prompt.reshape(bp, lp, d), y_sample.reshape(bs, ls, d))
```

```python
import functools
import math

import numpy as np
import jax
import jax.numpy as jnp
from jax import lax
from jax.experimental import pallas as pl
from jax.experimental.pallas import tpu as pltpu

GRID_W = 64
NA_HEAD_DIM = 64
NA_ROWS = 8
NA_COLS = 16
HGRN_DK = 128
TOP_K = 2
EPS = 1e-6
F_MIN = 1e-30
NEG_BIG = -1e30

LANE = 128
SUBLANE = 8
VMEM_LIMIT_BYTES = 56 * 1024 * 1024

MXU_DTYPE = jnp.bfloat16
ACT_DTYPE = jnp.bfloat16

HGRN_CHUNK = 128
HGRN_CHUNKS_PER_STEP = 4
HGRN_MATMUL_LEVELS = 3
NA_ROW_UNROLL = 4
NA_QGROUP = 16
F32 = jnp.float32


def _cparams(sem, vmem=VMEM_LIMIT_BYTES):
    return pltpu.CompilerParams(dimension_semantics=sem, vmem_limit_bytes=vmem)


def _sigmoid(x):
    return 1.0 / (1.0 + jnp.exp(-x))


def _neg_abs(x):
    return pltpu.bitcast(pltpu.bitcast(x, jnp.uint32) | jnp.uint32(0x80000000), F32)


def _dot(a, b):
    return jnp.dot(a, b, preferred_element_type=F32)


def _dot_nt(a, b):
    return lax.dot_general(a, b, (((1,), (1,)), ((), ())), preferred_element_type=F32)


def _dot_tn(a, b):
    return lax.dot_general(a, b, (((0,), (0,)), ((), ())), preferred_element_type=F32)


def _rms(x, g):
    return x * lax.rsqrt(jnp.mean(x * x, axis=-1, keepdims=True) + EPS) * g


def _norm_mod(x, g, sh, sc):
    return _rms(x, g) * (1.0 + sc) + sh


def _tile(n, pref, mult=SUBLANE):
    if n <= pref:
        return n
    t = (pref // mult) * mult
    while t >= mult:
        if n % t == 0:
            return t
        t -= mult
    return n


def _ada_kernel(c_ref, w_ref, b_ref, o_ref):
    c = c_ref[...]
    cs = (c * _sigmoid(c)).astype(MXU_DTYPE)
    o_ref[0] = _dot(cs, w_ref[0].astype(MXU_DTYPE)) + b_ref[0]


def _ada(c_all, ada_w, ada_b):
    depth, d, n = ada_w.shape
    rows = c_all.shape[0]
    tn = _tile(n, 1024, LANE)
    return pl.pallas_call(
        _ada_kernel,
        grid=(depth, n // tn),
        in_specs=[
            pl.BlockSpec((rows, d), lambda l, j: (0, 0)),
            pl.BlockSpec((1, d, tn), lambda l, j: (l, 0, j)),
            pl.BlockSpec((1, 1, tn), lambda l, j: (l, 0, j)),
        ],
        out_specs=pl.BlockSpec((1, rows, tn), lambda l, j: (l, 0, j)),
        out_shape=jax.ShapeDtypeStruct((depth, rows, n), F32),
        compiler_params=_cparams(("parallel", "parallel")),
        name="ada",
    )(c_all, ada_w, ada_b.reshape(depth, 1, n))


def _row_out_specs(t, d, tm, split_rows, rank):
    pad = (lambda f: (lambda i: (f(i), 0))) if rank == 1 else (lambda f: (lambda i, j: (f(i), 0)))
    if split_rows is None:
        return pl.BlockSpec((tm, d), pad(lambda i: i)), jax.ShapeDtypeStruct((t, d), F32), None
    nba, nbb = split_rows // tm, (t - split_rows) // tm
    specs = [pl.BlockSpec((tm, d), pad(lambda i: jnp.minimum(i, nba - 1))),
             pl.BlockSpec((tm, d), pad(lambda i: jnp.clip(i - nba, 0, nbb - 1)))]
    shapes = [jax.ShapeDtypeStruct((split_rows, d), F32), jax.ShapeDtypeStruct((t - split_rows, d), F32)]
    return specs, shapes, nba


def _row_out_store(o_refs, nba, val):
    if nba is None:
        o_refs[0][...] = val
        return

    @pl.when(pl.program_id(0) < nba)
    def _():
        o_refs[0][...] = val

    @pl.when(pl.program_id(0) >= nba)
    def _():
        o_refs[1][...] = val


def _nm_matmul_kernel(x_ref, g_ref, sh_ref, sc_ref, w_ref, oa_ref, of_ref, h_ref, *, nja):
    j = pl.program_id(1)

    @pl.when(j == 0)
    def _():
        h_ref[...] = _norm_mod(x_ref[...], g_ref[...], sh_ref[0], sc_ref[0]).astype(h_ref.dtype)

    r = _dot(h_ref[...], w_ref[...])

    @pl.when(j < nja)
    def _():
        oa_ref[...] = r.astype(oa_ref.dtype)

    @pl.when(j >= nja)
    def _():
        of_ref[...] = r


def _nm_matmul(x, g, sh, sc, w, n_a, lv, tm_pref=1024, tn_pref=1024):
    t, d = x.shape
    n = w.shape[1]
    n_f = n - n_a
    tm = _tile(lv, tm_pref)
    tn = _tile(math.gcd(n_a, n_f), tn_pref, LANE)
    nja = n_a // tn
    return pl.pallas_call(
        functools.partial(_nm_matmul_kernel, nja=nja),
        grid=(t // tm, n // tn),
        in_specs=[
            pl.BlockSpec((tm, d), lambda i, j: (i, 0)),
            pl.BlockSpec((1, d), lambda i, j: (0, 0)),
            pl.BlockSpec((1, 1, d), lambda i, j: (i * tm // lv, 0, 0)),
            pl.BlockSpec((1, 1, d), lambda i, j: (i * tm // lv, 0, 0)),
            pl.BlockSpec((d, tn), lambda i, j: (0, j)),
        ],
        out_specs=[
            pl.BlockSpec((tm, tn), lambda i, j: (i, jnp.minimum(j, nja - 1))),
            pl.BlockSpec((tm, tn), lambda i, j: (i, jnp.maximum(j - nja, 0))),
        ],
        out_shape=[jax.ShapeDtypeStruct((t, n_a), ACT_DTYPE), jax.ShapeDtypeStruct((t, n_f), F32)],
        scratch_shapes=[pltpu.VMEM((tm, d), MXU_DTYPE)],
        compiler_params=_cparams(("parallel", "arbitrary")),
        name="nm_matmul",
    )(x, g, sh, sc, w)


def _na_bias_table(rpb):
    cols = np.arange(GRID_W)
    col_start = np.clip(cols - NA_COLS // 2, 0, GRID_W - NA_COLS)
    in_win = (cols[None, :] >= col_start[:, None]) & (cols[None, :] < col_start[:, None] + NA_COLS)
    dc = cols[None, :] - cols[:, None] + (NA_COLS - 1)
    onehot = ((dc[None] == np.arange(2 * NA_COLS - 1)[:, None, None]) & in_win[None]).astype(np.float32)
    full = jnp.einsum("hrd,dqk->hrqk", rpb.astype(F32), jnp.asarray(onehot), precision=lax.Precision.HIGHEST)
    tabs = []
    for delta in range(NA_ROWS):
        rows = full[:, NA_ROWS - 1 - delta:2 * NA_ROWS - 1 - delta]
        tabs.append(rows.transpose(0, 2, 1, 3).reshape(rpb.shape[0], GRID_W, NA_ROWS * GRID_W))
    mask = jnp.asarray(np.tile(in_win, (1, NA_ROWS)))
    return jnp.where(mask[None, None], jnp.stack(tabs, axis=0), NEG_BIG)


def _na_kernel(q_ref, k_ref, v_ref, bias_ref, o_ref, s0_ref, s1_ref, p0_ref, p1_ref, *, rows):
    s_refs = (s0_ref, s1_ref)
    p_refs = (p0_ref, p1_ref)
    win = NA_ROWS * GRID_W
    lane = lax.broadcasted_iota(jnp.int32, (1, LANE), 1)
    head_masks = [lane < NA_HEAD_DIM, lane >= NA_HEAD_DIM]
    scale = NA_HEAD_DIM ** -0.5

    def row_start(r):
        return jnp.clip(r - NA_ROWS // 2, 0, rows - NA_ROWS)

    def scores(g, slot):
        for u in range(NA_ROW_UNROLL):
            r = g * NA_ROW_UNROLL + u
            q2 = q_ref[pl.ds(pl.multiple_of(r * GRID_W, GRID_W), GRID_W), :]
            q2 = q2 * jnp.asarray(scale, q2.dtype)
            kw = k_ref[pl.ds(pl.multiple_of(row_start(r) * GRID_W, GRID_W), win), :]
            qq = jnp.concatenate([jnp.where(head_masks[h], q2, jnp.zeros_like(q2)) for h in range(2)], axis=0)
            s_refs[slot][u] = _dot_nt(qq, kw)

    def finish(g, slot):
        rden = []
        for u in range(NA_ROW_UNROLL):
            r = g * NA_ROW_UNROLL + u
            delta = r - row_start(r)
            parts = []
            for qg in range(2 * GRID_W // NA_QGROUP):
                h, gq = divmod(qg * NA_QGROUP, GRID_W)
                rows_g = pl.ds(qg * NA_QGROUP, NA_QGROUP)
                s = s_refs[slot][u, rows_g, :] + bias_ref[delta, h, pl.ds(gq, NA_QGROUP), :]
                m = jnp.max(s, axis=-1, keepdims=True)
                e = jnp.exp(s - m)
                parts.append(1.0 / jnp.sum(e, axis=-1, keepdims=True))
                p_refs[slot][u, rows_g, :] = e.astype(p0_ref.dtype)
            rden.append(jnp.concatenate(parts, axis=0))
        for u in range(NA_ROW_UNROLL):
            r = g * NA_ROW_UNROLL + u
            vw = v_ref[pl.ds(pl.multiple_of(row_start(r) * GRID_W, GRID_W), win), :]
            o2 = _dot(p_refs[slot][u], vw) * rden[u]
            out = jnp.where(head_masks[0], o2[:GRID_W], o2[GRID_W:])
            o_ref[pl.ds(pl.multiple_of(r * GRID_W, GRID_W), GRID_W), :] = out.astype(o_ref.dtype)

    ngroups = rows // NA_ROW_UNROLL
    scores(0, 0)

    def body(i, carry):
        g = 2 * i
        scores(g + 1, 1)
        finish(g, 0)
        scores(jnp.minimum(g + 2, ngroups - 1), 0)
        finish(g + 1, 1)
        return carry

    lax.fori_loop(0, ngroups // 2, body, 0)


def _na_call(u, bias_tab, y_prev, tok_off, nseq, seq_len):
    t = u.shape[0]
    bw = bias_tab.shape[1] * NA_HEAD_DIM
    npair = bw // LANE
    rows = seq_len // GRID_W
    assert rows >= NA_ROWS and rows % (2 * NA_ROW_UNROLL) == 0 and seq_len % GRID_W == 0 and tok_off % seq_len == 0
    s0 = tok_off // seq_len
    kern = functools.partial(_na_kernel, rows=rows)
    in_specs = [
        pl.BlockSpec((seq_len, LANE), lambda b, p: (s0 + b, p)),
        pl.BlockSpec((seq_len, LANE), lambda b, p: (s0 + b, npair + p)),
        pl.BlockSpec((seq_len, LANE), lambda b, p: (s0 + b, 2 * npair + p)),
        pl.BlockSpec((NA_ROWS, 2, GRID_W, NA_ROWS * GRID_W), lambda b, p: (0, p, 0, 0)),
    ]
    args = [u, u, u, bias_tab]
    aliases = {}
    if y_prev is not None:
        in_specs.append(pl.BlockSpec(memory_space=pl.ANY))
        args.append(y_prev)
        aliases = {4: 0}
        kern_fn = lambda q, k, v, b, _prev, *rest: kern(q, k, v, b, *rest)
    else:
        kern_fn = kern
    return pl.pallas_call(
        kern_fn,
        grid=(nseq, npair),
        in_specs=in_specs,
        out_specs=pl.BlockSpec((seq_len, LANE), lambda b, p: (s0 + b, p)),
        out_shape=jax.ShapeDtypeStruct((t, bw), ACT_DTYPE),
        scratch_shapes=[
            pltpu.VMEM((NA_ROW_UNROLL, 2 * GRID_W, NA_ROWS * GRID_W), F32),
            pltpu.VMEM((NA_ROW_UNROLL, 2 * GRID_W, NA_ROWS * GRID_W), F32),
            pltpu.VMEM((NA_ROW_UNROLL, 2 * GRID_W, NA_ROWS * GRID_W), MXU_DTYPE),
            pltpu.VMEM((NA_ROW_UNROLL, 2 * GRID_W, NA_ROWS * GRID_W), MXU_DTYPE),
        ],
        input_output_aliases=aliases,
        compiler_params=_cparams(("parallel", "parallel")),
        name="na",
    )(*args)


def _hgrn_consts(c, reverse):
    nlev = int(math.log2(c))
    nmat = min(HGRN_MATMUL_LEVELS, nlev)
    idx = np.arange(c)
    tri = (idx[None, :] <= idx[:, None]).astype(np.float32)
    mats = [tri]
    for l in range(nmat):
        m = 1 << l
        ref = (idx // (2 * m)) * (2 * m) + m - 1
        mats.append(tri[ref])
    mall = np.concatenate(mats, axis=0)
    x = idx[:, None] ^ idx[None, :]
    lvl = np.where(x > 0, np.floor(np.log2(np.maximum(x, 1))), -1).astype(np.int32)
    lvl = np.where(idx[None, :] > idx[:, None], -2, lvl)
    if reverse:
        mall = mall.reshape(nmat + 1, c, c)[:, ::-1, ::-1].reshape((nmat + 1) * c, c)
        lvl = lvl[::-1, ::-1]
    return np.ascontiguousarray(np.tile(mall, (1, 3))), np.ascontiguousarray(lvl), nlev


def _hgrn_direction(q_ref, i_ref, f_ref, rows, lb, mall, lvl, st_ref, d, *, c, nlev, nheads, reverse):
    fg = jnp.maximum(lb + (1.0 - lb) * _sigmoid(f_ref[rows, :]), F_MIN)
    lg = jnp.log2(fg)
    kk = 1.0 - fg
    hi = lg.astype(MXU_DTYPE)
    r1 = lg - hi.astype(F32)
    mid = r1.astype(MXU_DTYPE)
    lo = (r1 - mid.astype(F32)).astype(MXU_DTYPE)
    ball = _dot(mall, jnp.concatenate([hi, mid, lo], axis=0))
    nmat = min(HGRN_MATMUL_LEVELS, nlev)
    b_all = ball[0:c]
    bref = [ball[(l + 1) * c:(l + 2) * c] for l in range(nmat)]
    for l in range(nmat, nlev):
        m = 1 << l
        blocks = []
        for g0 in range(0, c, 2 * m):
            r = g0 + m if reverse else g0 + m - 1
            blocks.append(jnp.broadcast_to(b_all[r:r + 1, :], (2 * m, b_all.shape[1])))
        bref.append(blocks[0] if len(blocks) == 1 else jnp.concatenate(blocks, axis=0))
    q = q_ref[rows, :].astype(F32) * (HGRN_DK ** -0.5)
    v = i_ref[rows, :].astype(MXU_DTYPE)
    tot_row = 0 if reverse else c - 1
    outs = []
    for h in range(nheads):
        sl = slice(h * HGRN_DK, (h + 1) * HGRN_DK)
        bh = b_all[:, sl]
        qh = q[:, sl]
        kh = kk[:, sl]
        vh = v[:, sl]
        qb = qh.astype(MXU_DTYPE)
        kb = kh.astype(MXU_DTYPE)
        a = jnp.where(lvl == -1, _dot_nt(qb, kb), 0.0)
        for l in range(nlev):
            e = jnp.exp2(_neg_abs(bh - bref[l][:, sl])).astype(MXU_DTYPE)
            a = jnp.where(lvl == l, _dot_nt(qb * e, kb * e), a)
        o = _dot(a.astype(MXU_DTYPE), vh)
        st = st_ref[d * nheads + h]
        o = o + _dot_nt((qh * jnp.exp2(bh)).astype(MXU_DTYPE), st.astype(MXU_DTYPE))
        btot = bh[tot_row:tot_row + 1, :]
        kd = (kh * jnp.exp2(btot - bh)).astype(MXU_DTYPE)
        st_ref[d * nheads + h] = st * jnp.exp2(btot) + _dot_tn(vh, kd)
        outs.append(o)
    return outs


def _hgrn_kernel(fblk_ref, rblk_ref, first_ref, qf_ref, if_ref, ff_ref, qr_ref, ir_ref, fr_ref, lb_ref, mall_ref, lvl_ref,
                 of_ref, or_ref, st_ref, *, c, nlev, nheads):
    @pl.when(first_ref[pl.program_id(0)] == 1)
    def _():
        st_ref[...] = jnp.zeros_like(st_ref)

    dirs = ((qf_ref, if_ref, ff_ref, of_ref, False), (qr_ref, ir_ref, fr_ref, or_ref, True))
    nsub = qf_ref.shape[0] // c
    for k in range(nsub):
        for d, (q_ref, i_ref, f_ref, o_ref, reverse) in enumerate(dirs):
            rows = pl.ds((nsub - 1 - k if reverse else k) * c, c)
            outs = _hgrn_direction(q_ref, i_ref, f_ref, rows, lb_ref[d:d + 1, :], mall_ref[d], lvl_ref[d], st_ref, d,
                                   c=c, nlev=nlev, nheads=nheads, reverse=reverse)
            for h in range(nheads):
                o_ref[rows, h * HGRN_DK:(h + 1) * HGRN_DK] = outs[h]


def _hgrn_call(u, uf, lb, seqs, cols):
    t = u.shape[0]
    bw = lb.shape[-1]
    nheads = bw // HGRN_DK
    c = min([HGRN_CHUNK] + [n for _, _, n in seqs])
    blk = c * HGRN_CHUNKS_PER_STEP
    assert all(n % blk == 0 and off % blk == 0 for off, _, n in seqs)
    fblk, rblk, first = [], [], []
    for off, cnt, n in seqs:
        nck = n // blk
        for b in range(cnt):
            base = (off + b * n) // blk
            fblk += [base + k for k in range(nck)]
            rblk += [base + nck - 1 - k for k in range(nck)]
            first += [1] + [0] * (nck - 1)
    consts = [_hgrn_consts(c, rev) for rev in (False, True)]
    nlev = consts[0][2]
    mall = jnp.asarray(np.stack([cc[0] for cc in consts]), MXU_DTYPE)
    lvl = jnp.asarray(np.stack([cc[1] for cc in consts]))
    cq, ci, cff, cfb = cols
    fwd = lambda col: pl.BlockSpec((blk, bw), lambda s, fb, rb, fr: (fb[s], col))
    rev = lambda col: pl.BlockSpec((blk, bw), lambda s, fb, rb, fr: (rb[s], col))
    const2 = lambda s, fb, rb, fr: (0, 0)
    const3 = lambda s, fb, rb, fr: (0, 0, 0)
    grid_spec = pltpu.PrefetchScalarGridSpec(
        num_scalar_prefetch=3,
        grid=(len(fblk),),
        in_specs=[fwd(cq), fwd(ci), fwd(cff), rev(cq), rev(ci), rev(cfb),
                  pl.BlockSpec((2, bw), const2), pl.BlockSpec(mall.shape, const3), pl.BlockSpec(lvl.shape, const3)],
        out_specs=[fwd(0), rev(0)],
        scratch_shapes=[pltpu.VMEM((2 * nheads, HGRN_DK, HGRN_DK), F32)],
    )
    tab = lambda z: jnp.asarray(np.asarray(z, np.int32))
    return pl.pallas_call(
        functools.partial(_hgrn_kernel, c=c, nlev=nlev, nheads=nheads),
        grid_spec=grid_spec,
        out_shape=[jax.ShapeDtypeStruct((t, bw), F32), jax.ShapeDtypeStruct((t, bw), F32)],
        compiler_params=_cparams(("arbitrary",)),
        name="hgrn",
    )(tab(fblk), tab(rblk), tab(first), u, u, uf, u, u, uf, lb, mall, lvl)


CONV_HALO = 16
CONV_ROWS = 32


def _conv_kernel(first_ref, last_ref, a_ref, g_ref, ap_ref, gp_ref, an_ref, gn_ref,
                 w_ref, b_ref, lg_ref, lb_ref, o_ref, hbuf, *, tb, width):
    i = pl.program_id(0)
    pad = width // 2

    def glu(a, g):
        return a.astype(F32) * _sigmoid(g.astype(F32))

    hbuf[0, CONV_HALO:CONV_HALO + tb, :] = glu(a_ref[...], g_ref[...])
    hp = glu(ap_ref[...], gp_ref[...])
    hbuf[0, 0:CONV_HALO, :] = jnp.where(first_ref[i] == 1, 0.0, hp)
    hn = glu(an_ref[...], gn_ref[...])
    hbuf[0, CONV_HALO + tb:2 * CONV_HALO + tb, :] = jnp.where(last_ref[i] == 1, 0.0, hn)
    span = tb + 2 * CONV_HALO - SUBLANE
    for k in range(1, SUBLANE):
        hbuf[k, 0:span, :] = hbuf[0, k:k + span, :]
    w = w_ref[...]
    for r0 in range(0, tb, CONV_ROWS):
        acc = jnp.zeros((CONV_ROWS, w.shape[1]), F32)
        for j in range(width):
            s = CONV_HALO + r0 + j - pad
            k = s % SUBLANE
            acc = acc + w[j:j + 1, :] * hbuf[k, s - k:s - k + CONV_ROWS, :]
        h = acc + b_ref[...]
        mu = jnp.mean(h, axis=-1, keepdims=True)
        hc = h - mu
        var = jnp.mean(hc * hc, axis=-1, keepdims=True)
        y = hc * lax.rsqrt(var + EPS) * lg_ref[...] + lb_ref[...]
        o_ref[r0:r0 + CONV_ROWS, :] = (y * _sigmoid(y)).astype(o_ref.dtype)


def _conv_call(u, conv_w, conv_b, ln_g, ln_b, seq_starts, seq_ends, col_a, col_g, tb):
    t = u.shape[0]
    width, ch = conv_w.shape
    assert width // 2 < CONV_HALO and tb % CONV_ROWS == 0
    nblk = t // tb
    hb = tb // CONV_HALO
    nh = t // CONV_HALO
    kern = functools.partial(_conv_kernel, tb=tb, width=width)
    cur = lambda col: pl.BlockSpec((tb, ch), lambda i, f, l: (i, col))
    prev = lambda col: pl.BlockSpec((CONV_HALO, ch), lambda i, f, l: (jnp.maximum(i * hb - 1, 0), col))
    nxt = lambda col: pl.BlockSpec((CONV_HALO, ch), lambda i, f, l: (jnp.minimum((i + 1) * hb, nh - 1), col))
    vec = lambda: pl.BlockSpec((1, ch), lambda i, f, l: (0, 0))
    grid_spec = pltpu.PrefetchScalarGridSpec(
        num_scalar_prefetch=2,
        grid=(nblk,),
        in_specs=[cur(col_a), cur(col_g), prev(col_a), prev(col_g), nxt(col_a), nxt(col_g),
                  pl.BlockSpec((width, ch), lambda i, f, l: (0, 0)), vec(), vec(), vec()],
        out_specs=pl.BlockSpec((tb, ch), lambda i, f, l: (i, 0)),
        scratch_shapes=[pltpu.VMEM((SUBLANE, tb + 2 * CONV_HALO, ch), F32)],
    )
    return pl.pallas_call(
        kern,
        grid_spec=grid_spec,
        out_shape=jax.ShapeDtypeStruct((t, ch), ACT_DTYPE),
        compiler_params=_cparams(("parallel",)),
        name="conv",
    )(seq_starts, seq_ends, u, u, u, u, u, u, conv_w, conv_b.reshape(1, ch), ln_g.reshape(1, ch), ln_b.reshape(1, ch))


def _merge_kernel(ya_ref, of_ref, or_ref, og_ref, ng_ref, yc_ref, g0_ref, g1_ref, g2_ref, wb_ref, wo_ref, x_ref,
                  gn_ref, gt_ref, o_ref, yb_ref, m_ref, *, cw):
    d = o_ref.shape[1]
    o = of_ref[...] + or_ref[...]
    og = og_ref[...].astype(F32)
    gate = ng_ref[...] * (og * _sigmoid(og))
    for h in range(o.shape[1] // HGRN_DK):
        sl = slice(h * HGRN_DK, (h + 1) * HGRN_DK)
        oh = o[:, sl]
        oh = oh * lax.rsqrt(jnp.mean(oh * oh, axis=-1, keepdims=True) + EPS)
        yb_ref[:, sl] = (oh * gate[:, sl]).astype(yb_ref.dtype)
    ys = (ya_ref, yb_ref, yc_ref)
    gs = (g0_ref, g1_ref, g2_ref)
    for cb in range(d // cw):
        cs = slice(cb * cw, (cb + 1) * cw)
        acc = None
        for i in range(3):
            term = _sigmoid(gs[i][:, cs].astype(F32)) * _dot(ys[i][...], wb_ref[i, :, cs])
            acc = term if acc is None else acc + term
        m_ref[:, cs] = acc.astype(m_ref.dtype)
    y = _dot(m_ref[...], wo_ref[...])
    o_ref[...] = x_ref[...] + gt_ref[0] * _rms(y, gn_ref[...])


def _merge_call(ya, o_fwd, o_rev, yc, u, og_col, gate_col0, hgrn_g, wb, wo, x, gn, gt, lv, tm_pref=256):
    t, bw = ya.shape
    d = wo.shape[1]
    tm = _tile(lv, tm_pref)
    cw = _tile(d, 512, LANE)
    row = lambda i: (i, 0)
    branch = lambda: pl.BlockSpec((tm, bw), row)
    gate = lambda k: pl.BlockSpec((tm, d), lambda i: (i, gate_col0 + k))
    const2 = lambda i: (0, 0)
    return pl.pallas_call(
        functools.partial(_merge_kernel, cw=cw),
        grid=(t // tm,),
        in_specs=[
            branch(), branch(), branch(),
            pl.BlockSpec((tm, bw), lambda i: (i, og_col)),
            pl.BlockSpec((1, bw), const2),
            branch(),
            gate(0), gate(1), gate(2),
            pl.BlockSpec(wb.shape, lambda i: (0, 0, 0)),
            pl.BlockSpec(wo.shape, const2),
            pl.BlockSpec((tm, d), row),
            pl.BlockSpec((1, d), const2),
            pl.BlockSpec((1, 1, d), lambda i: (i * tm // lv, 0, 0)),
        ],
        out_specs=pl.BlockSpec((tm, d), row),
        out_shape=jax.ShapeDtypeStruct((t, d), F32),
        scratch_shapes=[pltpu.VMEM((tm, bw), MXU_DTYPE), pltpu.VMEM((tm, d), MXU_DTYPE)],
        compiler_params=_cparams(("parallel",)),
        name="merge",
    )(ya, o_fwd, o_rev, u, hgrn_g.reshape(1, bw), yc, u, u, u, wb, wo, x, gn, gt)


def _ffn_kernel(x_ref, g_ref, sh_ref, sc_ref, w1_ref, w3_ref, w2_ref, gn_ref, gt_ref, *rest, nba):
    *o_refs, h_ref, acc_ref = rest
    j = pl.program_id(1)

    @pl.when(j == 0)
    def _():
        h_ref[...] = _norm_mod(x_ref[...], g_ref[...], sh_ref[0], sc_ref[0]).astype(h_ref.dtype)
        acc_ref[...] = jnp.zeros_like(acc_ref)

    h = h_ref[...]
    a = _dot(h, w1_ref[...])
    g = _dot(h, w3_ref[...])
    m = (a * _sigmoid(a) * g).astype(MXU_DTYPE)
    acc_ref[...] += _dot(m, w2_ref[...])

    @pl.when(j == pl.num_programs(1) - 1)
    def _():
        _row_out_store(o_refs, nba, x_ref[...] + gt_ref[0] * _rms(acc_ref[...], gn_ref[...]))


def _ffn_call(x, g, sh, sc, w13, w2, gn, gt, lv, split_rows, tm_pref=512, tf_pref=512):
    t, d = x.shape
    f = w2.shape[0]
    tm = _tile(lv, tm_pref)
    tf = _tile(f, tf_pref, LANE)
    nf = f // tf
    row = lambda i, j: (i, 0)
    const2 = lambda i, j: (0, 0)
    mod = lambda i, j: (i * tm // lv, 0, 0)
    out_specs, out_shape, nba = _row_out_specs(t, d, tm, split_rows, 2)
    return pl.pallas_call(
        functools.partial(_ffn_kernel, nba=nba),
        grid=(t // tm, nf),
        in_specs=[
            pl.BlockSpec((tm, d), row),
            pl.BlockSpec((1, d), const2),
            pl.BlockSpec((1, 1, d), mod),
            pl.BlockSpec((1, 1, d), mod),
            pl.BlockSpec((d, tf), lambda i, j: (0, j)),
            pl.BlockSpec((d, tf), lambda i, j: (0, nf + j)),
            pl.BlockSpec((tf, d), lambda i, j: (j, 0)),
            pl.BlockSpec((1, d), const2),
            pl.BlockSpec((1, 1, d), mod),
        ],
        out_specs=out_specs,
        out_shape=out_shape,
        scratch_shapes=[pltpu.VMEM((tm, d), MXU_DTYPE), pltpu.VMEM((tm, d), F32)],
        compiler_params=_cparams(("parallel", "arbitrary")),
        name="ffn",
    )(x, g, sh, sc, w13, w13, w2, gn, gt)


ROUTE_ROWS = 8
ROW_DMA_UNROLL = 8


def _pack_rows(x):
    half = x.shape[1] // 2
    bits = pltpu.bitcast(x.astype(jnp.bfloat16).astype(F32), jnp.uint32)
    return (bits[:, :half] >> 16) | bits[:, half:]


def _unpack_rows(p):
    lo = pltpu.bitcast(p << 16, F32)
    hi = pltpu.bitcast(p & jnp.uint32(0xFFFF0000), F32)
    return lo, hi


def _route_kernel(x_ref, g_ref, sh_ref, sc_ref, rw_ref, tri_ref, route_ref, rt_ref, cnt_ref, xs_hbm,
                  h_ref, run_ref, idx_ref, cntv_ref, cnts_ref, zrow_ref, sem_idx, sem, *, n_experts, cap, tb, tm):
    step = pl.program_id(0)
    last = pl.num_programs(0) - 1
    slot = lax.rem(step, 2)

    def row_copy(sl, k, dst_row):
        return pltpu.make_async_copy(h_ref.at[sl].at[pl.ds(k, 1)], xs_hbm.at[pl.ds(dst_row, 1)], sem.at[sl])

    def drain(sl):
        def body(k, carry):
            row_copy(sl, 0, 0).wait()
            row_copy(sl, 0, 0).wait()
            return carry

        lax.fori_loop(0, tb, body, 0, unroll=ROW_DMA_UNROLL)

    def compute():
        h = _norm_mod(x_ref[...], g_ref[...], sh_ref[0], sc_ref[0])
        h_hi = h.astype(MXU_DTYPE)
        h_lo = (h - h_hi.astype(F32)).astype(MXU_DTYPE)
        logits = _dot(h_hi, rw_ref[0]) + _dot(h_hi, rw_ref[1]) + _dot(h_lo, rw_ref[0])
        lane = lax.broadcasted_iota(jnp.int32, logits.shape, 1)
        neg_inf = -jnp.inf
        l1 = jnp.where(lane < n_experts, logits, neg_inf)
        m1 = jnp.max(l1, axis=-1, keepdims=True)
        i1 = jnp.min(jnp.where(l1 == m1, lane, LANE), axis=-1, keepdims=True)
        l2 = jnp.where(lane == i1, neg_inf, l1)
        m2 = jnp.max(l2, axis=-1, keepdims=True)
        i2 = jnp.min(jnp.where(l2 == m2, lane, LANE), axis=-1, keepdims=True)
        e = jnp.exp(m2 - m1)
        w0 = 1.0 / (1.0 + e)
        w1 = e * w0
        sel1 = lane == i1
        sel2 = lane == i2
        member = jnp.where(sel1, 1.0, jnp.where(sel2, 1.0, 0.0))
        rank = _dot(tri_ref[...], member.astype(MXU_DTYPE))
        base = run_ref[...] + rank + lane.astype(F32) * float(cap)
        pos0 = jnp.sum(jnp.where(sel1, base, 0.0), axis=-1, keepdims=True)
        pos1 = jnp.sum(jnp.where(sel2, base, 0.0), axis=-1, keepdims=True)
        route = jnp.where(lane == 0, pos0,
                          jnp.where(lane == 1, pos1, jnp.where(lane == 2, w0, jnp.where(lane == 3, w1, 0.0))))
        route_ref[...] = route
        rt_ref[...] = route.T[0:ROUTE_ROWS, :].astype(jnp.int32)
        run_ref[...] += jnp.sum(member, axis=0, keepdims=True)
        cnt_ref[...] = run_ref[...]
        h_ref[slot] = _pack_rows(h)

    @pl.when(step == 0)
    def _():
        run_ref[...] = jnp.zeros_like(run_ref)
        compute()

    @pl.when(step > 0)
    def _():
        @pl.when(step > 1)
        def _():
            drain(slot)

        for k in range(tb):
            row_copy(1 - slot, k, idx_ref[1 - slot, 0, k]).start(priority=0)
            row_copy(1 - slot, k, idx_ref[1 - slot, 1, k]).start(priority=1)
        compute()

    cp = pltpu.make_async_copy(rt_ref, idx_ref.at[slot], sem_idx)
    cp.start()
    cp.wait()

    @pl.when(step == last)
    def _():
        def issue(k, carry):
            row_copy(slot, k, idx_ref[slot, 0, k]).start()
            row_copy(slot, k, idx_ref[slot, 1, k]).start()
            return carry

        lax.fori_loop(0, tb, issue, 0, unroll=ROW_DMA_UNROLL)

        @pl.when(step > 0)
        def _():
            drain(1 - slot)

        drain(slot)
        zrow_ref[...] = jnp.zeros_like(zrow_ref)
        cntv_ref[...] = jnp.broadcast_to(run_ref[...], cntv_ref.shape).astype(jnp.int32)
        cpc = pltpu.make_async_copy(cntv_ref, cnts_ref, sem_idx)
        cpc.start()
        cpc.wait()

        def zero_copy(dst_row):
            return pltpu.make_async_copy(zrow_ref.at[pl.ds(0, 1)], xs_hbm.at[pl.ds(dst_row, 1)], sem.at[0])

        for e in range(n_experts):
            cnt = cnts_ref[0, e]
            end = ((cnt + tm - 1) // tm) * tm

            def zissue(r, carry, e=e):
                zero_copy(e * cap + r).start()
                return carry

            def zdrain(r, carry):
                zero_copy(0).wait()
                return carry

            lax.fori_loop(cnt, end, zissue, 0)
            lax.fori_loop(cnt, end, zdrain, 0)


def _route_call(x, g, sh, sc, router_w, lv, cap, tm, tb_pref=256):
    t, d = x.shape
    n_experts = router_w.shape[1]
    tb = _tile(lv, tb_pref, LANE)
    rw = jnp.zeros((d, LANE), F32).at[:, :n_experts].set(router_w)
    rw_hi = rw.astype(MXU_DTYPE)
    rw = jnp.stack([rw_hi, (rw - rw_hi.astype(F32)).astype(MXU_DTYPE)])
    idx = np.arange(tb)
    tri = jnp.asarray((idx[None, :] < idx[:, None]).astype(np.float32), MXU_DTYPE)
    row = lambda i: (i, 0)
    const2 = lambda i: (0, 0)
    mod = lambda i: (i * tb // lv, 0, 0)
    return pl.pallas_call(
        functools.partial(_route_kernel, n_experts=n_experts, cap=cap, tb=tb, tm=tm),
        grid=(t // tb,),
        in_specs=[
            pl.BlockSpec((tb, d), row),
            pl.BlockSpec((1, d), const2),
            pl.BlockSpec((1, 1, d), mod),
            pl.BlockSpec((1, 1, d), mod),
            pl.BlockSpec((2, d, LANE), lambda i: (0, 0, 0)),
            pl.BlockSpec((tb, tb), const2),
        ],
        out_specs=[
            pl.BlockSpec((tb, LANE), row),
            pl.BlockSpec((ROUTE_ROWS, tb), lambda i: (0, i)),
            pl.BlockSpec((1, LANE), const2),
            pl.BlockSpec(memory_space=pl.ANY),
        ],
        out_shape=[
            jax.ShapeDtypeStruct((t, LANE), F32),
            jax.ShapeDtypeStruct((ROUTE_ROWS, t), jnp.int32),
            jax.ShapeDtypeStruct((1, LANE), F32),
            jax.ShapeDtypeStruct((n_experts * cap, d // 2), jnp.uint32),
        ],
        scratch_shapes=[
            pltpu.VMEM((2, tb, d // 2), jnp.uint32),
            pltpu.VMEM((1, LANE), F32),
            pltpu.SMEM((2, ROUTE_ROWS, tb), jnp.int32),
            pltpu.VMEM((SUBLANE, LANE), jnp.int32),
            pltpu.SMEM((SUBLANE, LANE), jnp.int32),
            pltpu.VMEM((SUBLANE, d // 2), jnp.uint32),
            pltpu.SemaphoreType.DMA,
            pltpu.SemaphoreType.DMA((2,)),
        ],
        compiler_params=_cparams(("arbitrary",)),
        name="route",
    )(x, g, sh, sc, rw, tri)


def _gmm_kernel(be_ref, br_ref, bv_ref, xs_ref, w1_ref, w3_ref, w2_ref, y_ref, xb_ref, acc_ref):
    i = pl.program_id(0)
    j = pl.program_id(1)

    @pl.when(bv_ref[i] == 1)
    def _():
        @pl.when(j == 0)
        def _():
            half = xs_ref.shape[1]
            lo, hi = _unpack_rows(xs_ref[...])
            xb_ref[:, :half] = lo.astype(xb_ref.dtype)
            xb_ref[:, half:] = hi.astype(xb_ref.dtype)
            acc_ref[...] = jnp.zeros_like(acc_ref)

        xb = xb_ref[...]
        a = _dot(xb, w1_ref[0])
        g = _dot(xb, w3_ref[0])
        m = (a * _sigmoid(a) * g).astype(MXU_DTYPE)
        acc_ref[...] += _dot(m, w2_ref[0])

        @pl.when(j == pl.num_programs(1) - 1)
        def _():
            y_ref[...] = _pack_rows(acc_ref[...])


def _gmm_call(blk_e, blk_row, blk_valid, xs, w13, w2, tm, tf_pref=1024):
    n_rows, half = xs.shape
    d = 2 * half
    f = w2.shape[1]
    tf = _tile(f, tf_pref, LANE)
    nf = f // tf
    n_blk = blk_e.shape[0]

    def jsel(i, j, bv):
        return jnp.where(bv[i] == 1, j, nf - 1)

    grid_spec = pltpu.PrefetchScalarGridSpec(
        num_scalar_prefetch=3,
        grid=(n_blk, nf),
        in_specs=[
            pl.BlockSpec((tm, half), lambda i, j, be, br, bv: (br[i], 0)),
            pl.BlockSpec((1, d, tf), lambda i, j, be, br, bv: (be[i], 0, jsel(i, j, bv))),
            pl.BlockSpec((1, d, tf), lambda i, j, be, br, bv: (be[i], 0, nf + jsel(i, j, bv))),
            pl.BlockSpec((1, tf, d), lambda i, j, be, br, bv: (be[i], jsel(i, j, bv), 0)),
        ],
        out_specs=pl.BlockSpec((tm, half), lambda i, j, be, br, bv: (br[i], 0)),
        scratch_shapes=[pltpu.VMEM((tm, d), MXU_DTYPE), pltpu.VMEM((tm, d), F32)],
    )
    return pl.pallas_call(
        _gmm_kernel,
        grid_spec=grid_spec,
        out_shape=jax.ShapeDtypeStruct((n_rows, half), jnp.uint32),
        compiler_params=_cparams(("arbitrary", "arbitrary")),
        name="moe_gmm",
    )(blk_e, blk_row, blk_valid, xs, w13, w13, w2)


def _combine_kernel(rt_ref, rtn_ref, route_ref, x_ref, gn_ref, gt_ref, y_hbm, *rest, tb, nba):
    *o_refs, idx_ref, y0_ref, y1_ref, sem_idx, sem = rest
    step = pl.program_id(0)
    slot = lax.rem(step, 2)

    def row_copy(src_row, dst_ref, sl, k, s):
        return pltpu.make_async_copy(y_hbm.at[pl.ds(src_row, 1)], dst_ref.at[sl].at[pl.ds(k, 1)], sem.at[sl, s])

    def load_table(table_ref):
        cp = pltpu.make_async_copy(table_ref, idx_ref, sem_idx)
        cp.start()
        cp.wait()

    def drain(sl):
        def body(k, carry):
            row_copy(0, y0_ref, sl, k, 0).wait()
            row_copy(0, y1_ref, sl, k, 1).wait()
            return carry

        lax.fori_loop(0, tb, body, 0, unroll=ROW_DMA_UNROLL)

    @pl.when(step == 0)
    def _():
        load_table(rt_ref)

        def issue(k, carry):
            row_copy(idx_ref[0, k], y0_ref, 0, k, 0).start()
            row_copy(idx_ref[1, k], y1_ref, 0, k, 1).start()
            return carry

        lax.fori_loop(0, tb, issue, 0, unroll=ROW_DMA_UNROLL)

    load_table(rtn_ref)
    drain(slot)
    for k in range(tb):
        row_copy(idx_ref[0, k], y0_ref, 1 - slot, k, 0).start(priority=0)
        row_copy(idx_ref[1, k], y1_ref, 1 - slot, k, 1).start(priority=1)
    route = route_ref[...]
    w0 = route[:, 2:3]
    w1 = route[:, 3:4]
    lo0, hi0 = _unpack_rows(y0_ref[slot])
    lo1, hi1 = _unpack_rows(y1_ref[slot])
    y = jnp.concatenate([w0 * lo0 + w1 * lo1, w0 * hi0 + w1 * hi1], axis=1)
    _row_out_store(o_refs, nba, x_ref[...] + gt_ref[0] * _rms(y, gn_ref[...]))

    @pl.when(step == pl.num_programs(0) - 1)
    def _():
        drain(1 - slot)


def _combine_call(rt, route, x, gn, gt, y, lv, split_rows, tb_pref=256):
    t, d = x.shape
    tb = _tile(lv, tb_pref, LANE)
    nblk = t // tb
    row = lambda i: (i, 0)
    out_specs, out_shape, nba = _row_out_specs(t, d, tb, split_rows, 1)
    return pl.pallas_call(
        functools.partial(_combine_kernel, tb=tb, nba=nba),
        grid=(nblk,),
        in_specs=[
            pl.BlockSpec((ROUTE_ROWS, tb), lambda i: (0, i)),
            pl.BlockSpec((ROUTE_ROWS, tb), lambda i: (0, jnp.minimum(i + 1, nblk - 1))),
            pl.BlockSpec((tb, LANE), row),
            pl.BlockSpec((tb, d), row),
            pl.BlockSpec((1, d), lambda i: (0, 0)),
            pl.BlockSpec((1, 1, d), lambda i: (i * tb // lv, 0, 0)),
            pl.BlockSpec(memory_space=pl.ANY),
        ],
        out_specs=out_specs,
        out_shape=out_shape,
        scratch_shapes=[
            pltpu.SMEM((ROUTE_ROWS, tb), jnp.int32),
            pltpu.VMEM((2, tb, d // 2), jnp.uint32),
            pltpu.VMEM((2, tb, d // 2), jnp.uint32),
            pltpu.SemaphoreType.DMA,
            pltpu.SemaphoreType.DMA((2, 2)),
        ],
        compiler_params=_cparams(("arbitrary",)),
        name="moe_combine",
    )(rt, rt, route, x, gn, gt, y)


def _moe(x, g, sh, sc, router_w, w13, w2, gn, gt, lv, split_rows=None, tm_pref=512):
    t, d = x.shape
    n_experts = router_w.shape[1]
    tm = _tile(t, tm_pref, LANE)
    cap = -(-t // tm) * tm
    route, rt, cnt, xs = _route_call(x, g, sh, sc, router_w, lv, cap, tm)
    counts = cnt[0, :n_experts].astype(jnp.int32)
    nb = (counts + tm - 1) // tm
    ends = jnp.cumsum(nb)
    n_blk = TOP_K * t // tm + n_experts
    bi = jnp.arange(n_blk, dtype=jnp.int32)
    valid = bi < ends[-1]
    bi_c = jnp.minimum(bi, ends[-1] - 1)
    be = jnp.minimum(jnp.searchsorted(ends, bi_c, side="right"), n_experts - 1).astype(jnp.int32)
    br = be * (cap // tm) + (bi_c - (ends - nb)[be])
    y = _gmm_call(be, br.astype(jnp.int32), valid.astype(jnp.int32), xs, w13, w2, tm)
    return _combine_call(rt, route, x, gn, gt, y, lv, split_rows)


def kernel(x_prompt, x_sample, c_prompt, c_sample, ada_w, ada_b, norm_g, w_in, na_rpb, hgrn_lb, hgrn_norm_g,
           conv_w, conv_b, conv_ln_g, conv_ln_b, w_branch, w_out, ffn_w13, ffn_w2, router_w, moe_w13, moe_w2):
    bp, lp, d = x_prompt.shape
    bs, ls, _ = x_sample.shape
    depth = ada_w.shape[0]
    bw = w_branch.shape[2]
    tp = bp * lp
    t = tp + bs * ls
    lv = math.gcd(lp, ls)
    n_vseq = t // lv
    assert 8 * bw % d == 0 and d % LANE == 0 and bw % LANE == 0

    x = jnp.concatenate([x_prompt.reshape(tp, d), x_sample.reshape(bs * ls, d)], axis=0)

    nb = bp + bs
    rows = -(-nb // SUBLANE) * SUBLANE
    c_all = jnp.zeros((rows, d), F32).at[:nb].set(jnp.concatenate([c_prompt, c_sample], axis=0))
    mod = _ada(c_all, ada_w, ada_b)
    starts = np.arange(n_vseq) * lv
    vb = np.where(starts < tp, starts // lp, bp + (starts - tp) // ls)
    mod_v = mod[:, vb, :].reshape(depth, n_vseq, 1, 6, d)
    part = lambda l, k: mod_v[l, :, :, k, :]

    sm = jax.nn.softmax(hgrn_lb.astype(F32), axis=0)
    lb_all = jnp.clip(jnp.cumsum(sm, axis=0) - sm[0], 0.0, 1.0)

    cb = lambda k: slice(k * bw, (k + 1) * bw)
    order = [0, 1, 2, 3, 4, 7, 8, 9]
    conv_tb = _tile(lv, 512)
    blk_start = np.arange(t // conv_tb) * conv_tb
    seq_pos = np.where(blk_start < tp, blk_start % lp, (blk_start - tp) % ls)
    seq_len_of = np.where(blk_start < tp, lp, ls)
    seq_starts = jnp.asarray((seq_pos == 0).astype(np.int32))
    seq_ends = jnp.asarray((seq_pos + conv_tb == seq_len_of).astype(np.int32))

    for l in range(depth):
        wl = w_in[l]
        w_all = jnp.concatenate([wl[:, cb(k)] for k in order] + [wl[:, 10 * bw:], wl[:, 5 * bw:7 * bw]],
                                axis=1).astype(MXU_DTYPE)
        u, uf = _nm_matmul(x, norm_g[l, 0].reshape(1, d), part(l, 0), part(l, 1), w_all, w_all.shape[1] - 2 * bw, lv)

        bias_tab = _na_bias_table(na_rpb[l])
        ya = _na_call(u, bias_tab, None, 0, bp, lp)
        ya = _na_call(u, bias_tab, ya, tp, bs, ls)

        o_fwd, o_rev = _hgrn_call(u, uf, lb_all[l], ((0, bp, lp), (tp, bs, ls)), (3, 4, 0, 1))

        yc = _conv_call(u, conv_w[l], conv_b[l], conv_ln_g[l], conv_ln_b[l], seq_starts, seq_ends, 6, 7, conv_tb)

        x = _merge_call(ya, o_fwd, o_rev, yc, u, 5, 8 * bw // d, hgrn_norm_g[l], w_branch[l].astype(MXU_DTYPE),
                        w_out[l].astype(MXU_DTYPE), x, norm_g[l, 1].reshape(1, d), part(l, 2), lv)

        g2 = norm_g[l, 2].reshape(1, d)
        g3 = norm_g[l, 3].reshape(1, d)
        split_rows = tp if l == depth - 1 else None
        if l % 2 == 0:
            x = _ffn_call(x, g2, part(l, 3), part(l, 4), ffn_w13[l // 2].astype(MXU_DTYPE),
                          ffn_w2[l // 2].astype(MXU_DTYPE), g3, part(l, 5), lv, split_rows)
        else:
            x = _moe(x, g2, part(l, 3), part(l, 4), router_w[l // 2], moe_w13[l // 2].astype(MXU_DTYPE),
                     moe_w2[l // 2].astype(MXU_DTYPE), g3, part(l, 5), lv, split_rows)

    y_prompt, y_sample = x
    return (y_prompt.reshape(bp, lp, d), y_sample.reshape(bs, ls, d))
```
